```python
import math
import jax, jax.numpy as jnp
from jax import lax
import numpy as np

D_MODEL = 4096
BATCH = 4
SEQ = 2048
DEPTH = 4
DEC_BATCH = 32
DEC_SEQ = 4
PAST_LEN = 8192
PAGE_SIZE = 128

N_BRANCH = 4
BRANCH_DIM = D_MODEL // N_BRANCH
CONV_DIM = BRANCH_DIM
CONV_WIDTH = 31
SWA_HEAD_DIM = 64
SWA_HEADS = BRANCH_DIM // SWA_HEAD_DIM
SWA_KV_HEADS = SWA_HEADS // 4
SWA_REP = SWA_HEADS // SWA_KV_HEADS
WINDOW = 128
BAND_BLOCK = WINDOW
N_BUCKETS = 32
MAX_DISTANCE = WINDOW
SSM_GROUP = 16
SSM_GROUPS = BRANCH_DIM // SSM_GROUP
SSM_STATE = 64
MEM_LEN = 256
MEM_HEADS = 4
MEM_HEAD_DIM = BRANCH_DIM // MEM_HEADS
D_FF = ((8 * D_MODEL + 3 * 256 - 1) // (3 * 256)) * 256

COL_CONV = 0
COL_Q = COL_CONV + 2 * CONV_DIM
COL_K = COL_Q + SWA_HEADS * SWA_HEAD_DIM
COL_V = COL_K + SWA_KV_HEADS * SWA_HEAD_DIM
COL_U = COL_V + SWA_KV_HEADS * SWA_HEAD_DIM
COL_XQ = COL_U + BRANCH_DIM
COL_GATE = COL_XQ + BRANCH_DIM
N_IN = COL_GATE + N_BRANCH * D_MODEL

NEG_INF = -1e30

kernel_name = 'hybrid_gated_conv_swa_s5_memxattn_step'


def rmsnorm(x, g, eps=1e-6):
    xf = x.astype(jnp.float32)
    y = xf * lax.rsqrt(jnp.mean(xf * xf, axis=-1, keepdims=True) + eps)
    return (y * g.astype(jnp.float32)).astype(x.dtype)


def layernorm(x, g, b, eps=1e-5):
    xf = x.astype(jnp.float32)
    mu = jnp.mean(xf, axis=-1, keepdims=True)
    xc = xf - mu
    y = xc * lax.rsqrt(jnp.mean(xc * xc, axis=-1, keepdims=True) + eps)
    return (y * g.astype(jnp.float32) + b.astype(jnp.float32)).astype(x.dtype)


def t5_bucket(dist):
    n = jnp.maximum(dist, 0)
    max_exact = N_BUCKETS // 2
    nf = jnp.maximum(n, 1).astype(jnp.float32)
    large = max_exact + (jnp.log(nf / max_exact) / math.log(MAX_DISTANCE / max_exact)
                         * (N_BUCKETS - max_exact)).astype(jnp.int32)
    large = jnp.minimum(large, N_BUCKETS - 1)
    return jnp.where(n < max_exact, n, large)


def conv_branch(z, buf, w, b, ln_g, ln_b):
    a = z[..., :CONV_DIM] * jax.nn.sigmoid(z[..., CONV_DIM:])
    xx = jnp.concatenate([buf.astype(a.dtype), a], axis=1)
    y = lax.conv_general_dilated(xx, w[:, None, :].astype(a.dtype), window_strides=(1,), padding='VALID',
                                 dimension_numbers=('NWC', 'WIO', 'NWC'),
                                 feature_group_count=CONV_DIM) + b
    y = jax.nn.silu(layernorm(y, ln_g, ln_b))
    return y, xx[:, -(CONV_WIDTH - 1):]


def band_attention(q, k, v, mask, dist, sinks, t5_bias):
    nq, nk = dist.shape
    logits = jnp.einsum('bnqgrd,bnkgd->bngrqk', q, k).astype(jnp.float32) * (SWA_HEAD_DIM ** -0.5)
    bias = t5_bias.astype(jnp.float32)[t5_bucket(dist)]
    bias = jnp.transpose(bias, (2, 0, 1)).reshape(SWA_KV_HEADS, SWA_REP, nq, nk)
    logits = jnp.where(mask[None, :, None, None], logits + bias, NEG_INF)
    s = sinks.astype(jnp.float32).reshape(1, 1, SWA_KV_HEADS, SWA_REP, 1, 1)
    sink_col = jnp.broadcast_to(s, logits.shape[:-1] + (1,))
    probs = jax.nn.softmax(jnp.concatenate([logits, sink_col], axis=-1), axis=-1)[..., :-1]
    return jnp.einsum('bngrqk,bnkgd->bnqgrd', probs.astype(v.dtype), v)


def swa_prompt(q, k, v, sinks, t5_bias):
    bsz, L = q.shape[:2]
    nb = L // BAND_BLOCK
    qb = q.reshape(bsz, nb, BAND_BLOCK, SWA_KV_HEADS, SWA_REP, SWA_HEAD_DIM)

    def band_keys(t):
        tb = t.reshape(bsz, nb, BAND_BLOCK, SWA_KV_HEADS, SWA_HEAD_DIM)
        prev = jnp.pad(tb[:, :-1], ((0, 0), (1, 0), (0, 0), (0, 0), (0, 0)))
        return jnp.concatenate([prev, tb], axis=2)

    qi = jnp.arange(BAND_BLOCK)
    ki = jnp.arange(2 * BAND_BLOCK) - BAND_BLOCK
    dist = qi[:, None] - ki[None, :]
    band = (dist >= 0) & (dist < WINDOW)
    valid = (jnp.arange(nb)[:, None] * BAND_BLOCK + ki[None, :]) >= 0
    mask = band[None] & valid[:, None, :]
    out = band_attention(qb, band_keys(k), band_keys(v), mask, dist, sinks, t5_bias)
    n_keep = min(WINDOW, L)
    return out.reshape(bsz, L, SWA_HEADS * SWA_HEAD_DIM), k[:, -n_keep:], v[:, -n_keep:]


def swa_sample(q, k, v, kbuf, vbuf, sinks, t5_bias):
    bsz, S = q.shape[:2]
    W = kbuf.shape[1]
    kk = jnp.concatenate([kbuf.astype(k.dtype), k], axis=1)
    vv = jnp.concatenate([vbuf.astype(v.dtype), v], axis=1)
    dist = (W + jnp.arange(S))[:, None] - jnp.arange(W + S)[None, :]
    mask = ((dist >= 0) & (dist < WINDOW))[None]
    qb = q.reshape(bsz, 1, S, SWA_KV_HEADS, SWA_REP, SWA_HEAD_DIM)
    out = band_attention(qb, kk[:, None], vv[:, None], mask, dist, sinks, t5_bias)
    return out.reshape(bsz, S, SWA_HEADS * SWA_HEAD_DIM), kk[:, -W:], vv[:, -W:]


def _complex_affine_combine(e1, e2):
    a1r, a1i, b1r, b1i = e1
    a2r, a2i, b2r, b2i = e2
    return (a2r * a1r - a2i * a1i, a2r * a1i + a2i * a1r,
            a2r * b1r - a2i * b1i + b2r, a2r * b1i + a2i * b1r + b2i)


def s5_branch(u, h0_re, h0_im, a_re, a_im, log_dt, b_re, b_im, c_re, c_im, d, w_glu):
    f32 = jnp.float32
    bsz, L, _ = u.shape
    ug = u.reshape(bsz, L, SSM_GROUPS, SSM_GROUP).astype(f32)
    dt = jnp.exp(log_dt.astype(f32))[:, None]
    ar, ai = a_re.astype(f32), a_im.astype(f32)
    mag = jnp.exp(dt * ar)
    abar_re, abar_im = mag * jnp.cos(dt * ai), mag * jnp.sin(dt * ai)
    den = ar * ar + ai * ai
    nr, ni = abar_re - 1.0, abar_im
    f_re, f_im = (nr * ar + ni * ai) / den, (ni * ar - nr * ai) / den
    br, bi = b_re.astype(f32), b_im.astype(f32)
    bbar_re = f_re[..., None] * br - f_im[..., None] * bi
    bbar_im = f_re[..., None] * bi + f_im[..., None] * br
    bu_re = jnp.einsum('blgc,gnc->blgn', ug, bbar_re)
    bu_im = jnp.einsum('blgc,gnc->blgn', ug, bbar_im)
    a_re_t = jnp.broadcast_to(abar_re, bu_re.shape)
    a_im_t = jnp.broadcast_to(abar_im, bu_re.shape)
    p_re, p_im, s_re, s_im = lax.associative_scan(_complex_affine_combine,
                                                  (a_re_t, a_im_t, bu_re, bu_im), axis=1)
    h0r = h0_re.astype(f32)[:, None]
    h0i = h0_im.astype(f32)[:, None]
    h_re = s_re + p_re * h0r - p_im * h0i
    h_im = s_im + p_re * h0i + p_im * h0r
    y = (jnp.einsum('blgn,gcn->blgc', h_re, c_re.astype(f32))
         - jnp.einsum('blgn,gcn->blgc', h_im, c_im.astype(f32))
         + d.astype(f32) * ug)
    y = jax.nn.gelu(y.reshape(bsz, L, BRANCH_DIM)).astype(u.dtype)
    out = y * jax.nn.sigmoid(y @ w_glu)
    return out, h_re[:, -1].astype(h0_re.dtype), h_im[:, -1].astype(h0_im.dtype)


def mem_kv(mem, g, w):
    bsz, M, _ = mem.shape
    kv = rmsnorm(mem, g) @ w
    k = kv[..., :BRANCH_DIM].reshape(bsz, M, MEM_HEADS, MEM_HEAD_DIM)
    v = kv[..., BRANCH_DIM:].reshape(bsz, M, MEM_HEADS, MEM_HEAD_DIM)
    return k, v


def mem_attend(q, mk, mv):
    logits = jnp.einsum('blhd,bmhd->bhlm', q, mk.astype(q.dtype)).astype(jnp.float32) * (MEM_HEAD_DIM ** -0.5)
    probs = jax.nn.softmax(logits, axis=-1)
    return jnp.einsum('bhlm,bmhd->blhd', probs.astype(q.dtype), mv.astype(q.dtype))


def trunk_layer(x, conv_buf, win_k, win_v, h0_re, h0_im, mk, mv, p, prompt_mode):
    bsz, L, _ = x.shape
    xn = rmsnorm(x, p['g_pre'])
    z = xn @ p['w_in']
    ya, conv_new = conv_branch(z[..., COL_CONV:COL_Q], conv_buf, p['conv_w'], p['conv_b'],
                               p['ln_g'], p['ln_b'])
    q = z[..., COL_Q:COL_K].reshape(bsz, L, SWA_HEADS, SWA_HEAD_DIM)
    k = z[..., COL_K:COL_V].reshape(bsz, L, SWA_KV_HEADS, SWA_HEAD_DIM)
    v = z[..., COL_V:COL_U].reshape(bsz, L, SWA_KV_HEADS, SWA_HEAD_DIM)
    if prompt_mode:
        yb, wk_new, wv_new = swa_prompt(q, k, v, p['sinks'], p['t5'])
    else:
        yb, wk_new, wv_new = swa_sample(q, k, v, win_k, win_v, p['sinks'], p['t5'])
    yc, hr, hi = s5_branch(z[..., COL_U:COL_XQ], h0_re, h0_im, p['a_re'], p['a_im'], p['log_dt'],
                           p['b_re'], p['b_im'], p['c_re'], p['c_im'], p['d'], p['w_glu'])
    yx = mem_attend(z[..., COL_XQ:COL_GATE].reshape(bsz, L, MEM_HEADS, MEM_HEAD_DIM), mk, mv)
    yx = yx.reshape(bsz, L, BRANCH_DIM)
    merged = None
    for i, yi in enumerate((ya, yb, yc, yx)):
        gate = jax.nn.sigmoid(z[..., COL_GATE + i * D_MODEL:COL_GATE + (i + 1) * D_MODEL])
        term = gate * (yi @ p['w_branch'][i])
        merged = term if merged is None else merged + term
    x = x + rmsnorm(merged @ p['w_out'], p['g_post'])
    hn = rmsnorm(x, p['g_ffn_pre'])
    gu = hn @ p['w_ffn_in']
    f = (jax.nn.silu(gu[..., :D_FF]) * gu[..., D_FF:]) @ p['w_ffn_out']
    x = x + rmsnorm(f, p['g_ffn_post'])
    return x, conv_new, wk_new, wv_new, hr, hi


def setup_inputs(seed: int = 0) -> dict:
    key = jax.random.key(seed)
    ks = iter(jax.random.split(key, 48))
    f32 = jnp.float32

    def nrm(shape, scale):
        return jax.random.normal(next(ks), shape, f32) * scale

    n_win = min(WINDOW, PAST_LEN)
    a_im0 = jnp.broadcast_to(jnp.pi * jnp.arange(SSM_STATE, dtype=f32), (DEPTH, SSM_GROUPS, SSM_STATE))
    return {
        'x_prompt': nrm((BATCH, SEQ, D_MODEL), 1.0),
        'x_sample': nrm((DEC_BATCH, DEC_SEQ, D_MODEL), 1.0),
        'cache_conv': nrm((DEPTH, DEC_BATCH, CONV_WIDTH - 1, CONV_DIM), 0.5),
        'cache_win_k': nrm((DEPTH, DEC_BATCH, n_win, SWA_KV_HEADS, SWA_HEAD_DIM), 1.0),
        'cache_win_v': nrm((DEPTH, DEC_BATCH, n_win, SWA_KV_HEADS, SWA_HEAD_DIM), 1.0),
        'state_ssm_re': nrm((DEPTH, DEC_BATCH, SSM_GROUPS, SSM_STATE), 0.3),
        'state_ssm_im': nrm((DEPTH, DEC_BATCH, SSM_GROUPS, SSM_STATE), 0.3),
        'cache_mem_k': nrm((DEPTH, DEC_BATCH, MEM_LEN, MEM_HEADS, MEM_HEAD_DIM), 1.0),
        'cache_mem_v': nrm((DEPTH, DEC_BATCH, MEM_LEN, MEM_HEADS, MEM_HEAD_DIM), 1.0),
        'mem_prompt': nrm((BATCH, MEM_LEN, D_MODEL), 1.0),
        't5_bias': nrm((N_BUCKETS, SWA_HEADS), 0.5),
        'norm_mix_pre': 1.0 + nrm((DEPTH, D_MODEL), 0.02),
        'norm_mix_post': 1.0 + nrm((DEPTH, D_MODEL), 0.02),
        'norm_ffn_pre': 1.0 + nrm((DEPTH, D_MODEL), 0.02),
        'norm_ffn_post': 1.0 + nrm((DEPTH, D_MODEL), 0.02),
        'norm_mem': 1.0 + nrm((DEPTH, D_MODEL), 0.02),
        'w_in': nrm((DEPTH, D_MODEL, N_IN), D_MODEL ** -0.5),
        'conv_w': nrm((DEPTH, CONV_WIDTH, CONV_DIM), CONV_WIDTH ** -0.5),
        'conv_b': nrm((DEPTH, CONV_DIM), 0.01),
        'conv_ln_g': 1.0 + nrm((DEPTH, CONV_DIM), 0.02),
        'conv_ln_b': nrm((DEPTH, CONV_DIM), 0.01),
        'attn_sinks': nrm((DEPTH, SWA_HEADS), 1.0),
        'ssm_a_re': -0.5 + nrm((DEPTH, SSM_GROUPS, SSM_STATE), 0.01),
        'ssm_a_im': a_im0 + nrm((DEPTH, SSM_GROUPS, SSM_STATE), 0.01),
        'ssm_log_dt': jax.random.uniform(next(ks), (DEPTH, SSM_GROUPS), f32, math.log(1e-3), math.log(1e-1)),
        'ssm_b_re': nrm((DEPTH, SSM_GROUPS, SSM_STATE, SSM_GROUP), (2 * SSM_GROUP) ** -0.5),
        'ssm_b_im': nrm((DEPTH, SSM_GROUPS, SSM_STATE, SSM_GROUP), (2 * SSM_GROUP) ** -0.5),
        'ssm_c_re': nrm((DEPTH, SSM_GROUPS, SSM_GROUP, SSM_STATE), (2 * SSM_STATE) ** -0.5),
        'ssm_c_im': nrm((DEPTH, SSM_GROUPS, SSM_GROUP, SSM_STATE), (2 * SSM_STATE) ** -0.5),
        'ssm_d': nrm((DEPTH, SSM_GROUPS, SSM_GROUP), 1.0),
        'ssm_w_glu': nrm((DEPTH, BRANCH_DIM, BRANCH_DIM), BRANCH_DIM ** -0.5),
        'w_mem_kv': nrm((DEPTH, D_MODEL, 2 * BRANCH_DIM), D_MODEL ** -0.5),
        'w_branch': nrm((DEPTH, N_BRANCH, BRANCH_DIM, D_MODEL), BRANCH_DIM ** -0.5),
        'w_out': nrm((DEPTH, D_MODEL, D_MODEL), D_MODEL ** -0.5),
        'w_ffn_in': nrm((DEPTH, D_MODEL, 2 * D_FF), D_MODEL ** -0.5),
        'w_ffn_out': nrm((DEPTH, D_FF, D_MODEL), D_FF ** -0.5),
    }


def reference(x_prompt, x_sample, cache_conv, cache_win_k, cache_win_v, state_ssm_re, state_ssm_im,
              cache_mem_k, cache_mem_v, mem_prompt, t5_bias, norm_mix_pre, norm_mix_post,
              norm_ffn_pre, norm_ffn_post, norm_mem, w_in, conv_w, conv_b, conv_ln_g, conv_ln_b,
              attn_sinks, ssm_a_re, ssm_a_im, ssm_log_dt, ssm_b_re, ssm_b_im, ssm_c_re, ssm_c_im,
              ssm_d, ssm_w_glu, w_mem_kv, w_branch, w_out, w_ffn_in, w_ffn_out):
    bp = x_prompt.shape[0]
    conv0 = jnp.zeros((bp, CONV_WIDTH - 1, CONV_DIM), x_prompt.dtype)
    h0 = jnp.zeros((bp, SSM_GROUPS, SSM_STATE), state_ssm_re.dtype)
    hp, hs = x_prompt, x_sample
    conv_p, conv_s, wk_p, wv_p, wk_s, wv_s = [], [], [], [], [], []
    sr_p, si_p, sr_s, si_s, mk_p_all, mv_p_all = [], [], [], [], [], []
    for l in range(DEPTH):
        p = {
            'g_pre': norm_mix_pre[l], 'g_post': norm_mix_post[l],
            'g_ffn_pre': norm_ffn_pre[l], 'g_ffn_post': norm_ffn_post[l],
            'w_in': w_in[l], 'conv_w': conv_w[l], 'conv_b': conv_b[l],
            'ln_g': conv_ln_g[l], 'ln_b': conv_ln_b[l],
            'sinks': attn_sinks[l], 't5': t5_bias,
            'a_re': ssm_a_re[l], 'a_im': ssm_a_im[l], 'log_dt': ssm_log_dt[l],
            'b_re': ssm_b_re[l], 'b_im': ssm_b_im[l], 'c_re': ssm_c_re[l], 'c_im': ssm_c_im[l],
            'd': ssm_d[l], 'w_glu': ssm_w_glu[l],
            'w_branch': w_branch[l], 'w_out': w_out[l],
            'w_ffn_in': w_ffn_in[l], 'w_ffn_out': w_ffn_out[l],
        }
        mk_p, mv_p = mem_kv(mem_prompt, norm_mem[l], w_mem_kv[l])
        hp, c1, k1, v1, r1, i1 = trunk_layer(hp, conv0, None, None, h0, h0, mk_p, mv_p, p, True)
        hs, c2, k2, v2, r2, i2 = trunk_layer(hs, cache_conv[l], cache_win_k[l], cache_win_v[l],
                                             state_ssm_re[l], state_ssm_im[l],
                                             cache_mem_k[l], cache_mem_v[l], p, False)
        conv_p.append(c1); conv_s.append(c2)
        wk_p.append(k1); wv_p.append(v1); wk_s.append(k2); wv_s.append(v2)
        sr_p.append(r1); si_p.append(i1); sr_s.append(r2); si_s.append(i2)
        mk_p_all.append(mk_p); mv_p_all.append(mv_p)
    return (hp, hs,
            jnp.stack(conv_p), jnp.stack(conv_s),
            jnp.stack(wk_p), jnp.stack(wv_p), jnp.stack(wk_s), jnp.stack(wv_s),
            jnp.stack(sr_p), jnp.stack(si_p), jnp.stack(sr_s), jnp.stack(si_s),
            jnp.stack(mk_p_all), jnp.stack(mv_p_all))
```

```python
import functools
import math

import jax
import jax.numpy as jnp
import numpy as np
from jax import lax
from jax.experimental import pallas as pl
from jax.experimental.pallas import tpu as pltpu

F32 = jnp.float32
BF16 = jnp.bfloat16

D_MODEL = 4096
DEPTH = 4
BRANCH_DIM = 1024
CONV_WIDTH = 31
SWA_HEAD_DIM = 64
SWA_HEADS = 16
SWA_KV_HEADS = 4
SWA_REP = 4
WINDOW = 128
N_BUCKETS = 32
SSM_GROUP = 16
SSM_GROUPS = 64
SSM_STATE = 64
MEM_HEADS = 4
MEM_HEAD_DIM = 256
D_FF = 11008
COL_Q = 2048
COL_K = 3072
COL_V = 3328
COL_U = 3584
COL_XQ = 4608
COL_GATE = 5632
N_IN = COL_GATE + 4 * D_MODEL
NEG_INF = -1e30

SSM_CHUNKS = 4
SSM_CW = SSM_GROUPS // SSM_CHUNKS * SSM_STATE
SSM_UW = SSM_GROUPS // SSM_CHUNKS * SSM_GROUP
SCAN_LANES = 8


def _cp(sem, vmem_mb):
    return pltpu.CompilerParams(dimension_semantics=sem, vmem_limit_bytes=vmem_mb << 20)


def _rmsnorm_kernel(x_ref, g_ref, o_ref):
    x = x_ref[...]
    y = x * lax.rsqrt(jnp.mean(x * x, axis=-1, keepdims=True) + 1e-6)
    o_ref[...] = (y * g_ref[...]).astype(o_ref.dtype)


def rmsnorm_bf16(x, g, tr):
    m, d = x.shape
    return pl.pallas_call(
        _rmsnorm_kernel,
        grid=(m // tr,),
        in_specs=[pl.BlockSpec((tr, d), lambda i: (i, 0)), pl.BlockSpec((1, d), lambda i: (0, 0))],
        out_specs=pl.BlockSpec((tr, d), lambda i: (i, 0)),
        out_shape=jax.ShapeDtypeStruct((m, d), BF16),
        compiler_params=_cp(("parallel",), 40),
        name="rmsnorm",
    )(x, g.reshape(1, d))


def _resid_norm_kernel(x_ref, y_ref, gp_ref, gn_ref, xo_ref, hn_ref):
    y = y_ref[...]
    yn = y * lax.rsqrt(jnp.mean(y * y, axis=-1, keepdims=True) + 1e-6) * gp_ref[...]
    x = x_ref[...] + yn
    xo_ref[...] = x
    h = x * lax.rsqrt(jnp.mean(x * x, axis=-1, keepdims=True) + 1e-6)
    hn_ref[...] = (h * gn_ref[...]).astype(hn_ref.dtype)


def resid_norm(x, y, g_post, g_next, tr):
    m, d = x.shape
    row = pl.BlockSpec((tr, d), lambda i: (i, 0))
    vec = pl.BlockSpec((1, d), lambda i: (0, 0))
    return pl.pallas_call(
        _resid_norm_kernel,
        grid=(m // tr,),
        in_specs=[row, row, vec, vec],
        out_specs=[row, row],
        out_shape=[jax.ShapeDtypeStruct((m, d), F32), jax.ShapeDtypeStruct((m, d), BF16)],
        compiler_params=_cp(("parallel",), 48),
        name="resid_norm",
    )(x, y, g_post.reshape(1, d), g_next.reshape(1, d))


def _mm_kernel(a_ref, w_ref, o_ref):
    o_ref[...] = jnp.dot(a_ref[...], w_ref[...], preferred_element_type=F32).astype(o_ref.dtype)


def matmul(a, w, out_dtype, tm, tn, vmem_mb=56):
    m, k = a.shape
    n = w.shape[1]
    return pl.pallas_call(
        _mm_kernel,
        grid=(m // tm, n // tn),
        in_specs=[pl.BlockSpec((tm, k), lambda i, j: (i, 0)), pl.BlockSpec((k, tn), lambda i, j: (0, j))],
        out_specs=pl.BlockSpec((tm, tn), lambda i, j: (i, j)),
        out_shape=jax.ShapeDtypeStruct((m, n), out_dtype),
        compiler_params=_cp(("parallel", "parallel"), vmem_mb),
        name="matmul",
    )(a, w)


def _ffn_in_kernel(a_ref, wg_ref, wu_ref, o_ref):
    a = a_ref[...]
    g = jnp.dot(a, wg_ref[...], preferred_element_type=F32)
    u = jnp.dot(a, wu_ref[...], preferred_element_type=F32)
    o_ref[...] = (jax.nn.silu(g) * u).astype(o_ref.dtype)


def ffn_in(a, w, tm, tn):
    m, k = a.shape
    nt = D_FF // tn
    return pl.pallas_call(
        _ffn_in_kernel,
        grid=(m // tm, nt),
        in_specs=[pl.BlockSpec((tm, k), lambda i, j: (i, 0)),
                  pl.BlockSpec((k, tn), lambda i, j: (0, j)),
                  pl.BlockSpec((k, tn), lambda i, j: (0, j + nt))],
        out_specs=pl.BlockSpec((tm, tn), lambda i, j: (i, j)),
        out_shape=jax.ShapeDtypeStruct((m, D_FF), BF16),
        compiler_params=_cp(("parallel", "parallel"), 56),
        name="ffn_in",
    )(a, w, w)


def _merge_kernel(y_ref, wb_ref, g0_ref, g1_ref, g2_ref, g3_ref, o_ref):
    acc = None
    for i, g_ref in enumerate((g0_ref, g1_ref, g2_ref, g3_ref)):
        term = jax.nn.sigmoid(g_ref[...]) * jnp.dot(y_ref[i], wb_ref[i], preferred_element_type=F32)
        acc = term if acc is None else acc + term
    o_ref[...] = acc.astype(o_ref.dtype)


def merge_branches(y_all, wb, z, tm, tn):
    nb, m, kb = y_all.shape
    gate0 = COL_GATE // tn
    per = D_MODEL // tn
    gate_specs = [pl.BlockSpec((tm, tn), functools.partial(lambda i, j, b: (i, gate0 + b * per + j), b=b))
                  for b in range(nb)]
    return pl.pallas_call(
        _merge_kernel,
        grid=(m // tm, D_MODEL // tn),
        in_specs=[pl.BlockSpec((nb, tm, kb), lambda i, j: (0, i, 0)),
                  pl.BlockSpec((nb, kb, tn), lambda i, j: (0, 0, j))] + gate_specs,
        out_specs=pl.BlockSpec((tm, tn), lambda i, j: (i, j)),
        out_shape=jax.ShapeDtypeStruct((m, D_MODEL), BF16),
        compiler_params=_cp(("parallel", "parallel"), 56),
        name="merge",
    )(y_all, wb, z, z, z, z)


def _glu_kernel(y_ref, w_ref, o_ref):
    y = y_ref[...]
    s = jnp.dot(y.astype(BF16), w_ref[...], preferred_element_type=F32)
    o_ref[...] = (y * jax.nn.sigmoid(s)).astype(o_ref.dtype)


def glu(y, w, tr):
    m, d = y.shape
    return pl.pallas_call(
        _glu_kernel,
        grid=(m // tr,),
        in_specs=[pl.BlockSpec((tr, d), lambda i: (i, 0)), pl.BlockSpec((d, d), lambda i: (0, 0))],
        out_specs=pl.BlockSpec((tr, d), lambda i: (i, 0)),
        out_shape=jax.ShapeDtypeStruct((m, d), BF16),
        compiler_params=_cp(("parallel",), 40),
        name="glu",
    )(y, w)


def _ln_silu(y, g, b):
    mu = jnp.mean(y, axis=-1, keepdims=True)
    yc = y - mu
    yn = yc * lax.rsqrt(jnp.mean(yc * yc, axis=-1, keepdims=True) + 1e-5)
    return jax.nn.silu(yn * g + b)


CONV_HALO = 32
CONV_LANES = 128


def _conv_prompt_kernel(z_ref, w_ref, b_ref, g_ref, beta_ref, y_ref, cn_ref, xx, acc, *, tt):
    c = CONV_DIM_
    off = CONV_HALO - (CONV_WIDTH - 1)

    @pl.when(pl.program_id(1) == 0)
    def _():
        xx[0:CONV_HALO, :] = jnp.zeros((CONV_HALO, c), F32)

    xx[CONV_HALO:CONV_HALO + tt, :] = z_ref[:, :c] * jax.nn.sigmoid(z_ref[:, c:])
    for lc in range(c // CONV_LANES):
        ls = slice(lc * CONV_LANES, (lc + 1) * CONV_LANES)
        a = None
        for w in range(CONV_WIDTH):
            term = xx[off + w:off + w + tt, ls] * w_ref[w:w + 1, ls]
            a = term if a is None else a + term
        acc[:, ls] = a
    y_ref[...] = _ln_silu(acc[...] + b_ref[...], g_ref[...], beta_ref[...]).astype(y_ref.dtype)
    cn_ref[...] = xx[tt + off:tt + CONV_HALO, :]
    xx[0:CONV_HALO, :] = xx[tt:tt + CONV_HALO, :]


CONV_DIM_ = BRANCH_DIM


def conv_prompt(z, nb, seq, w, b, g, beta, tt):
    c = CONV_DIM_
    nt = seq // tt
    vec = pl.BlockSpec((1, c), lambda bi, t: (0, 0))
    return pl.pallas_call(
        functools.partial(_conv_prompt_kernel, tt=tt),
        grid=(nb, nt),
        in_specs=[pl.BlockSpec((tt, 2 * c), lambda bi, t: (bi * nt + t, 0)),
                  pl.BlockSpec((CONV_WIDTH, c), lambda bi, t: (0, 0)), vec, vec, vec],
        out_specs=[pl.BlockSpec((tt, c), lambda bi, t: (bi * nt + t, 0)),
                   pl.BlockSpec((None, CONV_WIDTH - 1, c), lambda bi, t: (bi, 0, 0))],
        out_shape=[jax.ShapeDtypeStruct((nb * seq, c), BF16),
                   jax.ShapeDtypeStruct((nb, CONV_WIDTH - 1, c), F32)],
        scratch_shapes=[pltpu.VMEM((tt + CONV_HALO, c), F32), pltpu.VMEM((tt, c), F32)],
        compiler_params=_cp(("parallel", "arbitrary"), 40),
        name="conv_prompt",
    )(z, w, b.reshape(1, c), g.reshape(1, c), beta.reshape(1, c))


def _conv_sample_kernel(z_ref, cache_ref, w_ref, b_ref, g_ref, beta_ref, y_ref, cn_ref, *, steps, nb):
    c = CONV_DIM_
    hist = CONV_WIDTH - 1
    a = z_ref[:, :c] * jax.nn.sigmoid(z_ref[:, c:])
    for t in range(steps):
        acc = None
        for w in range(CONV_WIDTH):
            idx = t + w
            src = cache_ref[idx] if idx < hist else a[(idx - hist) * nb:(idx - hist + 1) * nb]
            term = src * w_ref[w:w + 1, :]
            acc = term if acc is None else acc + term
        y_ref[t * nb:(t + 1) * nb, :] = _ln_silu(acc + b_ref[...], g_ref[...], beta_ref[...]).astype(y_ref.dtype)
    for r in range(hist - steps):
        cn_ref[r] = cache_ref[r + steps]
    for t in range(steps):
        cn_ref[hist - steps + t] = a[t * nb:(t + 1) * nb]


def conv_sample(z, row0, steps, nb, cache_t, w, b, g, beta):
    c = CONV_DIM_
    rows = steps * nb
    hist = CONV_WIDTH - 1
    vec = pl.BlockSpec((1, c), lambda i: (0, 0))
    return pl.pallas_call(
        functools.partial(_conv_sample_kernel, steps=steps, nb=nb),
        grid=(1,),
        in_specs=[pl.BlockSpec((rows, 2 * c), lambda i: (row0 // rows, 0)),
                  pl.BlockSpec((hist, nb, c), lambda i: (0, 0, 0)),
                  pl.BlockSpec((CONV_WIDTH, c), lambda i: (0, 0)), vec, vec, vec],
        out_specs=[pl.BlockSpec((rows, c), lambda i: (0, 0)),
                   pl.BlockSpec((hist, nb, c), lambda i: (0, 0, 0))],
        out_shape=[jax.ShapeDtypeStruct((rows, c), BF16), jax.ShapeDtypeStruct((hist, nb, c), F32)],
        compiler_params=_cp(("arbitrary",), 40),
        name="conv_sample",
    )(z, cache_t, w, b.reshape(1, c), g.reshape(1, c), beta.reshape(1, c))


def _t5_bucket_np(dist):
    n = np.maximum(dist, 0)
    max_exact = N_BUCKETS // 2
    nf = np.maximum(n, 1).astype(np.float32)
    large = max_exact + (np.log(nf / np.float32(max_exact)) / np.float32(math.log(WINDOW / max_exact))
                         * np.float32(N_BUCKETS - max_exact)).astype(np.int32)
    large = np.minimum(large, N_BUCKETS - 1)
    return np.where(n < max_exact, n, large)


def _bias_kernel(t5_ref, bucket_ref, o_ref):
    h = pl.program_id(0)
    bucket = bucket_ref[...]
    acc = jnp.full(bucket.shape, NEG_INF, F32)
    for b in range(N_BUCKETS):
        acc = jnp.where(bucket == b, t5_ref[b, h], acc)
    o_ref[...] = acc


def band_bias(t5_bias, dist, mask):
    bucket = np.where(mask, _t5_bucket_np(dist), -1).astype(np.int32)
    nq, nk = bucket.shape
    return pl.pallas_call(
        _bias_kernel,
        grid=(SWA_HEADS,),
        in_specs=[pl.BlockSpec(memory_space=pltpu.SMEM), pl.BlockSpec((nq, nk), lambda h: (0, 0))],
        out_specs=pl.BlockSpec((None, nq, nk), lambda h: (h, 0, 0)),
        out_shape=jax.ShapeDtypeStruct((SWA_HEADS, nq, nk), F32),
        compiler_params=_cp(("arbitrary",), 16),
        name="band_bias",
    )(t5_bias, jnp.asarray(bucket))


def _softmax_sink_pv(s, sink, v):
    m = jnp.maximum(jnp.max(s, axis=-1, keepdims=True), sink)
    p = jnp.exp(s - m)
    den = jnp.sum(p, axis=-1, keepdims=True) + jnp.exp(sink - m)
    return jnp.dot(p.astype(BF16), v, preferred_element_type=F32) / den


def _swa_prompt_kernel(sink_ref, q_ref, kp_ref, kc_ref, vp_ref, vc_ref, bias_ref, o_ref):
    blk = WINDOW
    scale = SWA_HEAD_DIM ** -0.5
    k = jnp.concatenate([kp_ref[...], kc_ref[...]], axis=0)
    v = jnp.concatenate([vp_ref[...], vc_ref[...]], axis=0)
    col = lax.broadcasted_iota(jnp.int32, (blk, 2 * blk), 1)
    no_prev = jnp.logical_and(pl.program_id(1) == 0, col < blk)
    q = q_ref[...]
    for hp in range(SWA_HEADS // 2):
        outs = []
        for h in (2 * hp, 2 * hp + 1):
            g = h // SWA_REP
            hs = slice(h * SWA_HEAD_DIM, (h + 1) * SWA_HEAD_DIM)
            gs = slice(g * SWA_HEAD_DIM, (g + 1) * SWA_HEAD_DIM)
            s = lax.dot_general(q[:, hs].astype(BF16), k[:, gs].astype(BF16), (((1,), (1,)), ((), ())),
                                preferred_element_type=F32) * scale
            s = jnp.where(no_prev, NEG_INF, s + bias_ref[h])
            outs.append(_softmax_sink_pv(s, sink_ref[h], v[:, gs].astype(BF16)))
        o_ref[:, 2 * hp * SWA_HEAD_DIM:(2 * hp + 2) * SWA_HEAD_DIM] = (
            jnp.concatenate(outs, axis=1).astype(o_ref.dtype))


def swa_prompt(z, nb, seq, sinks, bias):
    blk = WINDOW
    nj = seq // blk
    qw = SWA_HEADS * SWA_HEAD_DIM
    kw = SWA_KV_HEADS * SWA_HEAD_DIM
    cur = lambda col: (lambda b, j: (b * nj + j, col))
    prev = lambda col: (lambda b, j: (b * nj + jnp.maximum(j - 1, 0), col))
    return pl.pallas_call(
        _swa_prompt_kernel,
        grid=(nb, nj),
        in_specs=[pl.BlockSpec(memory_space=pltpu.SMEM),
                  pl.BlockSpec((blk, qw), cur(COL_Q // qw)),
                  pl.BlockSpec((blk, kw), prev(COL_K // kw)), pl.BlockSpec((blk, kw), cur(COL_K // kw)),
                  pl.BlockSpec((blk, kw), prev(COL_V // kw)), pl.BlockSpec((blk, kw), cur(COL_V // kw)),
                  pl.BlockSpec((SWA_HEADS, blk, 2 * blk), lambda b, j: (0, 0, 0))],
        out_specs=pl.BlockSpec((blk, qw), lambda b, j: (b * nj + j, 0)),
        out_shape=jax.ShapeDtypeStruct((nb * seq, qw), BF16),
        compiler_params=_cp(("parallel", "arbitrary"), 32),
        name="swa_prompt",
    )(sinks, z, z, z, z, z, bias)


SWA_SAMPLE_PAD = 4


def _swa_sample_kernel(q_ref, kc_ref, vc_ref, kn_ref, vn_ref, bias_ref, sink_ref, o_ref, wk_ref, wv_ref,
                       kk, vv, *, steps):
    win = WINDOW
    scale = SWA_HEAD_DIM ** -0.5
    kw = SWA_KV_HEADS * SWA_HEAD_DIM
    for buf, cache, new in ((kk, kc_ref, kn_ref), (vv, vc_ref, vn_ref)):
        buf[0:win, :] = cache[...]
        buf[win:win + steps, :] = new[...]
        buf[win + steps:win + steps + SWA_SAMPLE_PAD, :] = jnp.zeros((SWA_SAMPLE_PAD, kw), F32)
    wk_ref[...] = kk[steps:steps + win, :]
    wv_ref[...] = vv[steps:steps + win, :]
    for g in range(SWA_KV_HEADS):
        gs = slice(g * SWA_HEAD_DIM, (g + 1) * SWA_HEAD_DIM)
        s = lax.dot_general(q_ref[g].astype(BF16), kk[:, gs].astype(BF16), (((1,), (1,)), ((), ())),
                            preferred_element_type=F32) * scale + bias_ref[g]
        o_ref[g] = _softmax_sink_pv(s, sink_ref[g], vv[:, gs].astype(BF16))


def swa_sample(q, k_new, v_new, cache_k, cache_v, layer, bias, sink_col):
    nb, _, rows, _ = q.shape
    steps = k_new.shape[1]
    win = WINDOW
    kw = SWA_KV_HEADS * SWA_HEAD_DIM
    nk = win + steps + SWA_SAMPLE_PAD
    cache_spec = pl.BlockSpec((None, None, win, kw), lambda b: (layer, b, 0, 0))
    new_spec = pl.BlockSpec((None, steps, kw), lambda b: (b, 0, 0))
    win_spec = pl.BlockSpec((None, win, kw), lambda b: (b, 0, 0))
    qo_spec = pl.BlockSpec((None, SWA_KV_HEADS, rows, SWA_HEAD_DIM), lambda b: (b, 0, 0, 0))
    return pl.pallas_call(
        functools.partial(_swa_sample_kernel, steps=steps),
        grid=(nb,),
        in_specs=[qo_spec, cache_spec, cache_spec, new_spec, new_spec,
                  pl.BlockSpec((SWA_KV_HEADS, rows, nk), lambda b: (0, 0, 0)),
                  pl.BlockSpec((SWA_KV_HEADS, rows, 1), lambda b: (0, 0, 0))],
        out_specs=[qo_spec, win_spec, win_spec],
        out_shape=[jax.ShapeDtypeStruct(q.shape, F32),
                   jax.ShapeDtypeStruct((nb, win, kw), F32), jax.ShapeDtypeStruct((nb, win, kw), F32)],
        scratch_shapes=[pltpu.VMEM((nk, kw), F32), pltpu.VMEM((nk, kw), F32)],
        compiler_params=_cp(("parallel",), 16),
        name="swa_sample",
    )(q, cache_k, cache_v, k_new, v_new, bias, sink_col)


def _mem_attn_kernel(q0_ref, q1_ref, q2_ref, q3_ref, mk_ref, mv_ref, o_ref):
    scale = MEM_HEAD_DIM ** -0.5
    for h, q_ref in enumerate((q0_ref, q1_ref, q2_ref, q3_ref)):
        hs = slice(h * MEM_HEAD_DIM, (h + 1) * MEM_HEAD_DIM)
        s = lax.dot_general(q_ref[...].astype(BF16), mk_ref[:, hs].astype(BF16), (((1,), (1,)), ((), ())),
                            preferred_element_type=F32) * scale
        m = jnp.max(s, axis=-1, keepdims=True)
        p = jnp.exp(s - m)
        den = jnp.sum(p, axis=-1, keepdims=True)
        o = jnp.dot(p.astype(BF16), mv_ref[:, hs].astype(BF16), preferred_element_type=F32) / den
        o_ref[:, hs] = o.astype(o_ref.dtype)


def mem_attn_prompt(z, nb, seq, kv, tq):
    nt = seq // tq
    mlen = kv.shape[1]
    hd = MEM_HEAD_DIM
    q_specs = [pl.BlockSpec((tq, hd), functools.partial(lambda b, t, h: (b * nt + t, COL_XQ // hd + h), h=h))
               for h in range(MEM_HEADS)]
    return pl.pallas_call(
        _mem_attn_kernel,
        grid=(nb, nt),
        in_specs=q_specs + [pl.BlockSpec((None, mlen, BRANCH_DIM), lambda b, t: (b, 0, 0)),
                            pl.BlockSpec((None, mlen, BRANCH_DIM), lambda b, t: (b, 0, 1))],
        out_specs=pl.BlockSpec((tq, BRANCH_DIM), lambda b, t: (b * nt + t, 0)),
        out_shape=jax.ShapeDtypeStruct((nb * seq, BRANCH_DIM), BF16),
        compiler_params=_cp(("parallel", "arbitrary"), 32),
        name="mem_attn_prompt",
    )(z, z, z, z, kv, kv)


def mem_attn_sample(q, cache_k, cache_v, layer):
    nb, steps, _ = q.shape
    mlen = cache_k.shape[2]
    hd = MEM_HEAD_DIM
    q_specs = [pl.BlockSpec((None, steps, hd), functools.partial(lambda b, h: (b, 0, h), h=h))
               for h in range(MEM_HEADS)]
    cache_spec = pl.BlockSpec((None, None, mlen, BRANCH_DIM), lambda b: (layer, b, 0, 0))
    return pl.pallas_call(
        _mem_attn_kernel,
        grid=(nb,),
        in_specs=q_specs + [cache_spec, cache_spec],
        out_specs=pl.BlockSpec((None, steps, BRANCH_DIM), lambda b: (b, 0, 0)),
        out_shape=jax.ShapeDtypeStruct((nb, steps, BRANCH_DIM), BF16),
        compiler_params=_cp(("parallel",), 32),
        name="mem_attn_sample",
    )(q, q, q, q, cache_k, cache_v)


def _ssm_param_kernel(are_ref, aim_ref, ldt_ref, bre_ref, bim_ref, abr_ref, abi_ref, bbr_ref, bbi_ref):
    dt = jnp.exp(ldt_ref[...])
    ar, ai = are_ref[...], aim_ref[...]
    mag = jnp.exp(dt * ar)
    abr, abi = mag * jnp.cos(dt * ai), mag * jnp.sin(dt * ai)
    den = ar * ar + ai * ai
    nr, ni = abr - 1.0, abi
    fre, fim = (nr * ar + ni * ai) / den, (ni * ar - nr * ai) / den
    abr_ref[...] = abr
    abi_ref[...] = abi
    for c in range(SSM_GROUP):
        br, bi = bre_ref[c], bim_ref[c]
        bbr_ref[c] = fre * br - fim * bi
        bbi_ref[c] = fre * bi + fim * br


def ssm_params(a_re, a_im, log_dt, b_re_t, b_im_t):
    d, g, n = a_re.shape
    c = b_re_t.shape[1]
    gn = pl.BlockSpec((None, g, n), lambda l: (l, 0, 0))
    cgn = pl.BlockSpec((None, c, g, n), lambda l: (l, 0, 0, 0))
    return pl.pallas_call(
        _ssm_param_kernel,
        grid=(d,),
        in_specs=[gn, gn, pl.BlockSpec((None, g, 1), lambda l: (l, 0, 0)), cgn, cgn],
        out_specs=[gn, gn, cgn, cgn],
        out_shape=[jax.ShapeDtypeStruct((d, g, n), F32)] * 2 + [jax.ShapeDtypeStruct((d, c, g, n), F32)] * 2,
        compiler_params=_cp(("arbitrary",), 16),
        name="ssm_params",
    )(a_re, a_im, log_dt.reshape(d, g, 1), b_re_t, b_im_t)


def _cmul_add(ar, ai, hr, hi, br, bi):
    return ar * hr - ai * hi + br, ar * hi + ai * hr + bi


def _ssm_out(u, sre, sim, cre_ref, cim_ref, d_ref):
    y = (jnp.dot(sre[...].astype(BF16), cre_ref[...], preferred_element_type=F32)
         - jnp.dot(sim[...].astype(BF16), cim_ref[...], preferred_element_type=F32)
         + d_ref[...] * u)
    return jax.nn.gelu(y)


def _ssm_prompt_kernel(u_ref, bre_ref, bim_ref, cre_ref, cim_ref, ar_ref, ai_ref, d_ref,
                       y_ref, hr_ref, hi_ref, sre, sim, *, clen):
    np_ = SCAN_LANES
    w = sre.shape[1]
    u = u_ref[...]
    ub = u.astype(BF16)
    sre[...] = jnp.dot(ub, bre_ref[...], preferred_element_type=F32)
    sim[...] = jnp.dot(ub, bim_ref[...], preferred_element_type=F32)
    ar1, ai1 = ar_ref[...], ai_ref[...]
    ar = jnp.broadcast_to(ar1, (np_, w))
    ai = jnp.broadcast_to(ai1, (np_, w))

    def rows(t):
        return pl.ds(pl.multiple_of(t * np_, np_), np_)

    def local_step(t, carry):
        return _cmul_add(ar, ai, carry[0], carry[1], sre[rows(t), :], sim[rows(t), :])

    zero = jnp.zeros((np_, w), F32)
    fr, fi = lax.fori_loop(0, clen, local_step, (zero, zero))

    pr, pi = ar1, ai1
    for _ in range(int(math.log2(clen))):
        pr, pi = pr * pr - pi * pi, 2.0 * pr * pi
    row = lax.broadcasted_iota(jnp.int32, (np_, w), 0)
    cr = jnp.zeros((1, w), F32)
    ci = jnp.zeros((1, w), F32)
    hr0, hi0 = zero, zero
    for p in range(1, np_):
        cr, ci = _cmul_add(pr, pi, cr, ci, fr[p - 1:p], fi[p - 1:p])
        hr0 = jnp.where(row == p, cr, hr0)
        hi0 = jnp.where(row == p, ci, hi0)
    fin_r, fin_i = _cmul_add(pr, pi, cr, ci, fr[np_ - 1:np_], fi[np_ - 1:np_])
    hr_ref[...] = fin_r
    hi_ref[...] = fin_i

    def full_step(t, carry):
        nr, ni = _cmul_add(ar, ai, carry[0], carry[1], sre[rows(t), :], sim[rows(t), :])
        sre[rows(t), :] = nr
        sim[rows(t), :] = ni
        return nr, ni

    lax.fori_loop(0, clen, full_step, (hr0, hi0))
    y_ref[...] = _ssm_out(u, sre, sim, cre_ref, cim_ref, d_ref)


def _ssm_specs(kdim):
    kmap = lambda *idx: (idx[kdim], 0, 0)
    return [pl.BlockSpec((None, SSM_UW, SSM_CW), kmap), pl.BlockSpec((None, SSM_UW, SSM_CW), kmap),
            pl.BlockSpec((None, SSM_CW, SSM_UW), kmap), pl.BlockSpec((None, SSM_CW, SSM_UW), kmap),
            pl.BlockSpec((None, 1, SSM_CW), kmap), pl.BlockSpec((None, 1, SSM_CW), kmap),
            pl.BlockSpec((None, 1, SSM_UW), kmap)]


def ssm_prompt(u_perm, mats):
    nb, seq, _ = u_perm.shape
    clen = seq // SCAN_LANES
    nstate = SSM_GROUPS * SSM_STATE
    st = pl.BlockSpec((None, 1, SSM_CW), lambda b, k: (b, 0, k))
    return pl.pallas_call(
        functools.partial(_ssm_prompt_kernel, clen=clen),
        grid=(nb, SSM_CHUNKS),
        in_specs=[pl.BlockSpec((None, seq, SSM_UW), lambda b, k: (b, 0, k))] + _ssm_specs(1),
        out_specs=[pl.BlockSpec((None, seq, SSM_UW), lambda b, k: (b, 0, k)), st, st],
        out_shape=[jax.ShapeDtypeStruct((nb, seq, BRANCH_DIM), F32),
                   jax.ShapeDtypeStruct((nb, 1, nstate), F32), jax.ShapeDtypeStruct((nb, 1, nstate), F32)],
        scratch_shapes=[pltpu.VMEM((seq, SSM_CW), F32), pltpu.VMEM((seq, SSM_CW), F32)],
        compiler_params=_cp(("parallel", "parallel"), 48),
        name="ssm_prompt",
    )(u_perm, *mats)


def _ssm_sample_kernel(u_ref, bre_ref, bim_ref, cre_ref, cim_ref, ar_ref, ai_ref, d_ref, h0r_ref, h0i_ref,
                       y_ref, hr_ref, hi_ref, sre, sim, *, steps, nb):
    w = sre.shape[1]
    u = u_ref[...]
    ub = u.astype(BF16)
    bur = jnp.dot(ub, bre_ref[...], preferred_element_type=F32)
    bui = jnp.dot(ub, bim_ref[...], preferred_element_type=F32)
    ar = jnp.broadcast_to(ar_ref[...], (nb, w))
    ai = jnp.broadcast_to(ai_ref[...], (nb, w))
    hr, hi = h0r_ref[...], h0i_ref[...]
    for t in range(steps):
        rs = slice(t * nb, (t + 1) * nb)
        hr, hi = _cmul_add(ar, ai, hr, hi, bur[rs], bui[rs])
        sre[rs, :] = hr
        sim[rs, :] = hi
    hr_ref[...] = hr
    hi_ref[...] = hi
    y_ref[...] = _ssm_out(u, sre, sim, cre_ref, cim_ref, d_ref)


def ssm_sample(z, row0, steps, nb, mats, h0_re, h0_im, layer):
    rows = steps * nb
    nstate = SSM_GROUPS * SSM_STATE
    st_in = pl.BlockSpec((None, nb, SSM_CW), lambda k: (layer, 0, k))
    st_out = pl.BlockSpec((nb, SSM_CW), lambda k: (0, k))
    return pl.pallas_call(
        functools.partial(_ssm_sample_kernel, steps=steps, nb=nb),
        grid=(SSM_CHUNKS,),
        in_specs=[pl.BlockSpec((rows, SSM_UW), lambda k: (row0 // rows, COL_U // SSM_UW + k))]
                 + _ssm_specs(0) + [st_in, st_in],
        out_specs=[pl.BlockSpec((rows, SSM_UW), lambda k: (0, k)), st_out, st_out],
        out_shape=[jax.ShapeDtypeStruct((rows, BRANCH_DIM), F32),
                   jax.ShapeDtypeStruct((nb, nstate), F32), jax.ShapeDtypeStruct((nb, nstate), F32)],
        scratch_shapes=[pltpu.VMEM((rows, SSM_CW), F32), pltpu.VMEM((rows, SSM_CW), F32)],
        compiler_params=_cp(("parallel",), 32),
        name="ssm_sample",
    )(z, *mats, h0_re, h0_im)


def _block_diag(x, pattern):
    eye = jnp.eye(SSM_GROUPS // SSM_CHUNKS, dtype=x.dtype)
    return jnp.einsum(pattern, x, eye)


def kernel(x_prompt, x_sample, cache_conv, cache_win_k, cache_win_v, state_ssm_re, state_ssm_im, cache_mem_k, cache_mem_v, mem_prompt, t5_bias, norm_mix_pre, norm_mix_post, norm_ffn_pre, norm_ffn_post, norm_mem, w_in, conv_w, conv_b, conv_ln_g, conv_ln_b, attn_sinks, ssm_a_re, ssm_a_im, ssm_log_dt, ssm_b_re, ssm_b_im, ssm_c_re, ssm_c_im, ssm_d, ssm_w_glu, w_mem_kv, w_branch, w_out, w_ffn_in, w_ffn_out):
    bp, seq, d = x_prompt.shape
    bs, steps, _ = x_sample.shape
    rows_p = bp * seq
    rows_s = bs * steps
    rows = rows_p + rows_s
    mlen = mem_prompt.shape[1]
    hist = CONV_WIDTH - 1
    kw = SWA_KV_HEADS * SWA_HEAD_DIM
    gpc = SSM_GROUPS // SSM_CHUNKS
    clen = seq // SCAN_LANES

    tm = 1040
    tr = 320

    w_in_b, w_branch_b, w_out_b = w_in.astype(BF16), w_branch.astype(BF16), w_out.astype(BF16)
    w_ffn_in_b, w_ffn_out_b = w_ffn_in.astype(BF16), w_ffn_out.astype(BF16)
    w_glu_b, w_mem_kv_b = ssm_w_glu.astype(BF16), w_mem_kv.astype(BF16)

    qi = np.arange(WINDOW)
    ki = np.arange(2 * WINDOW) - WINDOW
    dist_p = qi[:, None] - ki[None, :]
    bias_p = band_bias(t5_bias, dist_p, (dist_p >= 0) & (dist_p < WINDOW))
    nk_s = WINDOW + steps + SWA_SAMPLE_PAD
    qrows = 8
    dist_s = (WINDOW + np.arange(qrows))[:, None] - np.arange(nk_s)[None, :]
    mask_s = (dist_s >= 0) & (dist_s < WINDOW) & (np.arange(nk_s)[None, :] < WINDOW + steps)
    bias_s = band_bias(t5_bias, dist_s, mask_s)[:, :steps]
    bias_s = bias_s.reshape(SWA_KV_HEADS, SWA_REP, steps, nk_s).transpose(0, 2, 1, 3)
    bias_s = bias_s.reshape(SWA_KV_HEADS, steps * SWA_REP, nk_s)

    abar_re, abar_im, bbar_re, bbar_im = ssm_params(
        ssm_a_re, ssm_a_im, ssm_log_dt, ssm_b_re.transpose(0, 3, 1, 2), ssm_b_im.transpose(0, 3, 1, 2))

    def in_mat(x):
        x = x.reshape(DEPTH, SSM_GROUP, SSM_CHUNKS, gpc, SSM_STATE)
        return _block_diag(x, 'lckgn,gh->lkgchn').reshape(DEPTH, SSM_CHUNKS, SSM_UW, SSM_CW).astype(BF16)

    def out_mat(x):
        x = x.reshape(DEPTH, SSM_CHUNKS, gpc, SSM_GROUP, SSM_STATE)
        return _block_diag(x, 'lkgcn,gh->lkgnhc').reshape(DEPTH, SSM_CHUNKS, SSM_CW, SSM_UW).astype(BF16)

    bmat_re, bmat_im = in_mat(bbar_re), in_mat(bbar_im)
    cmat_re, cmat_im = out_mat(ssm_c_re), out_mat(ssm_c_im)
    abar_re = abar_re.reshape(DEPTH, SSM_CHUNKS, 1, SSM_CW)
    abar_im = abar_im.reshape(DEPTH, SSM_CHUNKS, 1, SSM_CW)
    dvec = ssm_d.reshape(DEPTH, SSM_CHUNKS, 1, SSM_UW)

    cache_conv_t = cache_conv.transpose(0, 2, 1, 3)
    cache_k2 = cache_win_k.reshape(DEPTH, bs, -1, kw)
    cache_v2 = cache_win_v.reshape(DEPTH, bs, -1, kw)
    cache_mk = cache_mem_k.reshape(DEPTH, bs, mlen, BRANCH_DIM)
    cache_mv = cache_mem_v.reshape(DEPTH, bs, mlen, BRANCH_DIM)
    h0_re = state_ssm_re.reshape(DEPTH, bs, -1)
    h0_im = state_ssm_im.reshape(DEPTH, bs, -1)
    mem_rows = mem_prompt.reshape(bp * mlen, d)

    x = jnp.concatenate([x_prompt.reshape(rows_p, d), x_sample.transpose(1, 0, 2).reshape(rows_s, d)], axis=0)
    xn = rmsnorm_bf16(x, norm_mix_pre[0], tr)

    outs = [[] for _ in range(12)]
    for l in range(DEPTH):
        z = matmul(xn, w_in_b[l], F32, tm, 512)
        z_s = z[rows_p:]

        ya_p, conv_p = conv_prompt(z, bp, seq, conv_w[l], conv_b[l], conv_ln_g[l], conv_ln_b[l], 256)
        ya_s, conv_s_t = conv_sample(z, rows_p, steps, bs, cache_conv_t[l], conv_w[l], conv_b[l],
                                     conv_ln_g[l], conv_ln_b[l])

        yb_p = swa_prompt(z, bp, seq, attn_sinks[l], bias_p)
        q_s = z_s[:, COL_Q:COL_K].reshape(steps, bs, SWA_KV_HEADS, SWA_REP, SWA_HEAD_DIM)
        q_s = q_s.transpose(1, 2, 0, 3, 4).reshape(bs, SWA_KV_HEADS, steps * SWA_REP, SWA_HEAD_DIM)
        k_s = z_s[:, COL_K:COL_V].reshape(steps, bs, kw).transpose(1, 0, 2)
        v_s = z_s[:, COL_V:COL_U].reshape(steps, bs, kw).transpose(1, 0, 2)
        sink_col = jnp.tile(attn_sinks[l].reshape(SWA_KV_HEADS, 1, SWA_REP), (1, steps, 1))
        sink_col = sink_col.reshape(SWA_KV_HEADS, steps * SWA_REP, 1)
        ob_s, wk_s, wv_s = swa_sample(q_s, k_s, v_s, cache_k2, cache_v2, l, bias_s, sink_col)
        yb_s = ob_s.reshape(bs, SWA_KV_HEADS, steps, SWA_REP, SWA_HEAD_DIM).transpose(2, 0, 1, 3, 4)
        yb_s = yb_s.reshape(rows_s, BRANCH_DIM).astype(BF16)

        mats = (bmat_re[l], bmat_im[l], cmat_re[l], cmat_im[l], abar_re[l], abar_im[l], dvec[l])
        u_perm = z[:rows_p, COL_U:COL_XQ].reshape(bp, SCAN_LANES, clen, BRANCH_DIM).transpose(0, 2, 1, 3)
        yg_perm, hr_p, hi_p = ssm_prompt(u_perm.reshape(bp, seq, BRANCH_DIM), mats)
        yc_perm = glu(yg_perm.reshape(rows_p, BRANCH_DIM), w_glu_b[l], 1024)
        yc_p = yc_perm.reshape(bp, clen, SCAN_LANES, BRANCH_DIM).transpose(0, 2, 1, 3).reshape(rows_p, BRANCH_DIM)
        yg_s, hr_s, hi_s = ssm_sample(z, rows_p, steps, bs, mats, h0_re, h0_im, l)
        yc_s = glu(yg_s, w_glu_b[l], rows_s)

        kv = matmul(rmsnorm_bf16(mem_rows, norm_mem[l], 256), w_mem_kv_b[l], F32, bp * mlen, 512)
        kv = kv.reshape(bp, mlen, 2 * BRANCH_DIM)
        yx_p = mem_attn_prompt(z, bp, seq, kv, 512)
        xq_s = z_s[:, COL_XQ:COL_GATE].reshape(steps, bs, BRANCH_DIM).transpose(1, 0, 2)
        yx_s = mem_attn_sample(xq_s, cache_mk, cache_mv, l).transpose(1, 0, 2).reshape(rows_s, BRANCH_DIM)

        y_all = jnp.stack([jnp.concatenate([ya_p, ya_s]), jnp.concatenate([yb_p, yb_s]),
                           jnp.concatenate([yc_p, yc_s]), jnp.concatenate([yx_p, yx_s])])
        merged = merge_branches(y_all, w_branch_b[l], z, 640, 512)
        mix = matmul(merged, w_out_b[l], F32, tm, 512)
        x, hn = resid_norm(x, mix, norm_mix_post[l], norm_ffn_pre[l], tr)

        hid = ffn_in(hn, w_ffn_in_b[l], tm, 256)
        f = matmul(hid, w_ffn_out_b[l], F32, 640, 256)
        x, xn = resid_norm(x, f, norm_ffn_post[l], norm_mix_pre[(l + 1) % DEPTH], tr)

        z_p3 = z[:rows_p].reshape(bp, seq, N_IN)
        new = (conv_p, conv_s_t.transpose(1, 0, 2),
               z_p3[:, seq - WINDOW:, COL_K:COL_V].reshape(bp, WINDOW, SWA_KV_HEADS, SWA_HEAD_DIM),
               z_p3[:, seq - WINDOW:, COL_V:COL_U].reshape(bp, WINDOW, SWA_KV_HEADS, SWA_HEAD_DIM),
               wk_s.reshape(bs, WINDOW, SWA_KV_HEADS, SWA_HEAD_DIM),
               wv_s.reshape(bs, WINDOW, SWA_KV_HEADS, SWA_HEAD_DIM),
               hr_p.reshape(bp, SSM_GROUPS, SSM_STATE), hi_p.reshape(bp, SSM_GROUPS, SSM_STATE),
               hr_s.reshape(bs, SSM_GROUPS, SSM_STATE), hi_s.reshape(bs, SSM_GROUPS, SSM_STATE),
               kv[..., :BRANCH_DIM].reshape(bp, mlen, MEM_HEADS, MEM_HEAD_DIM),
               kv[..., BRANCH_DIM:].reshape(bp, mlen, MEM_HEADS, MEM_HEAD_DIM))
        for acc, val in zip(outs, new):
            acc.append(val)

    y_prompt = x[:rows_p].reshape(bp, seq, d)
    y_sample = x[rows_p:].reshape(steps, bs, d).transpose(1, 0, 2)
    return (y_prompt, y_sample) + tuple(jnp.stack(o) for o in outs)
```

```python
import functools
import math

import jax
import jax.numpy as jnp
import numpy as np
from jax import lax
from jax.experimental import pallas as pl
from jax.experimental.pallas import tpu as pltpu

F32 = jnp.float32
BF16 = jnp.bfloat16

D_MODEL = 4096
DEPTH = 4
BRANCH_DIM = 1024
CONV_WIDTH = 31
SWA_HEAD_DIM = 64
SWA_HEADS = 16
SWA_KV_HEADS = 4
SWA_REP = 4
WINDOW = 128
N_BUCKETS = 32
SSM_GROUP = 16
SSM_GROUPS = 64
SSM_STATE = 64
MEM_HEADS = 4
MEM_HEAD_DIM = 256
D_FF = 11008
COL_Q = 2048
COL_K = 3072
COL_V = 3328
COL_U = 3584
COL_XQ = 4608
COL_GATE = 5632
N_IN = COL_GATE + 4 * D_MODEL
NEG_INF = -1e30

SSM_CHUNKS = 4
SSM_CW = SSM_GROUPS // SSM_CHUNKS * SSM_STATE
SSM_UW = SSM_GROUPS // SSM_CHUNKS * SSM_GROUP
SCAN_LANES = 8


def _cp(sem, vmem_mb):
    return pltpu.CompilerParams(dimension_semantics=sem, vmem_limit_bytes=vmem_mb << 20)


def _rmsnorm_kernel(x_ref, g_ref, o_ref):
    x = x_ref[...]
    y = x * lax.rsqrt(jnp.mean(x * x, axis=-1, keepdims=True) + 1e-6)
    o_ref[...] = (y * g_ref[...]).astype(o_ref.dtype)


def rmsnorm_bf16(x, g, tr):
    m, d = x.shape
    return pl.pallas_call(
        _rmsnorm_kernel,
        grid=(m // tr,),
        in_specs=[pl.BlockSpec((tr, d), lambda i: (i, 0)), pl.BlockSpec((1, d), lambda i: (0, 0))],
        out_specs=pl.BlockSpec((tr, d), lambda i: (i, 0)),
        out_shape=jax.ShapeDtypeStruct((m, d), BF16),
        compiler_params=_cp(("parallel",), 40),
        name="rmsnorm",
    )(x, g.reshape(1, d))


def _resid_norm_kernel(x_ref, y_ref, gp_ref, gn_ref, xo_ref, hn_ref):
    y = y_ref[...]
    yn = y * lax.rsqrt(jnp.mean(y * y, axis=-1, keepdims=True) + 1e-6) * gp_ref[...]
    x = x_ref[...] + yn
    xo_ref[...] = x
    h = x * lax.rsqrt(jnp.mean(x * x, axis=-1, keepdims=True) + 1e-6)
    hn_ref[...] = (h * gn_ref[...]).astype(hn_ref.dtype)


def _resid_kernel(x_ref, y_ref, gp_ref, xo_ref):
    y = y_ref[...]
    xo_ref[...] = x_ref[...] + y * lax.rsqrt(jnp.mean(y * y, axis=-1, keepdims=True) + 1e-6) * gp_ref[...]


def resid_norm(x, y, g_post, g_next, tr):
    m, d = x.shape
    row = pl.BlockSpec((tr, d), lambda i: (i, 0))
    vec = pl.BlockSpec((1, d), lambda i: (0, 0))
    if g_next is None:
        return pl.pallas_call(
            _resid_kernel,
            grid=(m // tr,),
            in_specs=[row, row, vec],
            out_specs=row,
            out_shape=jax.ShapeDtypeStruct((m, d), F32),
            compiler_params=_cp(("parallel",), 48),
            name="resid",
        )(x, y, g_post.reshape(1, d)), None
    return pl.pallas_call(
        _resid_norm_kernel,
        grid=(m // tr,),
        in_specs=[row, row, vec, vec],
        out_specs=[row, row],
        out_shape=[jax.ShapeDtypeStruct((m, d), F32), jax.ShapeDtypeStruct((m, d), BF16)],
        compiler_params=_cp(("parallel",), 48),
        name="resid_norm",
    )(x, y, g_post.reshape(1, d), g_next.reshape(1, d))


def _mm_kernel(a_ref, w_ref, o_ref):
    o_ref[...] = jnp.dot(a_ref[...], w_ref[...].astype(BF16), preferred_element_type=F32).astype(o_ref.dtype)


def matmul(a, w, out_dtype, tm, tn, layer=None, n=None, vmem_mb=56):
    m, k = a.shape
    n = w.shape[-1] if n is None else n
    if layer is None:
        w_spec = pl.BlockSpec((k, tn), lambda i, j: (0, j))
    else:
        w_spec = pl.BlockSpec((None, k, tn), lambda i, j: (layer, 0, j))
    return pl.pallas_call(
        _mm_kernel,
        grid=(m // tm, n // tn),
        in_specs=[pl.BlockSpec((tm, k), lambda i, j: (i, 0)), w_spec],
        out_specs=pl.BlockSpec((tm, tn), lambda i, j: (i, j)),
        out_shape=jax.ShapeDtypeStruct((m, n), out_dtype),
        compiler_params=_cp(("parallel", "parallel"), vmem_mb),
        name="matmul",
    )(a, w)


def _ffn_in_kernel(a_ref, wg_ref, wu_ref, o_ref):
    a = a_ref[...]
    g = jnp.dot(a, wg_ref[...].astype(BF16), preferred_element_type=F32)
    u = jnp.dot(a, wu_ref[...].astype(BF16), preferred_element_type=F32)
    o_ref[...] = (jax.nn.silu(g) * u).astype(o_ref.dtype)


def ffn_in(a, w, layer, tm, tn):
    m, k = a.shape
    nt = D_FF // tn
    return pl.pallas_call(
        _ffn_in_kernel,
        grid=(m // tm, nt),
        in_specs=[pl.BlockSpec((tm, k), lambda i, j: (i, 0)),
                  pl.BlockSpec((None, k, tn), lambda i, j: (layer, 0, j)),
                  pl.BlockSpec((None, k, tn), lambda i, j: (layer, 0, j + nt))],
        out_specs=pl.BlockSpec((tm, tn), lambda i, j: (i, j)),
        out_shape=jax.ShapeDtypeStruct((m, D_FF), BF16),
        compiler_params=_cp(("parallel", "parallel"), 56),
        name="ffn_in",
    )(a, w, w)


def _merge_kernel(xn_ref, y_ref, wg_ref, wb_ref, o_ref, acc):
    b = pl.program_id(2)
    gate = jnp.dot(xn_ref[...], wg_ref[...].astype(BF16), preferred_element_type=F32)
    term = jax.nn.sigmoid(gate) * jnp.dot(y_ref[...], wb_ref[...].astype(BF16), preferred_element_type=F32)

    @pl.when(b == 0)
    def _():
        acc[...] = term

    @pl.when(b > 0)
    def _():
        acc[...] += term

    @pl.when(b == pl.num_programs(2) - 1)
    def _():
        o_ref[...] = acc[...].astype(o_ref.dtype)


def merge_branches(xn, y_all, w_in, w_branch, layer, tm, tn):
    nb, m, kb = y_all.shape
    k = xn.shape[1]
    gate0 = COL_GATE // tn
    per = D_MODEL // tn
    return pl.pallas_call(
        _merge_kernel,
        grid=(m // tm, per, nb),
        in_specs=[pl.BlockSpec((tm, k), lambda i, j, b: (i, 0), pipeline_mode=pl.Buffered(1)),
                  pl.BlockSpec((None, tm, kb), lambda i, j, b: (b, i, 0)),
                  pl.BlockSpec((None, k, tn), lambda i, j, b: (layer, 0, gate0 + b * per + j)),
                  pl.BlockSpec((None, None, kb, tn), lambda i, j, b: (layer, b, 0, j))],
        out_specs=pl.BlockSpec((tm, tn), lambda i, j, b: (i, j)),
        out_shape=jax.ShapeDtypeStruct((m, D_MODEL), BF16),
        scratch_shapes=[pltpu.VMEM((tm, tn), F32)],
        compiler_params=_cp(("parallel", "parallel", "arbitrary"), 56),
        name="merge",
    )(xn, y_all, w_in, w_branch)


def _skip_ref(body, pos):
    def wrapped(*refs):
        return body(*refs[:pos], *refs[pos + 1:])
    return wrapped


def _glu_kernel(y_ref, w_ref, o_ref):
    y = y_ref[...]
    s = jnp.dot(y.astype(BF16), w_ref[...].astype(BF16), preferred_element_type=F32)
    o_ref[...] = (y * jax.nn.sigmoid(s)).astype(o_ref.dtype)


def glu(y, w, layer, tr, ybuf=None, branch=0, row0=0):
    m, d = y.shape
    in_specs = [pl.BlockSpec((tr, d), lambda i: (i, 0)), pl.BlockSpec((None, d, d), lambda i: (layer, 0, 0))]
    if ybuf is None:
        return pl.pallas_call(
            _glu_kernel,
            grid=(m // tr,),
            in_specs=in_specs,
            out_specs=pl.BlockSpec((tr, d), lambda i: (i, 0)),
            out_shape=jax.ShapeDtypeStruct((m, d), BF16),
            compiler_params=_cp(("parallel",), 40),
            name="glu",
        )(y, w)
    return pl.pallas_call(
        _skip_ref(_glu_kernel, 2),
        grid=(m // tr,),
        in_specs=in_specs + [pl.BlockSpec(memory_space=pl.ANY)],
        out_specs=pl.BlockSpec((None, tr, d), lambda i: (branch, row0 // tr + i, 0)),
        out_shape=jax.ShapeDtypeStruct(ybuf.shape, ybuf.dtype),
        input_output_aliases={2: 0},
        compiler_params=_cp(("parallel",), 40),
        name="glu_into",
    )(y, w, ybuf)


def _ln_silu(y, g, b):
    mu = jnp.mean(y, axis=-1, keepdims=True)
    yc = y - mu
    yn = yc * lax.rsqrt(jnp.mean(yc * yc, axis=-1, keepdims=True) + 1e-5)
    return jax.nn.silu(yn * g + b)


CONV_HALO = 32
CONV_LANES = 128


def _conv_prompt_kernel(z_ref, w_ref, b_ref, g_ref, beta_ref, y_ref, cn_ref, xx, acc, *, tt):
    c = CONV_DIM_
    off = CONV_HALO - (CONV_WIDTH - 1)

    @pl.when(pl.program_id(1) == 0)
    def _():
        xx[0:CONV_HALO, :] = jnp.zeros((CONV_HALO, c), F32)

    xx[CONV_HALO:CONV_HALO + tt, :] = z_ref[:, :c] * jax.nn.sigmoid(z_ref[:, c:])
    for lc in range(c // CONV_LANES):
        ls = slice(lc * CONV_LANES, (lc + 1) * CONV_LANES)
        a = None
        for w in range(CONV_WIDTH):
            term = xx[off + w:off + w + tt, ls] * w_ref[w:w + 1, ls]
            a = term if a is None else a + term
        acc[:, ls] = a
    y_ref[...] = _ln_silu(acc[...] + b_ref[...], g_ref[...], beta_ref[...]).astype(y_ref.dtype)
    cn_ref[...] = xx[tt + off:tt + CONV_HALO, :]
    xx[0:CONV_HALO, :] = xx[tt:tt + CONV_HALO, :]


CONV_DIM_ = BRANCH_DIM


def conv_prompt(z, nb, seq, w, b, g, beta, tt, ybuf):
    c = CONV_DIM_
    nt = seq // tt
    vec = pl.BlockSpec((1, c), lambda bi, t: (0, 0))
    return pl.pallas_call(
        _skip_ref(functools.partial(_conv_prompt_kernel, tt=tt), 5),
        grid=(nb, nt),
        in_specs=[pl.BlockSpec((tt, 2 * c), lambda bi, t: (bi * nt + t, 0)),
                  pl.BlockSpec((CONV_WIDTH, c), lambda bi, t: (0, 0)), vec, vec, vec,
                  pl.BlockSpec(memory_space=pl.ANY)],
        out_specs=[pl.BlockSpec((None, tt, c), lambda bi, t: (0, bi * nt + t, 0)),
                   pl.BlockSpec((None, CONV_WIDTH - 1, c), lambda bi, t: (bi, 0, 0))],
        out_shape=[jax.ShapeDtypeStruct(ybuf.shape, ybuf.dtype),
                   jax.ShapeDtypeStruct((nb, CONV_WIDTH - 1, c), F32)],
        input_output_aliases={5: 0},
        scratch_shapes=[pltpu.VMEM((tt + CONV_HALO, c), F32), pltpu.VMEM((tt, c), F32)],
        compiler_params=_cp(("parallel", "arbitrary"), 40),
        name="conv_prompt",
    )(z, w, b.reshape(1, c), g.reshape(1, c), beta.reshape(1, c), ybuf)


def _conv_sample_kernel(z_ref, cache_ref, w_ref, b_ref, g_ref, beta_ref, y_ref, cn_ref, *, steps, nb):
    c = CONV_DIM_
    hist = CONV_WIDTH - 1
    a = z_ref[:, :c] * jax.nn.sigmoid(z_ref[:, c:])
    for t in range(steps):
        acc = None
        for w in range(CONV_WIDTH):
            idx = t + w
            src = cache_ref[idx] if idx < hist else a[(idx - hist) * nb:(idx - hist + 1) * nb]
            term = src * w_ref[w:w + 1, :]
            acc = term if acc is None else acc + term
        y_ref[t * nb:(t + 1) * nb, :] = _ln_silu(acc + b_ref[...], g_ref[...], beta_ref[...]).astype(y_ref.dtype)
    for r in range(hist - steps):
        cn_ref[r] = cache_ref[r + steps]
    for t in range(steps):
        cn_ref[hist - steps + t] = a[t * nb:(t + 1) * nb]


def conv_sample(z, row0, steps, nb, cache_t, w, b, g, beta, ybuf):
    c = CONV_DIM_
    rows = steps * nb
    hist = CONV_WIDTH - 1
    vec = pl.BlockSpec((1, c), lambda i: (0, 0))
    return pl.pallas_call(
        _skip_ref(functools.partial(_conv_sample_kernel, steps=steps, nb=nb), 6),
        grid=(1,),
        in_specs=[pl.BlockSpec((rows, 2 * c), lambda i: (row0 // rows, 0)),
                  pl.BlockSpec((hist, nb, c), lambda i: (0, 0, 0)),
                  pl.BlockSpec((CONV_WIDTH, c), lambda i: (0, 0)), vec, vec, vec,
                  pl.BlockSpec(memory_space=pl.ANY)],
        out_specs=[pl.BlockSpec((None, rows, c), lambda i: (0, row0 // rows, 0)),
                   pl.BlockSpec((hist, nb, c), lambda i: (0, 0, 0))],
        out_shape=[jax.ShapeDtypeStruct(ybuf.shape, ybuf.dtype), jax.ShapeDtypeStruct((hist, nb, c), F32)],
        input_output_aliases={6: 0},
        compiler_params=_cp(("arbitrary",), 40),
        name="conv_sample",
    )(z, cache_t, w, b.reshape(1, c), g.reshape(1, c), beta.reshape(1, c), ybuf)


def _t5_bucket_np(dist):
    n = np.maximum(dist, 0)
    max_exact = N_BUCKETS // 2
    nf = np.maximum(n, 1).astype(np.float32)
    large = max_exact + (np.log(nf / np.float32(max_exact)) / np.float32(math.log(WINDOW / max_exact))
                         * np.float32(N_BUCKETS - max_exact)).astype(np.int32)
    large = np.minimum(large, N_BUCKETS - 1)
    return np.where(n < max_exact, n, large)


def _bias_kernel(t5_ref, bucket_ref, o_ref):
    h = pl.program_id(0)
    bucket = bucket_ref[...]
    acc = jnp.full(bucket.shape, NEG_INF, F32)
    for b in range(N_BUCKETS):
        acc = jnp.where(bucket == b, t5_ref[b, h], acc)
    o_ref[...] = acc


def band_bias(t5_bias, dist, mask):
    bucket = np.where(mask, _t5_bucket_np(dist), -1).astype(np.int32)
    nq, nk = bucket.shape
    return pl.pallas_call(
        _bias_kernel,
        grid=(SWA_HEADS,),
        in_specs=[pl.BlockSpec(memory_space=pltpu.SMEM), pl.BlockSpec((nq, nk), lambda h: (0, 0))],
        out_specs=pl.BlockSpec((None, nq, nk), lambda h: (h, 0, 0)),
        out_shape=jax.ShapeDtypeStruct((SWA_HEADS, nq, nk), F32),
        compiler_params=_cp(("arbitrary",), 16),
        name="band_bias",
    )(t5_bias, jnp.asarray(bucket))


def _softmax_sink_pv(s, sink, v):
    m = jnp.maximum(jnp.max(s, axis=-1, keepdims=True), sink)
    p = jnp.exp(s - m)
    den = jnp.sum(p, axis=-1, keepdims=True) + jnp.exp(sink - m)
    return jnp.dot(p.astype(BF16), v, preferred_element_type=F32) / den


def _swa_prompt_kernel(sink_ref, q_ref, kp_ref, kc_ref, vp_ref, vc_ref, bias_ref, o_ref):
    blk = WINDOW
    scale = SWA_HEAD_DIM ** -0.5
    k = jnp.concatenate([kp_ref[...], kc_ref[...]], axis=0)
    v = jnp.concatenate([vp_ref[...], vc_ref[...]], axis=0)
    col = lax.broadcasted_iota(jnp.int32, (blk, 2 * blk), 1)
    no_prev = jnp.logical_and(pl.program_id(1) == 0, col < blk)
    q = q_ref[...]
    for hp in range(SWA_HEADS // 2):
        outs = []
        for h in (2 * hp, 2 * hp + 1):
            g = h // SWA_REP
            hs = slice(h * SWA_HEAD_DIM, (h + 1) * SWA_HEAD_DIM)
            gs = slice(g * SWA_HEAD_DIM, (g + 1) * SWA_HEAD_DIM)
            s = lax.dot_general(q[:, hs].astype(BF16), k[:, gs].astype(BF16), (((1,), (1,)), ((), ())),
                                preferred_element_type=F32) * scale
            s = jnp.where(no_prev, NEG_INF, s + bias_ref[h])
            outs.append(_softmax_sink_pv(s, sink_ref[h], v[:, gs].astype(BF16)))
        o_ref[:, 2 * hp * SWA_HEAD_DIM:(2 * hp + 2) * SWA_HEAD_DIM] = (
            jnp.concatenate(outs, axis=1).astype(o_ref.dtype))


def swa_prompt(z, nb, seq, sinks, bias, ybuf):
    blk = WINDOW
    nj = seq // blk
    qw = SWA_HEADS * SWA_HEAD_DIM
    kw = SWA_KV_HEADS * SWA_HEAD_DIM
    cur = lambda col: (lambda b, j: (b * nj + j, col))
    prev = lambda col: (lambda b, j: (b * nj + jnp.maximum(j - 1, 0), col))
    return pl.pallas_call(
        _skip_ref(_swa_prompt_kernel, 7),
        grid=(nb, nj),
        in_specs=[pl.BlockSpec(memory_space=pltpu.SMEM),
                  pl.BlockSpec((blk, qw), cur(COL_Q // qw)),
                  pl.BlockSpec((blk, kw), prev(COL_K // kw)), pl.BlockSpec((blk, kw), cur(COL_K // kw)),
                  pl.BlockSpec((blk, kw), prev(COL_V // kw)), pl.BlockSpec((blk, kw), cur(COL_V // kw)),
                  pl.BlockSpec((SWA_HEADS, blk, 2 * blk), lambda b, j: (0, 0, 0)),
                  pl.BlockSpec(memory_space=pl.ANY)],
        out_specs=pl.BlockSpec((None, blk, qw), lambda b, j: (1, b * nj + j, 0)),
        out_shape=jax.ShapeDtypeStruct(ybuf.shape, ybuf.dtype),
        input_output_aliases={7: 0},
        compiler_params=_cp(("parallel", "arbitrary"), 32),
        name="swa_prompt",
    )(sinks, z, z, z, z, z, bias, ybuf)


SWA_SAMPLE_PAD = 4


def _swa_sample_kernel(q_ref, kc_ref, vc_ref, kn_ref, vn_ref, bias_ref, sink_ref, o_ref, wk_ref, wv_ref,
                       kk, vv, *, steps):
    win = WINDOW
    scale = SWA_HEAD_DIM ** -0.5
    kw = SWA_KV_HEADS * SWA_HEAD_DIM
    for buf, cache, new in ((kk, kc_ref, kn_ref), (vv, vc_ref, vn_ref)):
        buf[0:win, :] = cache[...]
        buf[win:win + steps, :] = new[...]
        buf[win + steps:win + steps + SWA_SAMPLE_PAD, :] = jnp.zeros((SWA_SAMPLE_PAD, kw), F32)
    wk_ref[...] = kk[steps:steps + win, :]
    wv_ref[...] = vv[steps:steps + win, :]
    for g in range(SWA_KV_HEADS):
        gs = slice(g * SWA_HEAD_DIM, (g + 1) * SWA_HEAD_DIM)
        s = lax.dot_general(q_ref[g].astype(BF16), kk[:, gs].astype(BF16), (((1,), (1,)), ((), ())),
                            preferred_element_type=F32) * scale + bias_ref[g]
        o_ref[g] = _softmax_sink_pv(s, sink_ref[g], vv[:, gs].astype(BF16))


def swa_sample(q, k_new, v_new, cache_k, cache_v, layer, bias, sink_col):
    nb, _, rows, _ = q.shape
    steps = k_new.shape[1]
    win = WINDOW
    kw = SWA_KV_HEADS * SWA_HEAD_DIM
    nk = win + steps + SWA_SAMPLE_PAD
    cache_spec = pl.BlockSpec((None, None, win, kw), lambda b: (layer, b, 0, 0))
    new_spec = pl.BlockSpec((None, steps, kw), lambda b: (b, 0, 0))
    win_spec = pl.BlockSpec((None, win, kw), lambda b: (b, 0, 0))
    qo_spec = pl.BlockSpec((None, SWA_KV_HEADS, rows, SWA_HEAD_DIM), lambda b: (b, 0, 0, 0))
    return pl.pallas_call(
        functools.partial(_swa_sample_kernel, steps=steps),
        grid=(nb,),
        in_specs=[qo_spec, cache_spec, cache_spec, new_spec, new_spec,
                  pl.BlockSpec((SWA_KV_HEADS, rows, nk), lambda b: (0, 0, 0)),
                  pl.BlockSpec((SWA_KV_HEADS, rows, 1), lambda b: (0, 0, 0))],
        out_specs=[qo_spec, win_spec, win_spec],
        out_shape=[jax.ShapeDtypeStruct(q.shape, F32),
                   jax.ShapeDtypeStruct((nb, win, kw), F32), jax.ShapeDtypeStruct((nb, win, kw), F32)],
        scratch_shapes=[pltpu.VMEM((nk, kw), F32), pltpu.VMEM((nk, kw), F32)],
        compiler_params=_cp(("parallel",), 16),
        name="swa_sample",
    )(q, cache_k, cache_v, k_new, v_new, bias, sink_col)


def _mem_attn_kernel(q0_ref, q1_ref, q2_ref, q3_ref, mk_ref, mv_ref, o_ref):
    scale = MEM_HEAD_DIM ** -0.5
    for h, q_ref in enumerate((q0_ref, q1_ref, q2_ref, q3_ref)):
        hs = slice(h * MEM_HEAD_DIM, (h + 1) * MEM_HEAD_DIM)
        s = lax.dot_general(q_ref[...].astype(BF16), mk_ref[:, hs].astype(BF16), (((1,), (1,)), ((), ())),
                            preferred_element_type=F32) * scale
        m = jnp.max(s, axis=-1, keepdims=True)
        p = jnp.exp(s - m)
        den = jnp.sum(p, axis=-1, keepdims=True)
        o = jnp.dot(p.astype(BF16), mv_ref[:, hs].astype(BF16), preferred_element_type=F32) / den
        o_ref[:, hs] = o.astype(o_ref.dtype)


def mem_attn_prompt(z, nb, seq, kv, tq, ybuf):
    nt = seq // tq
    mlen = kv.shape[1]
    hd = MEM_HEAD_DIM
    q_specs = [pl.BlockSpec((tq, hd), functools.partial(lambda b, t, h: (b * nt + t, COL_XQ // hd + h), h=h))
               for h in range(MEM_HEADS)]
    return pl.pallas_call(
        _skip_ref(_mem_attn_kernel, 6),
        grid=(nb, nt),
        in_specs=q_specs + [pl.BlockSpec((None, mlen, BRANCH_DIM), lambda b, t: (b, 0, 0)),
                            pl.BlockSpec((None, mlen, BRANCH_DIM), lambda b, t: (b, 0, 1)),
                            pl.BlockSpec(memory_space=pl.ANY)],
        out_specs=pl.BlockSpec((None, tq, BRANCH_DIM), lambda b, t: (3, b * nt + t, 0)),
        out_shape=jax.ShapeDtypeStruct(ybuf.shape, ybuf.dtype),
        input_output_aliases={6: 0},
        compiler_params=_cp(("parallel", "arbitrary"), 32),
        name="mem_attn_prompt",
    )(z, z, z, z, kv, kv, ybuf)


def mem_attn_sample(q, cache_k, cache_v, layer):
    nb, steps, _ = q.shape
    mlen = cache_k.shape[2]
    hd = MEM_HEAD_DIM
    q_specs = [pl.BlockSpec((None, steps, hd), functools.partial(lambda b, h: (b, 0, h), h=h))
               for h in range(MEM_HEADS)]
    cache_spec = pl.BlockSpec((None, None, mlen, BRANCH_DIM), lambda b: (layer, b, 0, 0))
    return pl.pallas_call(
        _mem_attn_kernel,
        grid=(nb,),
        in_specs=q_specs + [cache_spec, cache_spec],
        out_specs=pl.BlockSpec((None, steps, BRANCH_DIM), lambda b: (b, 0, 0)),
        out_shape=jax.ShapeDtypeStruct((nb, steps, BRANCH_DIM), BF16),
        compiler_params=_cp(("parallel",), 32),
        name="mem_attn_sample",
    )(q, q, q, q, cache_k, cache_v)


def _ssm_param_kernel(are_ref, aim_ref, ldt_ref, bre_ref, bim_ref, abr_ref, abi_ref, bbr_ref, bbi_ref):
    dt = jnp.exp(ldt_ref[...])
    ar, ai = are_ref[...], aim_ref[...]
    mag = jnp.exp(dt * ar)
    abr, abi = mag * jnp.cos(dt * ai), mag * jnp.sin(dt * ai)
    den = ar * ar + ai * ai
    nr, ni = abr - 1.0, abi
    fre, fim = (nr * ar + ni * ai) / den, (ni * ar - nr * ai) / den
    abr_ref[...] = abr
    abi_ref[...] = abi
    for c in range(SSM_GROUP):
        br, bi = bre_ref[c], bim_ref[c]
        bbr_ref[c] = fre * br - fim * bi
        bbi_ref[c] = fre * bi + fim * br


def ssm_params(a_re, a_im, log_dt, b_re_t, b_im_t):
    d, g, n = a_re.shape
    c = b_re_t.shape[1]
    gn = pl.BlockSpec((None, g, n), lambda l: (l, 0, 0))
    cgn = pl.BlockSpec((None, c, g, n), lambda l: (l, 0, 0, 0))
    return pl.pallas_call(
        _ssm_param_kernel,
        grid=(d,),
        in_specs=[gn, gn, pl.BlockSpec((None, g, 1), lambda l: (l, 0, 0)), cgn, cgn],
        out_specs=[gn, gn, cgn, cgn],
        out_shape=[jax.ShapeDtypeStruct((d, g, n), F32)] * 2 + [jax.ShapeDtypeStruct((d, c, g, n), F32)] * 2,
        compiler_params=_cp(("arbitrary",), 16),
        name="ssm_params",
    )(a_re, a_im, log_dt.reshape(d, g, 1), b_re_t, b_im_t)


def _cmul_add(ar, ai, hr, hi, br, bi):
    return ar * hr - ai * hi + br, ar * hi + ai * hr + bi


def _ssm_out(u, sre, sim, cre_ref, cim_ref, d_ref):
    y = (jnp.dot(sre[...].astype(BF16), cre_ref[...], preferred_element_type=F32)
         - jnp.dot(sim[...].astype(BF16), cim_ref[...], preferred_element_type=F32)
         + d_ref[...] * u)
    return jax.nn.gelu(y)


def _ssm_prompt_kernel(u_ref, bre_ref, bim_ref, cre_ref, cim_ref, ar_ref, ai_ref, d_ref,
                       y_ref, hr_ref, hi_ref, sre, sim, *, clen):
    np_ = SCAN_LANES
    w = sre.shape[1]
    u = u_ref[...]
    ub = u.astype(BF16)
    sre[...] = jnp.dot(ub, bre_ref[...], preferred_element_type=F32)
    sim[...] = jnp.dot(ub, bim_ref[...], preferred_element_type=F32)
    ar1, ai1 = ar_ref[...], ai_ref[...]
    ar = jnp.broadcast_to(ar1, (np_, w))
    ai = jnp.broadcast_to(ai1, (np_, w))

    def rows(t):
        return pl.ds(pl.multiple_of(t * np_, np_), np_)

    def local_step(t, carry):
        return _cmul_add(ar, ai, carry[0], carry[1], sre[rows(t), :], sim[rows(t), :])

    zero = jnp.zeros((np_, w), F32)
    fr, fi = lax.fori_loop(0, clen, local_step, (zero, zero))

    pr, pi = ar1, ai1
    for _ in range(int(math.log2(clen))):
        pr, pi = pr * pr - pi * pi, 2.0 * pr * pi
    row = lax.broadcasted_iota(jnp.int32, (np_, w), 0)
    cr = jnp.zeros((1, w), F32)
    ci = jnp.zeros((1, w), F32)
    hr0, hi0 = zero, zero
    for p in range(1, np_):
        cr, ci = _cmul_add(pr, pi, cr, ci, fr[p - 1:p], fi[p - 1:p])
        hr0 = jnp.where(row == p, cr, hr0)
        hi0 = jnp.where(row == p, ci, hi0)
    fin_r, fin_i = _cmul_add(pr, pi, cr, ci, fr[np_ - 1:np_], fi[np_ - 1:np_])
    hr_ref[...] = fin_r
    hi_ref[...] = fin_i

    def full_step(t, carry):
        nr, ni = _cmul_add(ar, ai, carry[0], carry[1], sre[rows(t), :], sim[rows(t), :])
        sre[rows(t), :] = nr
        sim[rows(t), :] = ni
        return nr, ni

    lax.fori_loop(0, clen, full_step, (hr0, hi0))
    y_ref[...] = _ssm_out(u, sre, sim, cre_ref, cim_ref, d_ref)


def _ssm_specs(kdim):
    kmap = lambda *idx: (idx[kdim], 0, 0)
    return [pl.BlockSpec((None, SSM_UW, SSM_CW), kmap), pl.BlockSpec((None, SSM_UW, SSM_CW), kmap),
            pl.BlockSpec((None, SSM_CW, SSM_UW), kmap), pl.BlockSpec((None, SSM_CW, SSM_UW), kmap),
            pl.BlockSpec((None, 1, SSM_CW), kmap), pl.BlockSpec((None, 1, SSM_CW), kmap),
            pl.BlockSpec((None, 1, SSM_UW), kmap)]


def ssm_prompt(u_perm, mats):
    nb, seq, _ = u_perm.shape
    clen = seq // SCAN_LANES
    nstate = SSM_GROUPS * SSM_STATE
    st = pl.BlockSpec((None, 1, SSM_CW), lambda b, k: (b, 0, k))
    return pl.pallas_call(
        functools.partial(_ssm_prompt_kernel, clen=clen),
        grid=(nb, SSM_CHUNKS),
        in_specs=[pl.BlockSpec((None, seq, SSM_UW), lambda b, k: (b, 0, k))] + _ssm_specs(1),
        out_specs=[pl.BlockSpec((None, seq, SSM_UW), lambda b, k: (b, 0, k)), st, st],
        out_shape=[jax.ShapeDtypeStruct((nb, seq, BRANCH_DIM), F32),
                   jax.ShapeDtypeStruct((nb, 1, nstate), F32), jax.ShapeDtypeStruct((nb, 1, nstate), F32)],
        scratch_shapes=[pltpu.VMEM((seq, SSM_CW), F32), pltpu.VMEM((seq, SSM_CW), F32)],
        compiler_params=_cp(("parallel", "parallel"), 48),
        name="ssm_prompt",
    )(u_perm, *mats)


def _ssm_sample_kernel(u_ref, bre_ref, bim_ref, cre_ref, cim_ref, ar_ref, ai_ref, d_ref, h0r_ref, h0i_ref,
                       y_ref, hr_ref, hi_ref, sre, sim, *, steps, nb):
    w = sre.shape[1]
    u = u_ref[...]
    ub = u.astype(BF16)
    bur = jnp.dot(ub, bre_ref[...], preferred_element_type=F32)
    bui = jnp.dot(ub, bim_ref[...], preferred_element_type=F32)
    ar = jnp.broadcast_to(ar_ref[...], (nb, w))
    ai = jnp.broadcast_to(ai_ref[...], (nb, w))
    hr, hi = h0r_ref[...], h0i_ref[...]
    for t in range(steps):
        rs = slice(t * nb, (t + 1) * nb)
        hr, hi = _cmul_add(ar, ai, hr, hi, bur[rs], bui[rs])
        sre[rs, :] = hr
        sim[rs, :] = hi
    hr_ref[...] = hr
    hi_ref[...] = hi
    y_ref[...] = _ssm_out(u, sre, sim, cre_ref, cim_ref, d_ref)


def ssm_sample(z, row0, steps, nb, mats, h0_re, h0_im, layer):
    rows = steps * nb
    nstate = SSM_GROUPS * SSM_STATE
    st_in = pl.BlockSpec((None, nb, SSM_CW), lambda k: (layer, 0, k))
    st_out = pl.BlockSpec((nb, SSM_CW), lambda k: (0, k))
    return pl.pallas_call(
        functools.partial(_ssm_sample_kernel, steps=steps, nb=nb),
        grid=(SSM_CHUNKS,),
        in_specs=[pl.BlockSpec((rows, SSM_UW), lambda k: (row0 // rows, COL_U // SSM_UW + k))]
                 + _ssm_specs(0) + [st_in, st_in],
        out_specs=[pl.BlockSpec((rows, SSM_UW), lambda k: (0, k)), st_out, st_out],
        out_shape=[jax.ShapeDtypeStruct((rows, BRANCH_DIM), F32),
                   jax.ShapeDtypeStruct((nb, nstate), F32), jax.ShapeDtypeStruct((nb, nstate), F32)],
        scratch_shapes=[pltpu.VMEM((rows, SSM_CW), F32), pltpu.VMEM((rows, SSM_CW), F32)],
        compiler_params=_cp(("parallel",), 32),
        name="ssm_sample",
    )(z, *mats, h0_re, h0_im)


def _block_diag(x, pattern):
    eye = jnp.eye(SSM_GROUPS // SSM_CHUNKS, dtype=x.dtype)
    return jnp.einsum(pattern, x, eye)


def kernel(x_prompt, x_sample, cache_conv, cache_win_k, cache_win_v, state_ssm_re, state_ssm_im, cache_mem_k, cache_mem_v, mem_prompt, t5_bias, norm_mix_pre, norm_mix_post, norm_ffn_pre, norm_ffn_post, norm_mem, w_in, conv_w, conv_b, conv_ln_g, conv_ln_b, attn_sinks, ssm_a_re, ssm_a_im, ssm_log_dt, ssm_b_re, ssm_b_im, ssm_c_re, ssm_c_im, ssm_d, ssm_w_glu, w_mem_kv, w_branch, w_out, w_ffn_in, w_ffn_out):
    bp, seq, d = x_prompt.shape
    bs, steps, _ = x_sample.shape
    rows_p = bp * seq
    rows_s = bs * steps
    rows = rows_p + rows_s
    mlen = mem_prompt.shape[1]
    hist = CONV_WIDTH - 1
    kw = SWA_KV_HEADS * SWA_HEAD_DIM
    gpc = SSM_GROUPS // SSM_CHUNKS
    clen = seq // SCAN_LANES

    tm = 1040
    tr = 320

    w_ffn_out_b = w_ffn_out.astype(BF16)

    qi = np.arange(WINDOW)
    ki = np.arange(2 * WINDOW) - WINDOW
    dist_p = qi[:, None] - ki[None, :]
    bias_p = band_bias(t5_bias, dist_p, (dist_p >= 0) & (dist_p < WINDOW))
    nk_s = WINDOW + steps + SWA_SAMPLE_PAD
    qrows = 8
    dist_s = (WINDOW + np.arange(qrows))[:, None] - np.arange(nk_s)[None, :]
    mask_s = (dist_s >= 0) & (dist_s < WINDOW) & (np.arange(nk_s)[None, :] < WINDOW + steps)
    bias_s = band_bias(t5_bias, dist_s, mask_s)[:, :steps]
    bias_s = bias_s.reshape(SWA_KV_HEADS, SWA_REP, steps, nk_s).transpose(0, 2, 1, 3)
    bias_s = bias_s.reshape(SWA_KV_HEADS, steps * SWA_REP, nk_s)

    abar_re, abar_im, bbar_re, bbar_im = ssm_params(
        ssm_a_re, ssm_a_im, ssm_log_dt, ssm_b_re.transpose(0, 3, 1, 2), ssm_b_im.transpose(0, 3, 1, 2))

    def in_mat(x):
        x = x.reshape(DEPTH, SSM_GROUP, SSM_CHUNKS, gpc, SSM_STATE)
        return _block_diag(x, 'lckgn,gh->lkgchn').reshape(DEPTH, SSM_CHUNKS, SSM_UW, SSM_CW).astype(BF16)

    def out_mat(x):
        x = x.reshape(DEPTH, SSM_CHUNKS, gpc, SSM_GROUP, SSM_STATE)
        return _block_diag(x, 'lkgcn,gh->lkgnhc').reshape(DEPTH, SSM_CHUNKS, SSM_CW, SSM_UW).astype(BF16)

    bmat_re, bmat_im = in_mat(bbar_re), in_mat(bbar_im)
    cmat_re, cmat_im = out_mat(ssm_c_re), out_mat(ssm_c_im)
    abar_re = abar_re.reshape(DEPTH, SSM_CHUNKS, 1, SSM_CW)
    abar_im = abar_im.reshape(DEPTH, SSM_CHUNKS, 1, SSM_CW)
    dvec = ssm_d.reshape(DEPTH, SSM_CHUNKS, 1, SSM_UW)

    cache_conv_t = cache_conv.transpose(0, 2, 1, 3)
    cache_k2 = cache_win_k.reshape(DEPTH, bs, -1, kw)
    cache_v2 = cache_win_v.reshape(DEPTH, bs, -1, kw)
    cache_mk = cache_mem_k.reshape(DEPTH, bs, mlen, BRANCH_DIM)
    cache_mv = cache_mem_v.reshape(DEPTH, bs, mlen, BRANCH_DIM)
    h0_re = state_ssm_re.reshape(DEPTH, bs, -1)
    h0_im = state_ssm_im.reshape(DEPTH, bs, -1)
    mem_rows = mem_prompt.reshape(bp * mlen, d)

    x = jnp.concatenate([x_prompt.reshape(rows_p, d), x_sample.transpose(1, 0, 2).reshape(rows_s, d)], axis=0)
    xn = rmsnorm_bf16(x, norm_mix_pre[0], tr)

    outs = [[] for _ in range(12)]
    for l in range(DEPTH):
        z = matmul(xn, w_in, F32, tm, 512, layer=l, n=COL_GATE)
        z_s = lax.slice(z, (rows_p, 0), (rows, COL_GATE))
        ybuf = jnp.zeros((4, rows, BRANCH_DIM), BF16)

        ybuf, conv_p = conv_prompt(z, bp, seq, conv_w[l], conv_b[l], conv_ln_g[l], conv_ln_b[l], 256, ybuf)
        ybuf, conv_s_t = conv_sample(z, rows_p, steps, bs, cache_conv_t[l], conv_w[l], conv_b[l],
                                     conv_ln_g[l], conv_ln_b[l], ybuf)

        ybuf = swa_prompt(z, bp, seq, attn_sinks[l], bias_p, ybuf)
        q_s = z_s[:, COL_Q:COL_K].reshape(steps, bs, SWA_KV_HEADS, SWA_REP, SWA_HEAD_DIM)
        q_s = q_s.transpose(1, 2, 0, 3, 4).reshape(bs, SWA_KV_HEADS, steps * SWA_REP, SWA_HEAD_DIM)
        k_s = z_s[:, COL_K:COL_V].reshape(steps, bs, kw).transpose(1, 0, 2)
        v_s = z_s[:, COL_V:COL_U].reshape(steps, bs, kw).transpose(1, 0, 2)
        sink_col = jnp.tile(attn_sinks[l].reshape(SWA_KV_HEADS, 1, SWA_REP), (1, steps, 1))
        sink_col = sink_col.reshape(SWA_KV_HEADS, steps * SWA_REP, 1)
        ob_s, wk_s, wv_s = swa_sample(q_s, k_s, v_s, cache_k2, cache_v2, l, bias_s, sink_col)
        yb_s = ob_s.reshape(bs, SWA_KV_HEADS, steps, SWA_REP, SWA_HEAD_DIM).transpose(2, 0, 1, 3, 4)
        yb_s = yb_s.reshape(rows_s, BRANCH_DIM).astype(BF16)

        mats = (bmat_re[l], bmat_im[l], cmat_re[l], cmat_im[l], abar_re[l], abar_im[l], dvec[l])
        u_perm = lax.slice(z, (0, COL_U), (rows_p, COL_XQ))
        u_perm = u_perm.reshape(bp, SCAN_LANES, clen, BRANCH_DIM).transpose(0, 2, 1, 3)
        yg_perm, hr_p, hi_p = ssm_prompt(u_perm.reshape(bp, seq, BRANCH_DIM), mats)
        yc_perm = glu(yg_perm.reshape(rows_p, BRANCH_DIM), ssm_w_glu, l, 1024)
        yc_p = yc_perm.reshape(bp, clen, SCAN_LANES, BRANCH_DIM).transpose(0, 2, 1, 3).reshape(rows_p, BRANCH_DIM)
        yg_s, hr_s, hi_s = ssm_sample(z, rows_p, steps, bs, mats, h0_re, h0_im, l)
        ybuf = glu(yg_s, ssm_w_glu, l, rows_s, ybuf, 2, rows_p)

        kv = matmul(rmsnorm_bf16(mem_rows, norm_mem[l], 256), w_mem_kv, F32, bp * mlen, 512, layer=l)
        kv = kv.reshape(bp, mlen, 2 * BRANCH_DIM)
        ybuf = mem_attn_prompt(z, bp, seq, kv, 512, ybuf)
        xq_s = z_s[:, COL_XQ:COL_GATE].reshape(steps, bs, BRANCH_DIM).transpose(1, 0, 2)
        yx_s = mem_attn_sample(xq_s, cache_mk, cache_mv, l).transpose(1, 0, 2).reshape(rows_s, BRANCH_DIM)

        ybuf = lax.dynamic_update_slice(ybuf, yb_s[None], (1, rows_p, 0))
        ybuf = lax.dynamic_update_slice(ybuf, yc_p[None], (2, 0, 0))
        ybuf = lax.dynamic_update_slice(ybuf, yx_s[None], (3, rows_p, 0))
        merged = merge_branches(xn, ybuf, w_in, w_branch, l, tm, 512)
        mix = matmul(merged, w_out, F32, tm, 512, layer=l)
        x, hn = resid_norm(x, mix, norm_mix_post[l], norm_ffn_pre[l], tr)

        hid = ffn_in(hn, w_ffn_in, l, tm, 256)
        f = matmul(hid, w_ffn_out_b, F32, 640, 256, layer=l)
        x, xn = resid_norm(x, f, norm_ffn_post[l], norm_mix_pre[l + 1] if l + 1 < DEPTH else None, tr)

        kv_win = jnp.stack([lax.slice(z, ((b + 1) * seq - WINDOW, COL_K), ((b + 1) * seq, COL_U))
                            for b in range(bp)])
        new = (conv_p, conv_s_t.transpose(1, 0, 2),
               kv_win[..., :kw].reshape(bp, WINDOW, SWA_KV_HEADS, SWA_HEAD_DIM),
               kv_win[..., kw:].reshape(bp, WINDOW, SWA_KV_HEADS, SWA_HEAD_DIM),
               wk_s.reshape(bs, WINDOW, SWA_KV_HEADS, SWA_HEAD_DIM),
               wv_s.reshape(bs, WINDOW, SWA_KV_HEADS, SWA_HEAD_DIM),
               hr_p.reshape(bp, SSM_GROUPS, SSM_STATE), hi_p.reshape(bp, SSM_GROUPS, SSM_STATE),
               hr_s.reshape(bs, SSM_GROUPS, SSM_STATE), hi_s.reshape(bs, SSM_GROUPS, SSM_STATE),
               kv[..., :BRANCH_DIM].reshape(bp, mlen, MEM_HEADS, MEM_HEAD_DIM),
               kv[..., BRANCH_DIM:].reshape(bp, mlen, MEM_HEADS, MEM_HEAD_DIM))
        for acc, val in zip(outs, new):
            acc.append(val)

    y_prompt = x[:rows_p].reshape(bp, seq, d)
    y_sample = x[rows_p:].reshape(steps, bs, d).transpose(1, 0, 2)
    return (y_prompt, y_sample) + tuple(jnp.stack(o) for o in outs)
```

```python
import functools
import math

import jax
import jax.numpy as jnp
import numpy as np
from jax import lax
from jax.experimental import pallas as pl
from jax.experimental.pallas import tpu as pltpu

F32 = jnp.float32
BF16 = jnp.bfloat16

D_MODEL = 4096
DEPTH = 4
BRANCH_DIM = 1024
CONV_WIDTH = 31
SWA_HEAD_DIM = 64
SWA_HEADS = 16
SWA_KV_HEADS = 4
SWA_REP = 4
WINDOW = 128
N_BUCKETS = 32
SSM_GROUP = 16
SSM_GROUPS = 64
SSM_STATE = 64
MEM_HEADS = 4
MEM_HEAD_DIM = 256
D_FF = 11008
COL_Q = 2048
COL_K = 3072
COL_V = 3328
COL_U = 3584
COL_XQ = 4608
COL_GATE = 5632
N_IN = COL_GATE + 4 * D_MODEL
NEG_INF = -1e30

SSM_CHUNKS = 4
SSM_CW = SSM_GROUPS // SSM_CHUNKS * SSM_STATE
SSM_UW = SSM_GROUPS // SSM_CHUNKS * SSM_GROUP
SCAN_LANES = 8
SCAN_UNROLL = 8


def _cp(sem, vmem_mb):
    return pltpu.CompilerParams(dimension_semantics=sem, vmem_limit_bytes=vmem_mb << 20)


def _rmsnorm_kernel(x_ref, g_ref, o_ref):
    x = x_ref[...]
    y = x * lax.rsqrt(jnp.mean(x * x, axis=-1, keepdims=True) + 1e-6)
    o_ref[...] = (y * g_ref[...]).astype(o_ref.dtype)


def rmsnorm_bf16(x, g, tr):
    m, d = x.shape
    return pl.pallas_call(
        _rmsnorm_kernel,
        grid=(m // tr,),
        in_specs=[pl.BlockSpec((tr, d), lambda i: (i, 0)), pl.BlockSpec((1, d), lambda i: (0, 0))],
        out_specs=pl.BlockSpec((tr, d), lambda i: (i, 0)),
        out_shape=jax.ShapeDtypeStruct((m, d), BF16),
        compiler_params=_cp(("parallel",), 40),
        name="rmsnorm",
    )(x, g.reshape(1, d))


def _resid_norm_kernel(x_ref, y_ref, gp_ref, gn_ref, xo_ref, hn_ref):
    y = y_ref[...].astype(F32)
    yn = y * lax.rsqrt(jnp.mean(y * y, axis=-1, keepdims=True) + 1e-6) * gp_ref[...]
    x = x_ref[...] + yn
    xo_ref[...] = x
    h = x * lax.rsqrt(jnp.mean(x * x, axis=-1, keepdims=True) + 1e-6)
    hn_ref[...] = (h * gn_ref[...]).astype(hn_ref.dtype)


def _resid_kernel(x_ref, y_ref, gp_ref, xo_ref):
    y = y_ref[...].astype(F32)
    xo_ref[...] = x_ref[...] + y * lax.rsqrt(jnp.mean(y * y, axis=-1, keepdims=True) + 1e-6) * gp_ref[...]


def resid_norm(x, y, g_post, g_next, tr):
    m, d = x.shape
    row = pl.BlockSpec((tr, d), lambda i: (i, 0))
    vec = pl.BlockSpec((1, d), lambda i: (0, 0))
    if g_next is None:
        return pl.pallas_call(
            _resid_kernel,
            grid=(m // tr,),
            in_specs=[row, row, vec],
            out_specs=row,
            out_shape=jax.ShapeDtypeStruct((m, d), F32),
            compiler_params=_cp(("parallel",), 48),
            name="resid",
        )(x, y, g_post.reshape(1, d)), None
    return pl.pallas_call(
        _resid_norm_kernel,
        grid=(m // tr,),
        in_specs=[row, row, vec, vec],
        out_specs=[row, row],
        out_shape=[jax.ShapeDtypeStruct((m, d), F32), jax.ShapeDtypeStruct((m, d), BF16)],
        compiler_params=_cp(("parallel",), 48),
        name="resid_norm",
    )(x, y, g_post.reshape(1, d), g_next.reshape(1, d))


def _mm_kernel(a_ref, w_ref, o_ref):
    o_ref[...] = jnp.dot(a_ref[...], w_ref[...].astype(BF16), preferred_element_type=F32).astype(o_ref.dtype)


def _row_resident_spec(tm, k, single):
    if single:
        return pl.BlockSpec((tm, k), lambda i, j: (i, 0), pipeline_mode=pl.Buffered(1))
    return pl.BlockSpec((tm, k), lambda i, j: (i, 0))


def matmul(a, w, out_dtype, tm, tn, layer=None, n=None, single_a=False, vmem_mb=56):
    m, k = a.shape
    n = w.shape[-1] if n is None else n
    if layer is None:
        w_spec = pl.BlockSpec((k, tn), lambda i, j: (0, j))
    else:
        w_spec = pl.BlockSpec((None, k, tn), lambda i, j: (layer, 0, j))
    return pl.pallas_call(
        _mm_kernel,
        grid=(m // tm, n // tn),
        in_specs=[_row_resident_spec(tm, k, single_a), w_spec],
        out_specs=pl.BlockSpec((tm, tn), lambda i, j: (i, j)),
        out_shape=jax.ShapeDtypeStruct((m, n), out_dtype),
        compiler_params=_cp(("parallel", "parallel"), vmem_mb),
        name="matmul",
    )(a, w)


def _ffn_in_kernel(a_ref, wg_ref, wu_ref, o_ref):
    a = a_ref[...]
    g = jnp.dot(a, wg_ref[...].astype(BF16), preferred_element_type=F32)
    u = jnp.dot(a, wu_ref[...].astype(BF16), preferred_element_type=F32)
    o_ref[...] = (jax.nn.silu(g) * u).astype(o_ref.dtype)


def ffn_in(a, w, layer, tm, tn):
    m, k = a.shape
    nt = D_FF // tn
    return pl.pallas_call(
        _ffn_in_kernel,
        grid=(m // tm, nt),
        in_specs=[_row_resident_spec(tm, k, True),
                  pl.BlockSpec((None, k, tn), lambda i, j: (layer, 0, j)),
                  pl.BlockSpec((None, k, tn), lambda i, j: (layer, 0, j + nt))],
        out_specs=pl.BlockSpec((tm, tn), lambda i, j: (i, j)),
        out_shape=jax.ShapeDtypeStruct((m, D_FF), BF16),
        compiler_params=_cp(("parallel", "parallel"), 56),
        name="ffn_in",
    )(a, w, w)


def _merge_kernel(xn_ref, y_ref, wg_ref, wb_ref, o_ref, acc):
    b = pl.program_id(2)
    gate = jnp.dot(xn_ref[...], wg_ref[...].astype(BF16), preferred_element_type=F32)
    term = jax.nn.sigmoid(gate) * jnp.dot(y_ref[...], wb_ref[...].astype(BF16), preferred_element_type=F32)

    @pl.when(b == 0)
    def _():
        acc[...] = term

    @pl.when(b > 0)
    def _():
        acc[...] += term

    @pl.when(b == pl.num_programs(2) - 1)
    def _():
        o_ref[...] = acc[...].astype(o_ref.dtype)


def merge_branches(xn, y_all, w_in, w_branch, layer, tm, tn):
    nb, m, kb = y_all.shape
    k = xn.shape[1]
    gate0 = COL_GATE // tn
    per = D_MODEL // tn
    return pl.pallas_call(
        _merge_kernel,
        grid=(m // tm, per, nb),
        in_specs=[pl.BlockSpec((tm, k), lambda i, j, b: (i, 0), pipeline_mode=pl.Buffered(1)),
                  pl.BlockSpec((None, tm, kb), lambda i, j, b: (b, i, 0)),
                  pl.BlockSpec((None, k, tn), lambda i, j, b: (layer, 0, gate0 + b * per + j)),
                  pl.BlockSpec((None, None, kb, tn), lambda i, j, b: (layer, b, 0, j))],
        out_specs=pl.BlockSpec((tm, tn), lambda i, j, b: (i, j)),
        out_shape=jax.ShapeDtypeStruct((m, D_MODEL), BF16),
        scratch_shapes=[pltpu.VMEM((tm, tn), F32)],
        compiler_params=_cp(("parallel", "parallel", "arbitrary"), 56),
        name="merge",
    )(xn, y_all, w_in, w_branch)


def _skip_ref(body, pos):
    def wrapped(*refs):
        return body(*refs[:pos], *refs[pos + 1:])
    return wrapped


def _glu_kernel(y_ref, w_ref, o_ref):
    y = y_ref[...]
    s = jnp.dot(y.astype(BF16), w_ref[...].astype(BF16), preferred_element_type=F32)
    o_ref[...] = (y * jax.nn.sigmoid(s)).astype(o_ref.dtype)


def glu(y, w, layer, tr, ybuf=None, branch=0, row0=0):
    m, d = y.shape
    in_specs = [pl.BlockSpec((tr, d), lambda i: (i, 0)), pl.BlockSpec((None, d, d), lambda i: (layer, 0, 0))]
    if ybuf is None:
        return pl.pallas_call(
            _glu_kernel,
            grid=(m // tr,),
            in_specs=in_specs,
            out_specs=pl.BlockSpec((tr, d), lambda i: (i, 0)),
            out_shape=jax.ShapeDtypeStruct((m, d), BF16),
            compiler_params=_cp(("parallel",), 40),
            name="glu",
        )(y, w)
    return pl.pallas_call(
        _skip_ref(_glu_kernel, 2),
        grid=(m // tr,),
        in_specs=in_specs + [pl.BlockSpec(memory_space=pl.ANY)],
        out_specs=pl.BlockSpec((None, tr, d), lambda i: (branch, row0 // tr + i, 0)),
        out_shape=jax.ShapeDtypeStruct(ybuf.shape, ybuf.dtype),
        input_output_aliases={2: 0},
        compiler_params=_cp(("parallel",), 40),
        name="glu_into",
    )(y, w, ybuf)


def _ln_silu(y, g, b):
    mu = jnp.mean(y, axis=-1, keepdims=True)
    yc = y - mu
    yn = yc * lax.rsqrt(jnp.mean(yc * yc, axis=-1, keepdims=True) + 1e-5)
    return jax.nn.silu(yn * g + b)


CONV_HALO = 32
CONV_LANES = 128


SUBLANES = 8


def _conv_prompt_kernel(z_ref, w_ref, b_ref, g_ref, beta_ref, y_ref, cn_ref, xx, xs, acc, *, tt):
    c = CONV_DIM_
    off = CONV_HALO - (CONV_WIDTH - 1)
    span = tt + CONV_HALO - SUBLANES

    @pl.when(pl.program_id(1) == 0)
    def _():
        xx[0:CONV_HALO, :] = jnp.zeros((CONV_HALO, c), F32)

    xx[CONV_HALO:CONV_HALO + tt, :] = z_ref[:, :c] * jax.nn.sigmoid(z_ref[:, c:])
    for r in range(1, SUBLANES):
        xs[r - 1, 0:span, :] = xx[r:r + span, :]
    for lc in range(c // CONV_LANES):
        ls = slice(lc * CONV_LANES, (lc + 1) * CONV_LANES)
        a = None
        for w in range(CONV_WIDTH):
            q, r = divmod(off + w, SUBLANES)
            lo = SUBLANES * q
            src = xx[lo:lo + tt, ls] if r == 0 else xs[r - 1, lo:lo + tt, ls]
            term = src * w_ref[w:w + 1, ls]
            a = term if a is None else a + term
        acc[:, ls] = a
    y_ref[...] = _ln_silu(acc[...] + b_ref[...], g_ref[...], beta_ref[...]).astype(y_ref.dtype)
    cn_ref[...] = xx[tt + off:tt + CONV_HALO, :]
    xx[0:CONV_HALO, :] = xx[tt:tt + CONV_HALO, :]


CONV_DIM_ = BRANCH_DIM


def conv_prompt(z, nb, seq, w, b, g, beta, tt, ybuf):
    c = CONV_DIM_
    nt = seq // tt
    vec = pl.BlockSpec((1, c), lambda bi, t: (0, 0))
    return pl.pallas_call(
        _skip_ref(functools.partial(_conv_prompt_kernel, tt=tt), 5),
        grid=(nb, nt),
        in_specs=[pl.BlockSpec((tt, 2 * c), lambda bi, t: (bi * nt + t, 0)),
                  pl.BlockSpec((CONV_WIDTH, c), lambda bi, t: (0, 0)), vec, vec, vec,
                  pl.BlockSpec(memory_space=pl.ANY)],
        out_specs=[pl.BlockSpec((None, tt, c), lambda bi, t: (0, bi * nt + t, 0)),
                   pl.BlockSpec((None, CONV_WIDTH - 1, c), lambda bi, t: (bi, 0, 0))],
        out_shape=[jax.ShapeDtypeStruct(ybuf.shape, ybuf.dtype),
                   jax.ShapeDtypeStruct((nb, CONV_WIDTH - 1, c), F32)],
        input_output_aliases={5: 0},
        scratch_shapes=[pltpu.VMEM((tt + CONV_HALO, c), F32),
                        pltpu.VMEM((SUBLANES - 1, tt + CONV_HALO - SUBLANES, c), F32),
                        pltpu.VMEM((tt, c), F32)],
        compiler_params=_cp(("parallel", "arbitrary"), 40),
        name="conv_prompt",
    )(z, w, b.reshape(1, c), g.reshape(1, c), beta.reshape(1, c), ybuf)


def _conv_sample_kernel(z_ref, cache_ref, w_ref, b_ref, g_ref, beta_ref, y_ref, cn_ref, *, steps, nb):
    c = CONV_DIM_
    hist = CONV_WIDTH - 1
    a = z_ref[:, :c] * jax.nn.sigmoid(z_ref[:, c:])
    for t in range(steps):
        acc = None
        for w in range(CONV_WIDTH):
            idx = t + w
            src = cache_ref[idx] if idx < hist else a[(idx - hist) * nb:(idx - hist + 1) * nb]
            term = src * w_ref[w:w + 1, :]
            acc = term if acc is None else acc + term
        y_ref[t * nb:(t + 1) * nb, :] = _ln_silu(acc + b_ref[...], g_ref[...], beta_ref[...]).astype(y_ref.dtype)
    for r in range(hist - steps):
        cn_ref[r] = cache_ref[r + steps]
    for t in range(steps):
        cn_ref[hist - steps + t] = a[t * nb:(t + 1) * nb]


def conv_sample(z, row0, steps, nb, cache_t, w, b, g, beta, ybuf):
    c = CONV_DIM_
    rows = steps * nb
    hist = CONV_WIDTH - 1
    vec = pl.BlockSpec((1, c), lambda i: (0, 0))
    return pl.pallas_call(
        _skip_ref(functools.partial(_conv_sample_kernel, steps=steps, nb=nb), 6),
        grid=(1,),
        in_specs=[pl.BlockSpec((rows, 2 * c), lambda i: (row0 // rows, 0)),
                  pl.BlockSpec((hist, nb, c), lambda i: (0, 0, 0)),
                  pl.BlockSpec((CONV_WIDTH, c), lambda i: (0, 0)), vec, vec, vec,
                  pl.BlockSpec(memory_space=pl.ANY)],
        out_specs=[pl.BlockSpec((None, rows, c), lambda i: (0, row0 // rows, 0)),
                   pl.BlockSpec((hist, nb, c), lambda i: (0, 0, 0))],
        out_shape=[jax.ShapeDtypeStruct(ybuf.shape, ybuf.dtype), jax.ShapeDtypeStruct((hist, nb, c), F32)],
        input_output_aliases={6: 0},
        compiler_params=_cp(("arbitrary",), 40),
        name="conv_sample",
    )(z, cache_t, w, b.reshape(1, c), g.reshape(1, c), beta.reshape(1, c), ybuf)


def _t5_bucket_np(dist):
    n = np.maximum(dist, 0)
    max_exact = N_BUCKETS // 2
    nf = np.maximum(n, 1).astype(np.float32)
    large = max_exact + (np.log(nf / np.float32(max_exact)) / np.float32(math.log(WINDOW / max_exact))
                         * np.float32(N_BUCKETS - max_exact)).astype(np.int32)
    large = np.minimum(large, N_BUCKETS - 1)
    return np.where(n < max_exact, n, large)


def _bias_kernel(t5_ref, bucket_ref, o_ref):
    h = pl.program_id(0)
    bucket = bucket_ref[...]
    acc = jnp.full(bucket.shape, NEG_INF, F32)
    for b in range(N_BUCKETS):
        acc = jnp.where(bucket == b, t5_ref[b, h], acc)
    o_ref[...] = acc


def band_bias(t5_bias, dist, mask):
    bucket = np.where(mask, _t5_bucket_np(dist), -1).astype(np.int32)
    nq, nk = bucket.shape
    return pl.pallas_call(
        _bias_kernel,
        grid=(SWA_HEADS,),
        in_specs=[pl.BlockSpec(memory_space=pltpu.SMEM), pl.BlockSpec((nq, nk), lambda h: (0, 0))],
        out_specs=pl.BlockSpec((None, nq, nk), lambda h: (h, 0, 0)),
        out_shape=jax.ShapeDtypeStruct((SWA_HEADS, nq, nk), F32),
        compiler_params=_cp(("arbitrary",), 16),
        name="band_bias",
    )(t5_bias, jnp.asarray(bucket))


def _softmax_sink_pv(s, sink, v):
    m = jnp.maximum(jnp.max(s, axis=-1, keepdims=True), sink)
    p = jnp.exp(s - m)
    den = jnp.sum(p, axis=-1, keepdims=True) + jnp.exp(sink - m)
    return jnp.dot(p.astype(BF16), v, preferred_element_type=F32) / den


def _swa_prompt_kernel(sink_ref, q_ref, kp_ref, kc_ref, vp_ref, vc_ref, bias_ref, o_ref):
    blk = WINDOW
    scale = SWA_HEAD_DIM ** -0.5
    k = jnp.concatenate([kp_ref[...], kc_ref[...]], axis=0)
    v = jnp.concatenate([vp_ref[...], vc_ref[...]], axis=0)
    col = lax.broadcasted_iota(jnp.int32, (blk, 2 * blk), 1)
    no_prev = jnp.logical_and(pl.program_id(1) == 0, col < blk)
    q = q_ref[...]
    for hp in range(SWA_HEADS // 2):
        outs = []
        for h in (2 * hp, 2 * hp + 1):
            g = h // SWA_REP
            hs = slice(h * SWA_HEAD_DIM, (h + 1) * SWA_HEAD_DIM)
            gs = slice(g * SWA_HEAD_DIM, (g + 1) * SWA_HEAD_DIM)
            s = lax.dot_general(q[:, hs].astype(BF16), k[:, gs].astype(BF16), (((1,), (1,)), ((), ())),
                                preferred_element_type=F32) * scale
            s = jnp.where(no_prev, NEG_INF, s + bias_ref[h])
            outs.append(_softmax_sink_pv(s, sink_ref[h], v[:, gs].astype(BF16)))
        o_ref[:, 2 * hp * SWA_HEAD_DIM:(2 * hp + 2) * SWA_HEAD_DIM] = (
            jnp.concatenate(outs, axis=1).astype(o_ref.dtype))


def swa_prompt(z, nb, seq, sinks, bias, ybuf):
    blk = WINDOW
    nj = seq // blk
    qw = SWA_HEADS * SWA_HEAD_DIM
    kw = SWA_KV_HEADS * SWA_HEAD_DIM
    cur = lambda col: (lambda b, j: (b * nj + j, col))
    prev = lambda col: (lambda b, j: (b * nj + jnp.maximum(j - 1, 0), col))
    return pl.pallas_call(
        _skip_ref(_swa_prompt_kernel, 7),
        grid=(nb, nj),
        in_specs=[pl.BlockSpec(memory_space=pltpu.SMEM),
                  pl.BlockSpec((blk, qw), cur(COL_Q // qw)),
                  pl.BlockSpec((blk, kw), prev(COL_K // kw)), pl.BlockSpec((blk, kw), cur(COL_K // kw)),
                  pl.BlockSpec((blk, kw), prev(COL_V // kw)), pl.BlockSpec((blk, kw), cur(COL_V // kw)),
                  pl.BlockSpec((SWA_HEADS, blk, 2 * blk), lambda b, j: (0, 0, 0)),
                  pl.BlockSpec(memory_space=pl.ANY)],
        out_specs=pl.BlockSpec((None, blk, qw), lambda b, j: (1, b * nj + j, 0)),
        out_shape=jax.ShapeDtypeStruct(ybuf.shape, ybuf.dtype),
        input_output_aliases={7: 0},
        compiler_params=_cp(("parallel", "arbitrary"), 32),
        name="swa_prompt",
    )(sinks, z, z, z, z, z, bias, ybuf)


SWA_SAMPLE_PAD = 4


def _swa_sample_kernel(q_ref, kc_ref, vc_ref, kn_ref, vn_ref, bias_ref, sink_ref, o_ref, wk_ref, wv_ref,
                       kk, vv, *, steps):
    win = WINDOW
    scale = SWA_HEAD_DIM ** -0.5
    kw = SWA_KV_HEADS * SWA_HEAD_DIM
    for buf, cache, new in ((kk, kc_ref, kn_ref), (vv, vc_ref, vn_ref)):
        buf[0:win, :] = cache[...]
        buf[win:win + steps, :] = new[...]
        buf[win + steps:win + steps + SWA_SAMPLE_PAD, :] = jnp.zeros((SWA_SAMPLE_PAD, kw), F32)
    wk_ref[...] = kk[steps:steps + win, :]
    wv_ref[...] = vv[steps:steps + win, :]
    for g in range(SWA_KV_HEADS):
        gs = slice(g * SWA_HEAD_DIM, (g + 1) * SWA_HEAD_DIM)
        s = lax.dot_general(q_ref[g].astype(BF16), kk[:, gs].astype(BF16), (((1,), (1,)), ((), ())),
                            preferred_element_type=F32) * scale + bias_ref[g]
        o_ref[g] = _softmax_sink_pv(s, sink_ref[g], vv[:, gs].astype(BF16))


def swa_sample(q, k_new, v_new, cache_k, cache_v, layer, bias, sink_col):
    nb, _, rows, _ = q.shape
    steps = k_new.shape[1]
    win = WINDOW
    kw = SWA_KV_HEADS * SWA_HEAD_DIM
    nk = win + steps + SWA_SAMPLE_PAD
    cache_spec = pl.BlockSpec((None, None, win, kw), lambda b: (layer, b, 0, 0))
    new_spec = pl.BlockSpec((None, steps, kw), lambda b: (b, 0, 0))
    win_spec = pl.BlockSpec((None, win, kw), lambda b: (b, 0, 0))
    qo_spec = pl.BlockSpec((None, SWA_KV_HEADS, rows, SWA_HEAD_DIM), lambda b: (b, 0, 0, 0))
    return pl.pallas_call(
        functools.partial(_swa_sample_kernel, steps=steps),
        grid=(nb,),
        in_specs=[qo_spec, cache_spec, cache_spec, new_spec, new_spec,
                  pl.BlockSpec((SWA_KV_HEADS, rows, nk), lambda b: (0, 0, 0)),
                  pl.BlockSpec((SWA_KV_HEADS, rows, 1), lambda b: (0, 0, 0))],
        out_specs=[qo_spec, win_spec, win_spec],
        out_shape=[jax.ShapeDtypeStruct(q.shape, F32),
                   jax.ShapeDtypeStruct((nb, win, kw), F32), jax.ShapeDtypeStruct((nb, win, kw), F32)],
        scratch_shapes=[pltpu.VMEM((nk, kw), F32), pltpu.VMEM((nk, kw), F32)],
        compiler_params=_cp(("parallel",), 16),
        name="swa_sample",
    )(q, cache_k, cache_v, k_new, v_new, bias, sink_col)


def _attend(q, mk, mv):
    s = lax.dot_general(q.astype(BF16), mk.astype(BF16), (((1,), (1,)), ((), ())),
                        preferred_element_type=F32) * (MEM_HEAD_DIM ** -0.5)
    m = jnp.max(s, axis=-1, keepdims=True)
    p = jnp.exp(s - m)
    den = jnp.sum(p, axis=-1, keepdims=True)
    return jnp.dot(p.astype(BF16), mv.astype(BF16), preferred_element_type=F32) / den


def _mem_attn_kernel(q0_ref, q1_ref, q2_ref, q3_ref, mk_ref, mv_ref, o_ref):
    for h, q_ref in enumerate((q0_ref, q1_ref, q2_ref, q3_ref)):
        hs = slice(h * MEM_HEAD_DIM, (h + 1) * MEM_HEAD_DIM)
        o_ref[:, hs] = _attend(q_ref[...], mk_ref[:, hs], mv_ref[:, hs]).astype(o_ref.dtype)


def _mem_attn_sample_kernel(q_ref, mk_ref, mv_ref, o_ref):
    for bi in range(q_ref.shape[0]):
        o_ref[bi] = _attend(q_ref[bi], mk_ref[bi], mv_ref[bi]).astype(o_ref.dtype)


def mem_attn_prompt(z, nb, seq, kv, tq, ybuf):
    nt = seq // tq
    mlen = kv.shape[1]
    hd = MEM_HEAD_DIM
    q_specs = [pl.BlockSpec((tq, hd), functools.partial(lambda b, t, h: (b * nt + t, COL_XQ // hd + h), h=h))
               for h in range(MEM_HEADS)]
    return pl.pallas_call(
        _skip_ref(_mem_attn_kernel, 6),
        grid=(nb, nt),
        in_specs=q_specs + [pl.BlockSpec((None, mlen, BRANCH_DIM), lambda b, t: (b, 0, 0)),
                            pl.BlockSpec((None, mlen, BRANCH_DIM), lambda b, t: (b, 0, 1)),
                            pl.BlockSpec(memory_space=pl.ANY)],
        out_specs=pl.BlockSpec((None, tq, BRANCH_DIM), lambda b, t: (3, b * nt + t, 0)),
        out_shape=jax.ShapeDtypeStruct(ybuf.shape, ybuf.dtype),
        input_output_aliases={6: 0},
        compiler_params=_cp(("parallel", "arbitrary"), 32),
        name="mem_attn_prompt",
    )(z, z, z, z, kv, kv, ybuf)


def mem_attn_sample(q, cache_k, cache_v, layer, bb):
    nb, steps, _ = q.shape
    mlen = cache_k.shape[2]
    hd = MEM_HEAD_DIM
    qo_spec = pl.BlockSpec((bb, steps, hd), lambda b, h: (b, 0, h))
    cache_spec = pl.BlockSpec((None, bb, mlen, hd), lambda b, h: (layer, b, 0, h))
    return pl.pallas_call(
        _mem_attn_sample_kernel,
        grid=(nb // bb, MEM_HEADS),
        in_specs=[qo_spec, cache_spec, cache_spec],
        out_specs=qo_spec,
        out_shape=jax.ShapeDtypeStruct((nb, steps, BRANCH_DIM), BF16),
        compiler_params=_cp(("parallel", "parallel"), 32),
        name="mem_attn_sample",
    )(q, cache_k, cache_v)


def _ssm_param_kernel(are_ref, aim_ref, ldt_ref, bre_ref, bim_ref, abr_ref, abi_ref, bbr_ref, bbi_ref):
    dt = jnp.exp(ldt_ref[...])
    ar, ai = are_ref[...], aim_ref[...]
    mag = jnp.exp(dt * ar)
    abr, abi = mag * jnp.cos(dt * ai), mag * jnp.sin(dt * ai)
    den = ar * ar + ai * ai
    nr, ni = abr - 1.0, abi
    fre, fim = (nr * ar + ni * ai) / den, (ni * ar - nr * ai) / den
    abr_ref[...] = abr
    abi_ref[...] = abi
    for c in range(SSM_GROUP):
        br, bi = bre_ref[c], bim_ref[c]
        bbr_ref[c] = fre * br - fim * bi
        bbi_ref[c] = fre * bi + fim * br


def ssm_params(a_re, a_im, log_dt, b_re_t, b_im_t):
    d, g, n = a_re.shape
    c = b_re_t.shape[1]
    gn = pl.BlockSpec((None, g, n), lambda l: (l, 0, 0))
    cgn = pl.BlockSpec((None, c, g, n), lambda l: (l, 0, 0, 0))
    return pl.pallas_call(
        _ssm_param_kernel,
        grid=(d,),
        in_specs=[gn, gn, pl.BlockSpec((None, g, 1), lambda l: (l, 0, 0)), cgn, cgn],
        out_specs=[gn, gn, cgn, cgn],
        out_shape=[jax.ShapeDtypeStruct((d, g, n), F32)] * 2 + [jax.ShapeDtypeStruct((d, c, g, n), F32)] * 2,
        compiler_params=_cp(("arbitrary",), 16),
        name="ssm_params",
    )(a_re, a_im, log_dt.reshape(d, g, 1), b_re_t, b_im_t)


def _cmul_add(ar, ai, hr, hi, br, bi):
    return ar * hr - ai * hi + br, ar * hi + ai * hr + bi


def _ssm_out(u, sre, sim, cre_ref, cim_ref, d_ref):
    y = (jnp.dot(sre[...].astype(BF16), cre_ref[...], preferred_element_type=F32)
         - jnp.dot(sim[...].astype(BF16), cim_ref[...], preferred_element_type=F32)
         + d_ref[...] * u)
    return jax.nn.gelu(y)


def _ssm_prompt_kernel(u_ref, bre_ref, bim_ref, cre_ref, cim_ref, ar_ref, ai_ref, d_ref,
                       y_ref, hr_ref, hi_ref, sre, sim, *, clen):
    np_ = SCAN_LANES
    w = sre.shape[1]
    u = u_ref[...]
    ub = u.astype(BF16)
    sre[...] = jnp.dot(ub, bre_ref[...], preferred_element_type=F32)
    sim[...] = jnp.dot(ub, bim_ref[...], preferred_element_type=F32)
    ar1, ai1 = ar_ref[...], ai_ref[...]
    ar = jnp.broadcast_to(ar1, (np_, w))
    ai = jnp.broadcast_to(ai1, (np_, w))

    def rows(t):
        return pl.ds(pl.multiple_of(t * np_, np_), np_)

    def local_step(t, carry):
        return _cmul_add(ar, ai, carry[0], carry[1], sre[rows(t), :], sim[rows(t), :])

    zero = jnp.zeros((np_, w), F32)
    fr, fi = lax.fori_loop(0, clen, local_step, (zero, zero), unroll=SCAN_UNROLL)

    pr, pi = ar1, ai1
    for _ in range(int(math.log2(clen))):
        pr, pi = pr * pr - pi * pi, 2.0 * pr * pi
    row = lax.broadcasted_iota(jnp.int32, (np_, w), 0)
    cr = jnp.zeros((1, w), F32)
    ci = jnp.zeros((1, w), F32)
    hr0, hi0 = zero, zero
    for p in range(1, np_):
        cr, ci = _cmul_add(pr, pi, cr, ci, fr[p - 1:p], fi[p - 1:p])
        hr0 = jnp.where(row == p, cr, hr0)
        hi0 = jnp.where(row == p, ci, hi0)
    fin_r, fin_i = _cmul_add(pr, pi, cr, ci, fr[np_ - 1:np_], fi[np_ - 1:np_])
    hr_ref[...] = fin_r
    hi_ref[...] = fin_i

    def full_step(t, carry):
        nr, ni = _cmul_add(ar, ai, carry[0], carry[1], sre[rows(t), :], sim[rows(t), :])
        sre[rows(t), :] = nr
        sim[rows(t), :] = ni
        return nr, ni

    lax.fori_loop(0, clen, full_step, (hr0, hi0), unroll=SCAN_UNROLL)
    y_ref[...] = _ssm_out(u, sre, sim, cre_ref, cim_ref, d_ref)


def _ssm_specs(kdim):
    kmap = lambda *idx: (idx[kdim], 0, 0)
    return [pl.BlockSpec((None, SSM_UW, SSM_CW), kmap), pl.BlockSpec((None, SSM_UW, SSM_CW), kmap),
            pl.BlockSpec((None, SSM_CW, SSM_UW), kmap), pl.BlockSpec((None, SSM_CW, SSM_UW), kmap),
            pl.BlockSpec((None, 1, SSM_CW), kmap), pl.BlockSpec((None, 1, SSM_CW), kmap),
            pl.BlockSpec((None, 1, SSM_UW), kmap)]


def ssm_prompt(u_perm, mats):
    nb, seq, _ = u_perm.shape
    clen = seq // SCAN_LANES
    nstate = SSM_GROUPS * SSM_STATE
    st = pl.BlockSpec((None, 1, SSM_CW), lambda b, k: (b, 0, k))
    return pl.pallas_call(
        functools.partial(_ssm_prompt_kernel, clen=clen),
        grid=(nb, SSM_CHUNKS),
        in_specs=[pl.BlockSpec((None, seq, SSM_UW), lambda b, k: (b, 0, k))] + _ssm_specs(1),
        out_specs=[pl.BlockSpec((None, seq, SSM_UW), lambda b, k: (b, 0, k)), st, st],
        out_shape=[jax.ShapeDtypeStruct((nb, seq, BRANCH_DIM), F32),
                   jax.ShapeDtypeStruct((nb, 1, nstate), F32), jax.ShapeDtypeStruct((nb, 1, nstate), F32)],
        scratch_shapes=[pltpu.VMEM((seq, SSM_CW), F32), pltpu.VMEM((seq, SSM_CW), F32)],
        compiler_params=_cp(("parallel", "parallel"), 48),
        name="ssm_prompt",
    )(u_perm, *mats)


def _ssm_sample_kernel(u_ref, bre_ref, bim_ref, cre_ref, cim_ref, ar_ref, ai_ref, d_ref, h0r_ref, h0i_ref,
                       y_ref, hr_ref, hi_ref, sre, sim, *, steps, nb):
    w = sre.shape[1]
    u = u_ref[...]
    ub = u.astype(BF16)
    bur = jnp.dot(ub, bre_ref[...], preferred_element_type=F32)
    bui = jnp.dot(ub, bim_ref[...], preferred_element_type=F32)
    ar = jnp.broadcast_to(ar_ref[...], (nb, w))
    ai = jnp.broadcast_to(ai_ref[...], (nb, w))
    hr, hi = h0r_ref[...], h0i_ref[...]
    for t in range(steps):
        rs = slice(t * nb, (t + 1) * nb)
        hr, hi = _cmul_add(ar, ai, hr, hi, bur[rs], bui[rs])
        sre[rs, :] = hr
        sim[rs, :] = hi
    hr_ref[...] = hr
    hi_ref[...] = hi
    y_ref[...] = _ssm_out(u, sre, sim, cre_ref, cim_ref, d_ref)


def ssm_sample(z, row0, steps, nb, mats, h0_re, h0_im, layer):
    rows = steps * nb
    nstate = SSM_GROUPS * SSM_STATE
    st_in = pl.BlockSpec((None, nb, SSM_CW), lambda k: (layer, 0, k))
    st_out = pl.BlockSpec((nb, SSM_CW), lambda k: (0, k))
    return pl.pallas_call(
        functools.partial(_ssm_sample_kernel, steps=steps, nb=nb),
        grid=(SSM_CHUNKS,),
        in_specs=[pl.BlockSpec((rows, SSM_UW), lambda k: (row0 // rows, COL_U // SSM_UW + k))]
                 + _ssm_specs(0) + [st_in, st_in],
        out_specs=[pl.BlockSpec((rows, SSM_UW), lambda k: (0, k)), st_out, st_out],
        out_shape=[jax.ShapeDtypeStruct((rows, BRANCH_DIM), F32),
                   jax.ShapeDtypeStruct((nb, nstate), F32), jax.ShapeDtypeStruct((nb, nstate), F32)],
        scratch_shapes=[pltpu.VMEM((rows, SSM_CW), F32), pltpu.VMEM((rows, SSM_CW), F32)],
        compiler_params=_cp(("parallel",), 32),
        name="ssm_sample",
    )(z, *mats, h0_re, h0_im)


def _block_diag(x, pattern):
    eye = jnp.eye(SSM_GROUPS // SSM_CHUNKS, dtype=x.dtype)
    return jnp.einsum(pattern, x, eye)


def kernel(x_prompt, x_sample, cache_conv, cache_win_k, cache_win_v, state_ssm_re, state_ssm_im, cache_mem_k, cache_mem_v, mem_prompt, t5_bias, norm_mix_pre, norm_mix_post, norm_ffn_pre, norm_ffn_post, norm_mem, w_in, conv_w, conv_b, conv_ln_g, conv_ln_b, attn_sinks, ssm_a_re, ssm_a_im, ssm_log_dt, ssm_b_re, ssm_b_im, ssm_c_re, ssm_c_im, ssm_d, ssm_w_glu, w_mem_kv, w_branch, w_out, w_ffn_in, w_ffn_out):
    bp, seq, d = x_prompt.shape
    bs, steps, _ = x_sample.shape
    rows_p = bp * seq
    rows_s = bs * steps
    rows = rows_p + rows_s
    mlen = mem_prompt.shape[1]
    hist = CONV_WIDTH - 1
    kw = SWA_KV_HEADS * SWA_HEAD_DIM
    gpc = SSM_GROUPS // SSM_CHUNKS
    clen = seq // SCAN_LANES

    tm = rows // 4
    tm_merge = rows // 8
    tm_ffn_out = rows // 8
    tr = 320

    qi = np.arange(WINDOW)
    ki = np.arange(2 * WINDOW) - WINDOW
    dist_p = qi[:, None] - ki[None, :]
    bias_p = band_bias(t5_bias, dist_p, (dist_p >= 0) & (dist_p < WINDOW))
    nk_s = WINDOW + steps + SWA_SAMPLE_PAD
    qrows = 8
    dist_s = (WINDOW + np.arange(qrows))[:, None] - np.arange(nk_s)[None, :]
    mask_s = (dist_s >= 0) & (dist_s < WINDOW) & (np.arange(nk_s)[None, :] < WINDOW + steps)
    bias_s = band_bias(t5_bias, dist_s, mask_s)[:, :steps]
    bias_s = bias_s.reshape(SWA_KV_HEADS, SWA_REP, steps, nk_s).transpose(0, 2, 1, 3)
    bias_s = bias_s.reshape(SWA_KV_HEADS, steps * SWA_REP, nk_s)

    abar_re, abar_im, bbar_re, bbar_im = ssm_params(
        ssm_a_re, ssm_a_im, ssm_log_dt, ssm_b_re.transpose(0, 3, 1, 2), ssm_b_im.transpose(0, 3, 1, 2))

    def in_mat(x):
        x = x.reshape(DEPTH, SSM_GROUP, SSM_CHUNKS, gpc, SSM_STATE)
        return _block_diag(x, 'lckgn,gh->lkgchn').reshape(DEPTH, SSM_CHUNKS, SSM_UW, SSM_CW).astype(BF16)

    def out_mat(x):
        x = x.reshape(DEPTH, SSM_CHUNKS, gpc, SSM_GROUP, SSM_STATE)
        return _block_diag(x, 'lkgcn,gh->lkgnhc').reshape(DEPTH, SSM_CHUNKS, SSM_CW, SSM_UW).astype(BF16)

    bmat_re, bmat_im = in_mat(bbar_re), in_mat(bbar_im)
    cmat_re, cmat_im = out_mat(ssm_c_re), out_mat(ssm_c_im)
    abar_re = abar_re.reshape(DEPTH, SSM_CHUNKS, 1, SSM_CW)
    abar_im = abar_im.reshape(DEPTH, SSM_CHUNKS, 1, SSM_CW)
    dvec = ssm_d.reshape(DEPTH, SSM_CHUNKS, 1, SSM_UW)

    cache_conv_t = cache_conv.transpose(0, 2, 1, 3)
    cache_k2 = cache_win_k.reshape(DEPTH, bs, -1, kw)
    cache_v2 = cache_win_v.reshape(DEPTH, bs, -1, kw)
    cache_mk = cache_mem_k.reshape(DEPTH, bs, mlen, BRANCH_DIM)
    cache_mv = cache_mem_v.reshape(DEPTH, bs, mlen, BRANCH_DIM)
    h0_re = state_ssm_re.reshape(DEPTH, bs, -1)
    h0_im = state_ssm_im.reshape(DEPTH, bs, -1)
    mem_rows = mem_prompt.reshape(bp * mlen, d)

    x = jnp.concatenate([x_prompt.reshape(rows_p, d), x_sample.transpose(1, 0, 2).reshape(rows_s, d)], axis=0)
    xn = rmsnorm_bf16(x, norm_mix_pre[0], tr)

    outs = [[] for _ in range(12)]
    for l in range(DEPTH):
        z = matmul(xn, w_in, F32, tm, 512, layer=l, n=COL_GATE, single_a=True)
        z_s = lax.slice(z, (rows_p, 0), (rows, COL_GATE))
        ybuf = jnp.zeros((4, rows, BRANCH_DIM), BF16)

        ybuf, conv_p = conv_prompt(z, bp, seq, conv_w[l], conv_b[l], conv_ln_g[l], conv_ln_b[l], 256, ybuf)
        ybuf, conv_s_t = conv_sample(z, rows_p, steps, bs, cache_conv_t[l], conv_w[l], conv_b[l],
                                     conv_ln_g[l], conv_ln_b[l], ybuf)

        ybuf = swa_prompt(z, bp, seq, attn_sinks[l], bias_p, ybuf)
        q_s = z_s[:, COL_Q:COL_K].reshape(steps, bs, SWA_KV_HEADS, SWA_REP, SWA_HEAD_DIM)
        q_s = q_s.transpose(1, 2, 0, 3, 4).reshape(bs, SWA_KV_HEADS, steps * SWA_REP, SWA_HEAD_DIM)
        k_s = z_s[:, COL_K:COL_V].reshape(steps, bs, kw).transpose(1, 0, 2)
        v_s = z_s[:, COL_V:COL_U].reshape(steps, bs, kw).transpose(1, 0, 2)
        sink_col = jnp.tile(attn_sinks[l].reshape(SWA_KV_HEADS, 1, SWA_REP), (1, steps, 1))
        sink_col = sink_col.reshape(SWA_KV_HEADS, steps * SWA_REP, 1)
        ob_s, wk_s, wv_s = swa_sample(q_s, k_s, v_s, cache_k2, cache_v2, l, bias_s, sink_col)
        yb_s = ob_s.reshape(bs, SWA_KV_HEADS, steps, SWA_REP, SWA_HEAD_DIM).transpose(2, 0, 1, 3, 4)
        yb_s = yb_s.reshape(rows_s, BRANCH_DIM).astype(BF16)

        mats = (bmat_re[l], bmat_im[l], cmat_re[l], cmat_im[l], abar_re[l], abar_im[l], dvec[l])
        u_perm = lax.slice(z, (0, COL_U), (rows_p, COL_XQ))
        u_perm = u_perm.reshape(bp, SCAN_LANES, clen, BRANCH_DIM).transpose(0, 2, 1, 3)
        yg_perm, hr_p, hi_p = ssm_prompt(u_perm.reshape(bp, seq, BRANCH_DIM), mats)
        yc_perm = glu(yg_perm.reshape(rows_p, BRANCH_DIM), ssm_w_glu, l, 1024)
        yc_p = yc_perm.reshape(bp, clen, SCAN_LANES, BRANCH_DIM).transpose(0, 2, 1, 3).reshape(rows_p, BRANCH_DIM)
        yg_s, hr_s, hi_s = ssm_sample(z, rows_p, steps, bs, mats, h0_re, h0_im, l)
        ybuf = glu(yg_s, ssm_w_glu, l, rows_s, ybuf, 2, rows_p)

        kv = matmul(rmsnorm_bf16(mem_rows, norm_mem[l], 256), w_mem_kv, F32, bp * mlen, 512, layer=l)
        kv = kv.reshape(bp, mlen, 2 * BRANCH_DIM)
        ybuf = mem_attn_prompt(z, bp, seq, kv, 512, ybuf)
        xq_s = z_s[:, COL_XQ:COL_GATE].reshape(steps, bs, BRANCH_DIM).transpose(1, 0, 2)
        yx_s = mem_attn_sample(xq_s, cache_mk, cache_mv, l, 8).transpose(1, 0, 2).reshape(rows_s, BRANCH_DIM)

        ybuf = lax.dynamic_update_slice(ybuf, yb_s[None], (1, rows_p, 0))
        ybuf = lax.dynamic_update_slice(ybuf, yc_p[None], (2, 0, 0))
        ybuf = lax.dynamic_update_slice(ybuf, yx_s[None], (3, rows_p, 0))
        merged = merge_branches(xn, ybuf, w_in, w_branch, l, tm_merge, 512)
        mix = matmul(merged, w_out, BF16, tm, 512, layer=l, single_a=True)
        x, hn = resid_norm(x, mix, norm_mix_post[l], norm_ffn_pre[l], tr)

        hid = ffn_in(hn, w_ffn_in, l, tm, 256)
        f = matmul(hid, w_ffn_out, BF16, tm_ffn_out, 256, layer=l, single_a=True)
        x, xn = resid_norm(x, f, norm_ffn_post[l], norm_mix_pre[l + 1] if l + 1 < DEPTH else None, tr)

        kv_win = jnp.stack([lax.slice(z, ((b + 1) * seq - WINDOW, COL_K), ((b + 1) * seq, COL_U))
                            for b in range(bp)])
        new = (conv_p, conv_s_t.transpose(1, 0, 2),
               kv_win[..., :kw].reshape(bp, WINDOW, SWA_KV_HEADS, SWA_HEAD_DIM),
               kv_win[..., kw:].reshape(bp, WINDOW, SWA_KV_HEADS, SWA_HEAD_DIM),
               wk_s.reshape(bs, WINDOW, SWA_KV_HEADS, SWA_HEAD_DIM),
               wv_s.reshape(bs, WINDOW, SWA_KV_HEADS, SWA_HEAD_DIM),
               hr_p.reshape(bp, SSM_GROUPS, SSM_STATE), hi_p.reshape(bp, SSM_GROUPS, SSM_STATE),
               hr_s.reshape(bs, SSM_GROUPS, SSM_STATE), hi_s.reshape(bs, SSM_GROUPS, SSM_STATE),
               kv[..., :BRANCH_DIM].reshape(bp, mlen, MEM_HEADS, MEM_HEAD_DIM),
               kv[..., BRANCH_DIM:].reshape(bp, mlen, MEM_HEADS, MEM_HEAD_DIM))
        for acc, val in zip(outs, new):
            acc.append(val)

    y_prompt = x[:rows_p].reshape(bp, seq, d)
    y_sample = x[rows_p:].reshape(steps, bs, d).transpose(1, 0, 2)
    return (y_prompt, y_sample) + tuple(jnp.stack(o) for o in outs)
```

```python
import functools
import math

import jax
import jax.numpy as jnp
import numpy as np
from jax import lax
from jax.experimental import pallas as pl
from jax.experimental.pallas import tpu as pltpu

F32 = jnp.float32
BF16 = jnp.bfloat16

D_MODEL = 4096
DEPTH = 4
BRANCH_DIM = 1024
CONV_WIDTH = 31
SWA_HEAD_DIM = 64
SWA_HEADS = 16
SWA_KV_HEADS = 4
SWA_REP = 4
WINDOW = 128
N_BUCKETS = 32
SSM_GROUP = 16
SSM_GROUPS = 64
SSM_STATE = 64
MEM_HEADS = 4
MEM_HEAD_DIM = 256
D_FF = 11008
COL_Q = 2048
COL_K = 3072
COL_V = 3328
COL_U = 3584
COL_XQ = 4608
COL_GATE = 5632
N_IN = COL_GATE + 4 * D_MODEL
NEG_INF = -1e30

SSM_CHUNKS = 4
SSM_CW = SSM_GROUPS // SSM_CHUNKS * SSM_STATE
SSM_UW = SSM_GROUPS // SSM_CHUNKS * SSM_GROUP
SCAN_LANES = 8
SCAN_UNROLL = 8


def _cp(sem, vmem_mb):
    return pltpu.CompilerParams(dimension_semantics=sem, vmem_limit_bytes=vmem_mb << 20)


def _rmsnorm_kernel(x_ref, g_ref, o_ref):
    x = x_ref[...]
    y = x * lax.rsqrt(jnp.mean(x * x, axis=-1, keepdims=True) + 1e-6)
    o_ref[...] = (y * g_ref[...]).astype(o_ref.dtype)


def rmsnorm_bf16(x, g, tr):
    m, d = x.shape
    return pl.pallas_call(
        _rmsnorm_kernel,
        grid=(m // tr,),
        in_specs=[pl.BlockSpec((tr, d), lambda i: (i, 0)), pl.BlockSpec((1, d), lambda i: (0, 0))],
        out_specs=pl.BlockSpec((tr, d), lambda i: (i, 0)),
        out_shape=jax.ShapeDtypeStruct((m, d), BF16),
        compiler_params=_cp(("parallel",), 40),
        name="rmsnorm",
    )(x, g.reshape(1, d))


def _resid_norm_kernel(x_ref, y_ref, gp_ref, gn_ref, xo_ref, hn_ref):
    y = y_ref[...].astype(F32)
    yn = y * lax.rsqrt(jnp.mean(y * y, axis=-1, keepdims=True) + 1e-6) * gp_ref[...]
    x = x_ref[...] + yn
    xo_ref[...] = x
    h = x * lax.rsqrt(jnp.mean(x * x, axis=-1, keepdims=True) + 1e-6)
    hn_ref[...] = (h * gn_ref[...]).astype(hn_ref.dtype)


def _resid_kernel(x_ref, y_ref, gp_ref, xo_ref):
    y = y_ref[...].astype(F32)
    xo_ref[...] = x_ref[...] + y * lax.rsqrt(jnp.mean(y * y, axis=-1, keepdims=True) + 1e-6) * gp_ref[...]


def resid_norm(x, y, g_post, g_next, tr):
    m, d = x.shape
    row = pl.BlockSpec((tr, d), lambda i: (i, 0))
    vec = pl.BlockSpec((1, d), lambda i: (0, 0))
    if g_next is None:
        return pl.pallas_call(
            _resid_kernel,
            grid=(m // tr,),
            in_specs=[row, row, vec],
            out_specs=row,
            out_shape=jax.ShapeDtypeStruct((m, d), F32),
            compiler_params=_cp(("parallel",), 48),
            name="resid",
        )(x, y, g_post.reshape(1, d)), None
    return pl.pallas_call(
        _resid_norm_kernel,
        grid=(m // tr,),
        in_specs=[row, row, vec, vec],
        out_specs=[row, row],
        out_shape=[jax.ShapeDtypeStruct((m, d), F32), jax.ShapeDtypeStruct((m, d), BF16)],
        compiler_params=_cp(("parallel",), 48),
        name="resid_norm",
    )(x, y, g_post.reshape(1, d), g_next.reshape(1, d))


def _mm_kernel(a_ref, w_ref, o_ref):
    o_ref[...] = jnp.dot(a_ref[...], w_ref[...].astype(BF16), preferred_element_type=F32).astype(o_ref.dtype)


def _row_resident_spec(tm, k, single):
    if single:
        return pl.BlockSpec((tm, k), lambda i, j: (i, 0), pipeline_mode=pl.Buffered(1))
    return pl.BlockSpec((tm, k), lambda i, j: (i, 0))


def matmul(a, w, out_dtype, tm, tn, layer=None, n=None, single_a=False, vmem_mb=56):
    m, k = a.shape
    n = w.shape[-1] if n is None else n
    if layer is None:
        w_spec = pl.BlockSpec((k, tn), lambda i, j: (0, j))
    else:
        w_spec = pl.BlockSpec((None, k, tn), lambda i, j: (layer, 0, j))
    return pl.pallas_call(
        _mm_kernel,
        grid=(m // tm, n // tn),
        in_specs=[_row_resident_spec(tm, k, single_a), w_spec],
        out_specs=pl.BlockSpec((tm, tn), lambda i, j: (i, j)),
        out_shape=jax.ShapeDtypeStruct((m, n), out_dtype),
        compiler_params=_cp(("parallel", "parallel"), vmem_mb),
        name="matmul",
    )(a, w)


def _ffn_in_kernel(a_ref, wg_ref, wu_ref, o_ref):
    a = a_ref[...]
    g = jnp.dot(a, wg_ref[...].astype(BF16), preferred_element_type=F32)
    u = jnp.dot(a, wu_ref[...].astype(BF16), preferred_element_type=F32)
    o_ref[...] = (jax.nn.silu(g) * u).astype(o_ref.dtype)


def ffn_in(a, w, layer, tm, tn):
    m, k = a.shape
    nt = D_FF // tn
    return pl.pallas_call(
        _ffn_in_kernel,
        grid=(m // tm, nt),
        in_specs=[_row_resident_spec(tm, k, True),
                  pl.BlockSpec((None, k, tn), lambda i, j: (layer, 0, j)),
                  pl.BlockSpec((None, k, tn), lambda i, j: (layer, 0, j + nt))],
        out_specs=pl.BlockSpec((tm, tn), lambda i, j: (i, j)),
        out_shape=jax.ShapeDtypeStruct((m, D_FF), BF16),
        compiler_params=_cp(("parallel", "parallel"), 56),
        name="ffn_in",
    )(a, w, w)


def _merge_kernel(xn_ref, y_ref, wg_ref, wb_ref, o_ref, acc):
    b = pl.program_id(2)
    gate = jnp.dot(xn_ref[...], wg_ref[...].astype(BF16), preferred_element_type=F32)
    term = jax.nn.sigmoid(gate) * jnp.dot(y_ref[...], wb_ref[...].astype(BF16), preferred_element_type=F32)

    @pl.when(b == 0)
    def _():
        acc[...] = term

    @pl.when(b > 0)
    def _():
        acc[...] += term

    @pl.when(b == pl.num_programs(2) - 1)
    def _():
        o_ref[...] = acc[...].astype(o_ref.dtype)


def merge_branches(xn, y_all, w_in, w_branch, layer, tm, tn):
    nb, m, kb = y_all.shape
    k = xn.shape[1]
    gate0 = COL_GATE // tn
    per = D_MODEL // tn
    return pl.pallas_call(
        _merge_kernel,
        grid=(m // tm, per, nb),
        in_specs=[pl.BlockSpec((tm, k), lambda i, j, b: (i, 0), pipeline_mode=pl.Buffered(1)),
                  pl.BlockSpec((None, tm, kb), lambda i, j, b: (b, i, 0)),
                  pl.BlockSpec((None, k, tn), lambda i, j, b: (layer, 0, gate0 + b * per + j)),
                  pl.BlockSpec((None, None, kb, tn), lambda i, j, b: (layer, b, 0, j))],
        out_specs=pl.BlockSpec((tm, tn), lambda i, j, b: (i, j)),
        out_shape=jax.ShapeDtypeStruct((m, D_MODEL), BF16),
        scratch_shapes=[pltpu.VMEM((tm, tn), F32)],
        compiler_params=_cp(("parallel", "parallel", "arbitrary"), 56),
        name="merge",
    )(xn, y_all, w_in, w_branch)


def _skip_ref(body, pos):
    def wrapped(*refs):
        return body(*refs[:pos], *refs[pos + 1:])
    return wrapped


def _glu_kernel(y_ref, w_ref, o_ref):
    y = y_ref[...]
    s = jnp.dot(y.astype(BF16), w_ref[...].astype(BF16), preferred_element_type=F32)
    o_ref[...] = (y * jax.nn.sigmoid(s)).astype(o_ref.dtype)


def glu(y, w, layer, tr, ybuf=None, branch=0, row0=0):
    m, d = y.shape
    in_specs = [pl.BlockSpec((tr, d), lambda i: (i, 0)), pl.BlockSpec((None, d, d), lambda i: (layer, 0, 0))]
    if ybuf is None:
        return pl.pallas_call(
            _glu_kernel,
            grid=(m // tr,),
            in_specs=in_specs,
            out_specs=pl.BlockSpec((tr, d), lambda i: (i, 0)),
            out_shape=jax.ShapeDtypeStruct((m, d), BF16),
            compiler_params=_cp(("parallel",), 40),
            name="glu",
        )(y, w)
    return pl.pallas_call(
        _skip_ref(_glu_kernel, 2),
        grid=(m // tr,),
        in_specs=in_specs + [pl.BlockSpec(memory_space=pl.ANY)],
        out_specs=pl.BlockSpec((None, tr, d), lambda i: (branch, row0 // tr + i, 0)),
        out_shape=jax.ShapeDtypeStruct(ybuf.shape, ybuf.dtype),
        input_output_aliases={2: 0},
        compiler_params=_cp(("parallel",), 40),
        name="glu_into",
    )(y, w, ybuf)


def _ln_silu(y, g, b):
    mu = jnp.mean(y, axis=-1, keepdims=True)
    yc = y - mu
    yn = yc * lax.rsqrt(jnp.mean(yc * yc, axis=-1, keepdims=True) + 1e-5)
    return jax.nn.silu(yn * g + b)


CONV_HALO = 32
CONV_LANES = 128


SUBLANES = 8


def _conv_prompt_kernel(z_ref, w_ref, b_ref, g_ref, beta_ref, y_ref, cn_ref, xx, xs, acc, *, tt):
    c = CONV_DIM_
    off = CONV_HALO - (CONV_WIDTH - 1)
    span = tt + CONV_HALO - SUBLANES

    @pl.when(pl.program_id(1) == 0)
    def _():
        xx[0:CONV_HALO, :] = jnp.zeros((CONV_HALO, c), F32)

    xx[CONV_HALO:CONV_HALO + tt, :] = z_ref[:, :c] * jax.nn.sigmoid(z_ref[:, c:])
    for r in range(1, SUBLANES):
        xs[r - 1, 0:span, :] = xx[r:r + span, :]
    for lc in range(c // CONV_LANES):
        ls = slice(lc * CONV_LANES, (lc + 1) * CONV_LANES)
        a = None
        for w in range(CONV_WIDTH):
            q, r = divmod(off + w, SUBLANES)
            lo = SUBLANES * q
            src = xx[lo:lo + tt, ls] if r == 0 else xs[r - 1, lo:lo + tt, ls]
            term = src * w_ref[w:w + 1, ls]
            a = term if a is None else a + term
        acc[:, ls] = a
    y_ref[...] = _ln_silu(acc[...] + b_ref[...], g_ref[...], beta_ref[...]).astype(y_ref.dtype)
    cn_ref[...] = xx[tt + off:tt + CONV_HALO, :]
    xx[0:CONV_HALO, :] = xx[tt:tt + CONV_HALO, :]


CONV_DIM_ = BRANCH_DIM


def conv_prompt(z, nb, seq, w, b, g, beta, tt, ybuf):
    c = CONV_DIM_
    nt = seq // tt
    vec = pl.BlockSpec((1, c), lambda bi, t: (0, 0))
    return pl.pallas_call(
        _skip_ref(functools.partial(_conv_prompt_kernel, tt=tt), 5),
        grid=(nb, nt),
        in_specs=[pl.BlockSpec((tt, 2 * c), lambda bi, t: (bi * nt + t, 0)),
                  pl.BlockSpec((CONV_WIDTH, c), lambda bi, t: (0, 0)), vec, vec, vec,
                  pl.BlockSpec(memory_space=pl.ANY)],
        out_specs=[pl.BlockSpec((None, tt, c), lambda bi, t: (0, bi * nt + t, 0)),
                   pl.BlockSpec((None, CONV_WIDTH - 1, c), lambda bi, t: (bi, 0, 0))],
        out_shape=[jax.ShapeDtypeStruct(ybuf.shape, ybuf.dtype),
                   jax.ShapeDtypeStruct((nb, CONV_WIDTH - 1, c), F32)],
        input_output_aliases={5: 0},
        scratch_shapes=[pltpu.VMEM((tt + CONV_HALO, c), F32),
                        pltpu.VMEM((SUBLANES - 1, tt + CONV_HALO - SUBLANES, c), F32),
                        pltpu.VMEM((tt, c), F32)],
        compiler_params=_cp(("parallel", "arbitrary"), 40),
        name="conv_prompt",
    )(z, w, b.reshape(1, c), g.reshape(1, c), beta.reshape(1, c), ybuf)


def _conv_sample_kernel(z_ref, cache_ref, w_ref, b_ref, g_ref, beta_ref, y_ref, cn_ref, *, steps, nb):
    c = CONV_DIM_
    hist = CONV_WIDTH - 1
    a = z_ref[:, :c] * jax.nn.sigmoid(z_ref[:, c:])
    for t in range(steps):
        acc = None
        for w in range(CONV_WIDTH):
            idx = t + w
            src = cache_ref[idx] if idx < hist else a[(idx - hist) * nb:(idx - hist + 1) * nb]
            term = src * w_ref[w:w + 1, :]
            acc = term if acc is None else acc + term
        y_ref[t * nb:(t + 1) * nb, :] = _ln_silu(acc + b_ref[...], g_ref[...], beta_ref[...]).astype(y_ref.dtype)
    for r in range(hist - steps):
        cn_ref[r] = cache_ref[r + steps]
    for t in range(steps):
        cn_ref[hist - steps + t] = a[t * nb:(t + 1) * nb]


def conv_sample(z, row0, steps, nb, cache_t, w, b, g, beta, ybuf):
    c = CONV_DIM_
    rows = steps * nb
    hist = CONV_WIDTH - 1
    vec = pl.BlockSpec((1, c), lambda i: (0, 0))
    return pl.pallas_call(
        _skip_ref(functools.partial(_conv_sample_kernel, steps=steps, nb=nb), 6),
        grid=(1,),
        in_specs=[pl.BlockSpec((rows, 2 * c), lambda i: (row0 // rows, 0)),
                  pl.BlockSpec((hist, nb, c), lambda i: (0, 0, 0)),
                  pl.BlockSpec((CONV_WIDTH, c), lambda i: (0, 0)), vec, vec, vec,
                  pl.BlockSpec(memory_space=pl.ANY)],
        out_specs=[pl.BlockSpec((None, rows, c), lambda i: (0, row0 // rows, 0)),
                   pl.BlockSpec((hist, nb, c), lambda i: (0, 0, 0))],
        out_shape=[jax.ShapeDtypeStruct(ybuf.shape, ybuf.dtype), jax.ShapeDtypeStruct((hist, nb, c), F32)],
        input_output_aliases={6: 0},
        compiler_params=_cp(("arbitrary",), 40),
        name="conv_sample",
    )(z, cache_t, w, b.reshape(1, c), g.reshape(1, c), beta.reshape(1, c), ybuf)


def _t5_bucket_np(dist):
    n = np.maximum(dist, 0)
    max_exact = N_BUCKETS // 2
    nf = np.maximum(n, 1).astype(np.float32)
    large = max_exact + (np.log(nf / np.float32(max_exact)) / np.float32(math.log(WINDOW / max_exact))
                         * np.float32(N_BUCKETS - max_exact)).astype(np.int32)
    large = np.minimum(large, N_BUCKETS - 1)
    return np.where(n < max_exact, n, large)


def _bias_kernel(t5_ref, bucket_ref, o_ref):
    h = pl.program_id(0)
    bucket = bucket_ref[...]
    acc = jnp.full(bucket.shape, NEG_INF, F32)
    for b in range(N_BUCKETS):
        acc = jnp.where(bucket == b, t5_ref[b, h], acc)
    o_ref[...] = acc


def band_bias(t5_bias, dist, mask):
    bucket = np.where(mask, _t5_bucket_np(dist), -1).astype(np.int32)
    nq, nk = bucket.shape
    return pl.pallas_call(
        _bias_kernel,
        grid=(SWA_HEADS,),
        in_specs=[pl.BlockSpec(memory_space=pltpu.SMEM), pl.BlockSpec((nq, nk), lambda h: (0, 0))],
        out_specs=pl.BlockSpec((None, nq, nk), lambda h: (h, 0, 0)),
        out_shape=jax.ShapeDtypeStruct((SWA_HEADS, nq, nk), F32),
        compiler_params=_cp(("arbitrary",), 16),
        name="band_bias",
    )(t5_bias, jnp.asarray(bucket))


def _softmax_sink_pv(s, sink, v):
    m = jnp.maximum(jnp.max(s, axis=-1, keepdims=True), sink)
    p = jnp.exp(s - m)
    den = jnp.sum(p, axis=-1, keepdims=True) + jnp.exp(sink - m)
    return jnp.dot(p.astype(BF16), v, preferred_element_type=F32) / den


def _swa_prompt_kernel(sink_ref, q_ref, kp_ref, kc_ref, vp_ref, vc_ref, bias_ref, o_ref):
    blk = WINDOW
    scale = SWA_HEAD_DIM ** -0.5
    k = jnp.concatenate([kp_ref[...], kc_ref[...]], axis=0)
    v = jnp.concatenate([vp_ref[...], vc_ref[...]], axis=0)
    col = lax.broadcasted_iota(jnp.int32, (blk, 2 * blk), 1)
    no_prev = jnp.logical_and(pl.program_id(1) == 0, col < blk)
    q = q_ref[...]
    for hp in range(SWA_HEADS // 2):
        outs = []
        for h in (2 * hp, 2 * hp + 1):
            g = h // SWA_REP
            hs = slice(h * SWA_HEAD_DIM, (h + 1) * SWA_HEAD_DIM)
            gs = slice(g * SWA_HEAD_DIM, (g + 1) * SWA_HEAD_DIM)
            s = lax.dot_general(q[:, hs].astype(BF16), k[:, gs].astype(BF16), (((1,), (1,)), ((), ())),
                                preferred_element_type=F32) * scale
            s = jnp.where(no_prev, NEG_INF, s + bias_ref[h])
            outs.append(_softmax_sink_pv(s, sink_ref[h], v[:, gs].astype(BF16)))
        o_ref[:, 2 * hp * SWA_HEAD_DIM:(2 * hp + 2) * SWA_HEAD_DIM] = (
            jnp.concatenate(outs, axis=1).astype(o_ref.dtype))


def swa_prompt(z, nb, seq, sinks, bias, ybuf):
    blk = WINDOW
    nj = seq // blk
    qw = SWA_HEADS * SWA_HEAD_DIM
    kw = SWA_KV_HEADS * SWA_HEAD_DIM
    cur = lambda col: (lambda b, j: (b * nj + j, col))
    prev = lambda col: (lambda b, j: (b * nj + jnp.maximum(j - 1, 0), col))
    return pl.pallas_call(
        _skip_ref(_swa_prompt_kernel, 7),
        grid=(nb, nj),
        in_specs=[pl.BlockSpec(memory_space=pltpu.SMEM),
                  pl.BlockSpec((blk, qw), cur(COL_Q // qw)),
                  pl.BlockSpec((blk, kw), prev(COL_K // kw)), pl.BlockSpec((blk, kw), cur(COL_K // kw)),
                  pl.BlockSpec((blk, kw), prev(COL_V // kw)), pl.BlockSpec((blk, kw), cur(COL_V // kw)),
                  pl.BlockSpec((SWA_HEADS, blk, 2 * blk), lambda b, j: (0, 0, 0)),
                  pl.BlockSpec(memory_space=pl.ANY)],
        out_specs=pl.BlockSpec((None, blk, qw), lambda b, j: (1, b * nj + j, 0)),
        out_shape=jax.ShapeDtypeStruct(ybuf.shape, ybuf.dtype),
        input_output_aliases={7: 0},
        compiler_params=_cp(("parallel", "arbitrary"), 32),
        name="swa_prompt",
    )(sinks, z, z, z, z, z, bias, ybuf)


def _swa_sample_kernel(q_ref, kc_ref, vc_ref, kn_ref, vn_ref, bias_ref, sink_ref, o_ref, wk_ref, wv_ref):
    scale = SWA_HEAD_DIM ** -0.5
    nbat, win, ng, hd = kc_ref.shape
    new_rows = kn_ref.shape[1]
    for bi in range(nbat):
        kk = jnp.concatenate([kc_ref[bi].reshape(win * ng, hd), kn_ref[bi]], axis=0)
        vv = jnp.concatenate([vc_ref[bi].reshape(win * ng, hd), vn_ref[bi]], axis=0)
        wk_ref[bi] = kk[new_rows:, :].reshape(win, ng, hd)
        wv_ref[bi] = vv[new_rows:, :].reshape(win, ng, hd)
        s = lax.dot_general(q_ref[bi].astype(BF16), kk.astype(BF16), (((1,), (1,)), ((), ())),
                            preferred_element_type=F32) * scale + bias_ref[...]
        o_ref[bi] = _softmax_sink_pv(s, sink_ref[...], vv.astype(BF16))


def swa_sample(q, k_new, v_new, cache_k, cache_v, layer, bias, sink_col, bb):
    nb, rows, hd = q.shape
    new_rows = k_new.shape[1]
    _, _, win, ng, _ = cache_k.shape
    nk = win * ng + new_rows
    cache_spec = pl.BlockSpec((None, bb, win, ng, hd), lambda b: (layer, b, 0, 0, 0))
    new_spec = pl.BlockSpec((bb, new_rows, hd), lambda b: (b, 0, 0))
    win_spec = pl.BlockSpec((bb, win, ng, hd), lambda b: (b, 0, 0, 0))
    qo_spec = pl.BlockSpec((bb, rows, hd), lambda b: (b, 0, 0))
    win_shape = jax.ShapeDtypeStruct((nb, win, ng, hd), F32)
    return pl.pallas_call(
        _swa_sample_kernel,
        grid=(nb // bb,),
        in_specs=[qo_spec, cache_spec, cache_spec, new_spec, new_spec,
                  pl.BlockSpec((rows, nk), lambda b: (0, 0)), pl.BlockSpec((rows, 1), lambda b: (0, 0))],
        out_specs=[qo_spec, win_spec, win_spec],
        out_shape=[jax.ShapeDtypeStruct(q.shape, F32), win_shape, win_shape],
        compiler_params=_cp(("parallel",), 32),
        name="swa_sample",
    )(q, cache_k, cache_v, k_new, v_new, bias, sink_col)


def _attend(q, mk, mv, allowed=None):
    s = lax.dot_general(q.astype(BF16), mk.astype(BF16), (((1,), (1,)), ((), ())),
                        preferred_element_type=F32) * (MEM_HEAD_DIM ** -0.5)
    if allowed is not None:
        s = jnp.where(allowed, s, NEG_INF)
    m = jnp.max(s, axis=-1, keepdims=True)
    p = jnp.exp(s - m)
    den = jnp.sum(p, axis=-1, keepdims=True)
    return jnp.dot(p.astype(BF16), mv.astype(BF16), preferred_element_type=F32) / den


def _mem_attn_kernel(q0_ref, q1_ref, q2_ref, q3_ref, mk_ref, mv_ref, o_ref):
    for h, q_ref in enumerate((q0_ref, q1_ref, q2_ref, q3_ref)):
        hs = slice(h * MEM_HEAD_DIM, (h + 1) * MEM_HEAD_DIM)
        o_ref[:, hs] = _attend(q_ref[...], mk_ref[:, hs], mv_ref[:, hs]).astype(o_ref.dtype)


def _mem_attn_sample_kernel(q_ref, mk_ref, mv_ref, o_ref):
    nbat, rows, hd = q_ref.shape
    nkeys = mk_ref.shape[1] * MEM_HEADS
    same_head = (lax.broadcasted_iota(jnp.int32, (rows, nkeys), 0) % MEM_HEADS
                 == lax.broadcasted_iota(jnp.int32, (rows, nkeys), 1) % MEM_HEADS)
    for bi in range(nbat):
        mk = mk_ref[bi].reshape(nkeys, hd)
        mv = mv_ref[bi].reshape(nkeys, hd)
        o_ref[bi] = _attend(q_ref[bi], mk, mv, same_head).astype(o_ref.dtype)


def mem_attn_prompt(z, nb, seq, kv, tq, ybuf):
    nt = seq // tq
    mlen = kv.shape[1]
    hd = MEM_HEAD_DIM
    q_specs = [pl.BlockSpec((tq, hd), functools.partial(lambda b, t, h: (b * nt + t, COL_XQ // hd + h), h=h))
               for h in range(MEM_HEADS)]
    return pl.pallas_call(
        _skip_ref(_mem_attn_kernel, 6),
        grid=(nb, nt),
        in_specs=q_specs + [pl.BlockSpec((None, mlen, BRANCH_DIM), lambda b, t: (b, 0, 0)),
                            pl.BlockSpec((None, mlen, BRANCH_DIM), lambda b, t: (b, 0, 1)),
                            pl.BlockSpec(memory_space=pl.ANY)],
        out_specs=pl.BlockSpec((None, tq, BRANCH_DIM), lambda b, t: (3, b * nt + t, 0)),
        out_shape=jax.ShapeDtypeStruct(ybuf.shape, ybuf.dtype),
        input_output_aliases={6: 0},
        compiler_params=_cp(("parallel", "arbitrary"), 32),
        name="mem_attn_prompt",
    )(z, z, z, z, kv, kv, ybuf)


def mem_attn_sample(q, cache_k, cache_v, layer, bb):
    nb, rows, hd = q.shape
    mlen = cache_k.shape[2]
    qo_spec = pl.BlockSpec((bb, rows, hd), lambda b: (b, 0, 0))
    cache_spec = pl.BlockSpec((None, bb, mlen, MEM_HEADS, hd), lambda b: (layer, b, 0, 0, 0))
    return pl.pallas_call(
        _mem_attn_sample_kernel,
        grid=(nb // bb,),
        in_specs=[qo_spec, cache_spec, cache_spec],
        out_specs=qo_spec,
        out_shape=jax.ShapeDtypeStruct(q.shape, BF16),
        compiler_params=_cp(("parallel",), 40),
        name="mem_attn_sample",
    )(q, cache_k, cache_v)


def _ssm_param_kernel(are_ref, aim_ref, ldt_ref, bre_ref, bim_ref, abr_ref, abi_ref, bbr_ref, bbi_ref):
    dt = jnp.exp(ldt_ref[...])
    ar, ai = are_ref[...], aim_ref[...]
    mag = jnp.exp(dt * ar)
    abr, abi = mag * jnp.cos(dt * ai), mag * jnp.sin(dt * ai)
    den = ar * ar + ai * ai
    nr, ni = abr - 1.0, abi
    fre, fim = (nr * ar + ni * ai) / den, (ni * ar - nr * ai) / den
    abr_ref[...] = abr
    abi_ref[...] = abi
    for c in range(SSM_GROUP):
        br, bi = bre_ref[c], bim_ref[c]
        bbr_ref[c] = fre * br - fim * bi
        bbi_ref[c] = fre * bi + fim * br


def ssm_params(a_re, a_im, log_dt, b_re_t, b_im_t):
    d, g, n = a_re.shape
    c = b_re_t.shape[1]
    gn = pl.BlockSpec((None, g, n), lambda l: (l, 0, 0))
    cgn = pl.BlockSpec((None, c, g, n), lambda l: (l, 0, 0, 0))
    return pl.pallas_call(
        _ssm_param_kernel,
        grid=(d,),
        in_specs=[gn, gn, pl.BlockSpec((None, g, 1), lambda l: (l, 0, 0)), cgn, cgn],
        out_specs=[gn, gn, cgn, cgn],
        out_shape=[jax.ShapeDtypeStruct((d, g, n), F32)] * 2 + [jax.ShapeDtypeStruct((d, c, g, n), F32)] * 2,
        compiler_params=_cp(("arbitrary",), 16),
        name="ssm_params",
    )(a_re, a_im, log_dt.reshape(d, g, 1), b_re_t, b_im_t)


def _cmul_add(ar, ai, hr, hi, br, bi):
    return ar * hr - ai * hi + br, ar * hi + ai * hr + bi


def _ssm_out(u, sre, sim, cre_ref, cim_ref, d_ref):
    y = (jnp.dot(sre[...].astype(BF16), cre_ref[...], preferred_element_type=F32)
         - jnp.dot(sim[...].astype(BF16), cim_ref[...], preferred_element_type=F32)
         + d_ref[...] * u)
    return jax.nn.gelu(y)


def _ssm_prompt_kernel(u_ref, bre_ref, bim_ref, cre_ref, cim_ref, ar_ref, ai_ref, d_ref,
                       y_ref, hr_ref, hi_ref, sre, sim, *, clen):
    np_ = SCAN_LANES
    w = sre.shape[1]
    u = u_ref[...]
    ub = u.astype(BF16)
    sre[...] = jnp.dot(ub, bre_ref[...], preferred_element_type=F32)
    sim[...] = jnp.dot(ub, bim_ref[...], preferred_element_type=F32)
    ar1, ai1 = ar_ref[...], ai_ref[...]
    ar = jnp.broadcast_to(ar1, (np_, w))
    ai = jnp.broadcast_to(ai1, (np_, w))

    def rows(t):
        return pl.ds(pl.multiple_of(t * np_, np_), np_)

    def local_step(t, carry):
        return _cmul_add(ar, ai, carry[0], carry[1], sre[rows(t), :], sim[rows(t), :])

    zero = jnp.zeros((np_, w), F32)
    fr, fi = lax.fori_loop(0, clen, local_step, (zero, zero), unroll=SCAN_UNROLL)

    pr, pi = ar1, ai1
    for _ in range(int(math.log2(clen))):
        pr, pi = pr * pr - pi * pi, 2.0 * pr * pi
    row = lax.broadcasted_iota(jnp.int32, (np_, w), 0)
    cr = jnp.zeros((1, w), F32)
    ci = jnp.zeros((1, w), F32)
    hr0, hi0 = zero, zero
    for p in range(1, np_):
        cr, ci = _cmul_add(pr, pi, cr, ci, fr[p - 1:p], fi[p - 1:p])
        hr0 = jnp.where(row == p, cr, hr0)
        hi0 = jnp.where(row == p, ci, hi0)
    fin_r, fin_i = _cmul_add(pr, pi, cr, ci, fr[np_ - 1:np_], fi[np_ - 1:np_])
    hr_ref[...] = fin_r
    hi_ref[...] = fin_i

    def full_step(t, carry):
        nr, ni = _cmul_add(ar, ai, carry[0], carry[1], sre[rows(t), :], sim[rows(t), :])
        sre[rows(t), :] = nr
        sim[rows(t), :] = ni
        return nr, ni

    lax.fori_loop(0, clen, full_step, (hr0, hi0), unroll=SCAN_UNROLL)
    y_ref[...] = _ssm_out(u, sre, sim, cre_ref, cim_ref, d_ref)


def _ssm_specs(kdim):
    kmap = lambda *idx: (idx[kdim], 0, 0)
    return [pl.BlockSpec((None, SSM_UW, SSM_CW), kmap), pl.BlockSpec((None, SSM_UW, SSM_CW), kmap),
            pl.BlockSpec((None, SSM_CW, SSM_UW), kmap), pl.BlockSpec((None, SSM_CW, SSM_UW), kmap),
            pl.BlockSpec((None, 1, SSM_CW), kmap), pl.BlockSpec((None, 1, SSM_CW), kmap),
            pl.BlockSpec((None, 1, SSM_UW), kmap)]


def ssm_prompt(u_perm, mats):
    nb, seq, _ = u_perm.shape
    clen = seq // SCAN_LANES
    nstate = SSM_GROUPS * SSM_STATE
    st = pl.BlockSpec((None, 1, SSM_CW), lambda b, k: (b, 0, k))
    return pl.pallas_call(
        functools.partial(_ssm_prompt_kernel, clen=clen),
        grid=(nb, SSM_CHUNKS),
        in_specs=[pl.BlockSpec((None, seq, SSM_UW), lambda b, k: (b, 0, k))] + _ssm_specs(1),
        out_specs=[pl.BlockSpec((None, seq, SSM_UW), lambda b, k: (b, 0, k)), st, st],
        out_shape=[jax.ShapeDtypeStruct((nb, seq, BRANCH_DIM), F32),
                   jax.ShapeDtypeStruct((nb, 1, nstate), F32), jax.ShapeDtypeStruct((nb, 1, nstate), F32)],
        scratch_shapes=[pltpu.VMEM((seq, SSM_CW), F32), pltpu.VMEM((seq, SSM_CW), F32)],
        compiler_params=_cp(("parallel", "parallel"), 48),
        name="ssm_prompt",
    )(u_perm, *mats)


def _ssm_sample_kernel(u_ref, bre_ref, bim_ref, cre_ref, cim_ref, ar_ref, ai_ref, d_ref, h0r_ref, h0i_ref,
                       y_ref, hr_ref, hi_ref, sre, sim, *, steps, nb):
    w = sre.shape[1]
    u = u_ref[...]
    ub = u.astype(BF16)
    bur = jnp.dot(ub, bre_ref[...], preferred_element_type=F32)
    bui = jnp.dot(ub, bim_ref[...], preferred_element_type=F32)
    ar = jnp.broadcast_to(ar_ref[...], (nb, w))
    ai = jnp.broadcast_to(ai_ref[...], (nb, w))
    hr, hi = h0r_ref[...], h0i_ref[...]
    for t in range(steps):
        rs = slice(t * nb, (t + 1) * nb)
        hr, hi = _cmul_add(ar, ai, hr, hi, bur[rs], bui[rs])
        sre[rs, :] = hr
        sim[rs, :] = hi
    hr_ref[...] = hr
    hi_ref[...] = hi
    y_ref[...] = _ssm_out(u, sre, sim, cre_ref, cim_ref, d_ref)


def ssm_sample(z, row0, steps, nb, mats, h0_re, h0_im, layer):
    rows = steps * nb
    nstate = SSM_GROUPS * SSM_STATE
    st_in = pl.BlockSpec((None, nb, SSM_CW), lambda k: (layer, 0, k))
    st_out = pl.BlockSpec((nb, SSM_CW), lambda k: (0, k))
    return pl.pallas_call(
        functools.partial(_ssm_sample_kernel, steps=steps, nb=nb),
        grid=(SSM_CHUNKS,),
        in_specs=[pl.BlockSpec((rows, SSM_UW), lambda k: (row0 // rows, COL_U // SSM_UW + k))]
                 + _ssm_specs(0) + [st_in, st_in],
        out_specs=[pl.BlockSpec((rows, SSM_UW), lambda k: (0, k)), st_out, st_out],
        out_shape=[jax.ShapeDtypeStruct((rows, BRANCH_DIM), F32),
                   jax.ShapeDtypeStruct((nb, nstate), F32), jax.ShapeDtypeStruct((nb, nstate), F32)],
        scratch_shapes=[pltpu.VMEM((rows, SSM_CW), F32), pltpu.VMEM((rows, SSM_CW), F32)],
        compiler_params=_cp(("parallel",), 32),
        name="ssm_sample",
    )(z, *mats, h0_re, h0_im)


def _block_diag(x, pattern):
    eye = jnp.eye(SSM_GROUPS // SSM_CHUNKS, dtype=x.dtype)
    return jnp.einsum(pattern, x, eye)


def kernel(x_prompt, x_sample, cache_conv, cache_win_k, cache_win_v, state_ssm_re, state_ssm_im, cache_mem_k, cache_mem_v, mem_prompt, t5_bias, norm_mix_pre, norm_mix_post, norm_ffn_pre, norm_ffn_post, norm_mem, w_in, conv_w, conv_b, conv_ln_g, conv_ln_b, attn_sinks, ssm_a_re, ssm_a_im, ssm_log_dt, ssm_b_re, ssm_b_im, ssm_c_re, ssm_c_im, ssm_d, ssm_w_glu, w_mem_kv, w_branch, w_out, w_ffn_in, w_ffn_out):
    bp, seq, d = x_prompt.shape
    bs, steps, _ = x_sample.shape
    rows_p = bp * seq
    rows_s = bs * steps
    rows = rows_p + rows_s
    mlen = mem_prompt.shape[1]
    hist = CONV_WIDTH - 1
    kw = SWA_KV_HEADS * SWA_HEAD_DIM
    gpc = SSM_GROUPS // SSM_CHUNKS
    clen = seq // SCAN_LANES

    tm = rows // 4
    tm_merge = rows // 8
    tm_ffn_out = rows // 8
    tr = 320

    qi = np.arange(WINDOW)
    ki = np.arange(2 * WINDOW) - WINDOW
    dist_p = qi[:, None] - ki[None, :]
    bias_p = band_bias(t5_bias, dist_p, (dist_p >= 0) & (dist_p < WINDOW))
    nk_s = WINDOW + steps
    dist_s = (WINDOW + np.arange(SUBLANES))[:, None] - np.arange(nk_s)[None, :]
    bias_s = band_bias(t5_bias, dist_s, (dist_s >= 0) & (dist_s < WINDOW))[:, :steps]
    bias_s = bias_s.reshape(SWA_KV_HEADS, SWA_REP, steps, nk_s).transpose(0, 2, 1, 3)
    same_group = jnp.eye(SWA_KV_HEADS, dtype=bool)[:, None, None, None, :]
    bias_s = jnp.where(same_group, bias_s[..., None], NEG_INF)
    bias_s = bias_s.reshape(SWA_HEADS * steps, nk_s * SWA_KV_HEADS)

    abar_re, abar_im, bbar_re, bbar_im = ssm_params(
        ssm_a_re, ssm_a_im, ssm_log_dt, ssm_b_re.transpose(0, 3, 1, 2), ssm_b_im.transpose(0, 3, 1, 2))

    def in_mat(x):
        x = x.reshape(DEPTH, SSM_GROUP, SSM_CHUNKS, gpc, SSM_STATE)
        return _block_diag(x, 'lckgn,gh->lkgchn').reshape(DEPTH, SSM_CHUNKS, SSM_UW, SSM_CW).astype(BF16)

    def out_mat(x):
        x = x.reshape(DEPTH, SSM_CHUNKS, gpc, SSM_GROUP, SSM_STATE)
        return _block_diag(x, 'lkgcn,gh->lkgnhc').reshape(DEPTH, SSM_CHUNKS, SSM_CW, SSM_UW).astype(BF16)

    bmat_re, bmat_im = in_mat(bbar_re), in_mat(bbar_im)
    cmat_re, cmat_im = out_mat(ssm_c_re), out_mat(ssm_c_im)
    abar_re = abar_re.reshape(DEPTH, SSM_CHUNKS, 1, SSM_CW)
    abar_im = abar_im.reshape(DEPTH, SSM_CHUNKS, 1, SSM_CW)
    dvec = ssm_d.reshape(DEPTH, SSM_CHUNKS, 1, SSM_UW)

    cache_conv_t = cache_conv.transpose(0, 2, 1, 3)
    h0_re = state_ssm_re.reshape(DEPTH, bs, -1)
    h0_im = state_ssm_im.reshape(DEPTH, bs, -1)
    mem_rows = mem_prompt.reshape(bp * mlen, d)

    x = jnp.concatenate([x_prompt.reshape(rows_p, d), x_sample.transpose(1, 0, 2).reshape(rows_s, d)], axis=0)
    xn = rmsnorm_bf16(x, norm_mix_pre[0], tr)

    outs = [[] for _ in range(12)]
    for l in range(DEPTH):
        z = matmul(xn, w_in, F32, tm, 512, layer=l, n=COL_GATE, single_a=True)
        z_s = lax.slice(z, (rows_p, 0), (rows, COL_GATE))
        ybuf = jnp.zeros((4, rows, BRANCH_DIM), BF16)

        ybuf, conv_p = conv_prompt(z, bp, seq, conv_w[l], conv_b[l], conv_ln_g[l], conv_ln_b[l], 256, ybuf)
        ybuf, conv_s_t = conv_sample(z, rows_p, steps, bs, cache_conv_t[l], conv_w[l], conv_b[l],
                                     conv_ln_g[l], conv_ln_b[l], ybuf)

        ybuf = swa_prompt(z, bp, seq, attn_sinks[l], bias_p, ybuf)
        q_s = z_s[:, COL_Q:COL_K].reshape(steps, bs, SWA_KV_HEADS, SWA_REP, SWA_HEAD_DIM)
        q_s = q_s.transpose(1, 2, 0, 3, 4).reshape(bs, SWA_HEADS * steps, SWA_HEAD_DIM)
        k_s = z_s[:, COL_K:COL_V].reshape(steps, bs, SWA_KV_HEADS, SWA_HEAD_DIM).transpose(1, 0, 2, 3)
        v_s = z_s[:, COL_V:COL_U].reshape(steps, bs, SWA_KV_HEADS, SWA_HEAD_DIM).transpose(1, 0, 2, 3)
        k_s = k_s.reshape(bs, steps * SWA_KV_HEADS, SWA_HEAD_DIM)
        v_s = v_s.reshape(bs, steps * SWA_KV_HEADS, SWA_HEAD_DIM)
        sink_col = jnp.tile(attn_sinks[l].reshape(SWA_KV_HEADS, 1, SWA_REP), (1, steps, 1))
        sink_col = sink_col.reshape(SWA_HEADS * steps, 1)
        ob_s, wk_s, wv_s = swa_sample(q_s, k_s, v_s, cache_win_k, cache_win_v, l, bias_s, sink_col, 4)
        yb_s = ob_s.reshape(bs, SWA_KV_HEADS, steps, SWA_REP, SWA_HEAD_DIM).transpose(2, 0, 1, 3, 4)
        yb_s = yb_s.reshape(rows_s, BRANCH_DIM).astype(BF16)

        mats = (bmat_re[l], bmat_im[l], cmat_re[l], cmat_im[l], abar_re[l], abar_im[l], dvec[l])
        u_perm = lax.slice(z, (0, COL_U), (rows_p, COL_XQ))
        u_perm = u_perm.reshape(bp, SCAN_LANES, clen, BRANCH_DIM).transpose(0, 2, 1, 3)
        yg_perm, hr_p, hi_p = ssm_prompt(u_perm.reshape(bp, seq, BRANCH_DIM), mats)
        yc_perm = glu(yg_perm.reshape(rows_p, BRANCH_DIM), ssm_w_glu, l, 1024)
        yc_p = yc_perm.reshape(bp, clen, SCAN_LANES, BRANCH_DIM).transpose(0, 2, 1, 3).reshape(rows_p, BRANCH_DIM)
        yg_s, hr_s, hi_s = ssm_sample(z, rows_p, steps, bs, mats, h0_re, h0_im, l)
        ybuf = glu(yg_s, ssm_w_glu, l, rows_s, ybuf, 2, rows_p)

        kv = matmul(rmsnorm_bf16(mem_rows, norm_mem[l], 256), w_mem_kv, F32, bp * mlen, 512, layer=l)
        kv = kv.reshape(bp, mlen, 2 * BRANCH_DIM)
        ybuf = mem_attn_prompt(z, bp, seq, kv, 512, ybuf)
        xq_s = z_s[:, COL_XQ:COL_GATE].reshape(steps, bs, MEM_HEADS, MEM_HEAD_DIM).transpose(1, 0, 2, 3)
        yx_s = mem_attn_sample(xq_s.reshape(bs, steps * MEM_HEADS, MEM_HEAD_DIM), cache_mem_k, cache_mem_v, l, 2)
        yx_s = yx_s.reshape(bs, steps, BRANCH_DIM).transpose(1, 0, 2).reshape(rows_s, BRANCH_DIM)

        ybuf = lax.dynamic_update_slice(ybuf, yb_s[None], (1, rows_p, 0))
        ybuf = lax.dynamic_update_slice(ybuf, yc_p[None], (2, 0, 0))
        ybuf = lax.dynamic_update_slice(ybuf, yx_s[None], (3, rows_p, 0))
        merged = merge_branches(xn, ybuf, w_in, w_branch, l, tm_merge, 512)
        mix = matmul(merged, w_out, BF16, tm, 512, layer=l, single_a=True)
        x, hn = resid_norm(x, mix, norm_mix_post[l], norm_ffn_pre[l], tr)

        hid = ffn_in(hn, w_ffn_in, l, tm, 256)
        f = matmul(hid, w_ffn_out, BF16, tm_ffn_out, 256, layer=l, single_a=True)
        x, xn = resid_norm(x, f, norm_ffn_post[l], norm_mix_pre[l + 1] if l + 1 < DEPTH else None, tr)

        kv_win = jnp.stack([lax.slice(z, ((b + 1) * seq - WINDOW, COL_K), ((b + 1) * seq, COL_U))
                            for b in range(bp)])
        new = (conv_p, conv_s_t.transpose(1, 0, 2),
               kv_win[..., :kw].reshape(bp, WINDOW, SWA_KV_HEADS, SWA_HEAD_DIM),
               kv_win[..., kw:].reshape(bp, WINDOW, SWA_KV_HEADS, SWA_HEAD_DIM),
               wk_s, wv_s,
               hr_p.reshape(bp, SSM_GROUPS, SSM_STATE), hi_p.reshape(bp, SSM_GROUPS, SSM_STATE),
               hr_s.reshape(bs, SSM_GROUPS, SSM_STATE), hi_s.reshape(bs, SSM_GROUPS, SSM_STATE),
               kv[..., :BRANCH_DIM].reshape(bp, mlen, MEM_HEADS, MEM_HEAD_DIM),
               kv[..., BRANCH_DIM:].reshape(bp, mlen, MEM_HEADS, MEM_HEAD_DIM))
        for acc, val in zip(outs, new):
            acc.append(val)

    y_prompt = x[:rows_p].reshape(bp, seq, d)
    y_sample = x[rows_p:].reshape(steps, bs, d).transpose(1, 0, 2)
    return (y_prompt, y_sample) + tuple(jnp.stack(o) for o in outs)
```

```python
import functools
import math

import jax
import jax.numpy as jnp
import numpy as np
from jax import lax
from jax.experimental import pallas as pl
from jax.experimental.pallas import tpu as pltpu

F32 = jnp.float32
BF16 = jnp.bfloat16

D_MODEL = 4096
DEPTH = 4
BRANCH_DIM = 1024
CONV_WIDTH = 31
SWA_HEAD_DIM = 64
SWA_HEADS = 16
SWA_KV_HEADS = 4
SWA_REP = 4
WINDOW = 128
N_BUCKETS = 32
SSM_GROUP = 16
SSM_GROUPS = 64
SSM_STATE = 64
MEM_HEADS = 4
MEM_HEAD_DIM = 256
D_FF = 11008
COL_Q = 2048
COL_K = 3072
COL_V = 3328
COL_U = 3584
COL_XQ = 4608
COL_GATE = 5632
N_IN = COL_GATE + 4 * D_MODEL
NEG_INF = -1e30

SSM_CHUNKS = 4
SSM_CW = SSM_GROUPS // SSM_CHUNKS * SSM_STATE
SSM_UW = SSM_GROUPS // SSM_CHUNKS * SSM_GROUP
SCAN_LANES = 8
SCAN_UNROLL = 8


def _cp(sem, vmem_mb):
    return pltpu.CompilerParams(dimension_semantics=sem, vmem_limit_bytes=vmem_mb << 20)


def _rmsnorm_kernel(x_ref, g_ref, o_ref):
    x = x_ref[...]
    y = x * lax.rsqrt(jnp.mean(x * x, axis=-1, keepdims=True) + 1e-6)
    o_ref[...] = (y * g_ref[...]).astype(o_ref.dtype)


def rmsnorm_bf16(x, g, tr):
    m, d = x.shape
    return pl.pallas_call(
        _rmsnorm_kernel,
        grid=(m // tr,),
        in_specs=[pl.BlockSpec((tr, d), lambda i: (i, 0)), pl.BlockSpec((1, d), lambda i: (0, 0))],
        out_specs=pl.BlockSpec((tr, d), lambda i: (i, 0)),
        out_shape=jax.ShapeDtypeStruct((m, d), BF16),
        compiler_params=_cp(("parallel",), 40),
        name="rmsnorm",
    )(x, g.reshape(1, d))


def _resid_norm_kernel(x_ref, y_ref, gp_ref, gn_ref, xo_ref, hn_ref):
    y = y_ref[...].astype(F32)
    yn = y * lax.rsqrt(jnp.mean(y * y, axis=-1, keepdims=True) + 1e-6) * gp_ref[...]
    x = x_ref[...] + yn
    xo_ref[...] = x
    h = x * lax.rsqrt(jnp.mean(x * x, axis=-1, keepdims=True) + 1e-6)
    hn_ref[...] = (h * gn_ref[...]).astype(hn_ref.dtype)


def _resid_kernel(x_ref, y_ref, gp_ref, xo_ref):
    y = y_ref[...].astype(F32)
    xo_ref[...] = x_ref[...] + y * lax.rsqrt(jnp.mean(y * y, axis=-1, keepdims=True) + 1e-6) * gp_ref[...]


def resid_norm(x, y, g_post, g_next, tr):
    m, d = x.shape
    row = pl.BlockSpec((tr, d), lambda i: (i, 0))
    vec = pl.BlockSpec((1, d), lambda i: (0, 0))
    if g_next is None:
        return pl.pallas_call(
            _resid_kernel,
            grid=(m // tr,),
            in_specs=[row, row, vec],
            out_specs=row,
            out_shape=jax.ShapeDtypeStruct((m, d), F32),
            compiler_params=_cp(("parallel",), 48),
            name="resid",
        )(x, y, g_post.reshape(1, d)), None
    return pl.pallas_call(
        _resid_norm_kernel,
        grid=(m // tr,),
        in_specs=[row, row, vec, vec],
        out_specs=[row, row],
        out_shape=[jax.ShapeDtypeStruct((m, d), F32), jax.ShapeDtypeStruct((m, d), BF16)],
        compiler_params=_cp(("parallel",), 48),
        name="resid_norm",
    )(x, y, g_post.reshape(1, d), g_next.reshape(1, d))


def _mm_kernel(a_ref, w_ref, o_ref):
    o_ref[...] = jnp.dot(a_ref[...], w_ref[...].astype(BF16), preferred_element_type=F32).astype(o_ref.dtype)


def _row_resident_spec(tm, k, single):
    if single:
        return pl.BlockSpec((tm, k), lambda i, j: (i, 0), pipeline_mode=pl.Buffered(1))
    return pl.BlockSpec((tm, k), lambda i, j: (i, 0))


def matmul(a, w, out_dtype, tm, tn, layer=None, n=None, single_a=False, vmem_mb=56):
    m, k = a.shape
    n = w.shape[-1] if n is None else n
    if layer is None:
        w_spec = pl.BlockSpec((k, tn), lambda i, j: (0, j))
    else:
        w_spec = pl.BlockSpec((None, k, tn), lambda i, j: (layer, 0, j))
    return pl.pallas_call(
        _mm_kernel,
        grid=(m // tm, n // tn),
        in_specs=[_row_resident_spec(tm, k, single_a), w_spec],
        out_specs=pl.BlockSpec((tm, tn), lambda i, j: (i, j)),
        out_shape=jax.ShapeDtypeStruct((m, n), out_dtype),
        compiler_params=_cp(("parallel", "parallel"), vmem_mb),
        name="matmul",
    )(a, w)


def _ffn_in_kernel(a_ref, wg_ref, wu_ref, o_ref):
    a = a_ref[...]
    g = jnp.dot(a, wg_ref[...].astype(BF16), preferred_element_type=F32)
    u = jnp.dot(a, wu_ref[...].astype(BF16), preferred_element_type=F32)
    o_ref[...] = (jax.nn.silu(g) * u).astype(o_ref.dtype)


def ffn_in(a, w, layer, tm, tn):
    m, k = a.shape
    nt = D_FF // tn
    return pl.pallas_call(
        _ffn_in_kernel,
        grid=(m // tm, nt),
        in_specs=[_row_resident_spec(tm, k, False),
                  pl.BlockSpec((None, k, tn), lambda i, j: (layer, 0, j)),
                  pl.BlockSpec((None, k, tn), lambda i, j: (layer, 0, j + nt))],
        out_specs=pl.BlockSpec((tm, tn), lambda i, j: (i, j)),
        out_shape=jax.ShapeDtypeStruct((m, D_FF), BF16),
        compiler_params=_cp(("parallel", "parallel"), 56),
        name="ffn_in",
    )(a, w, w)


def _merge_kernel(xn_ref, y_ref, wg_ref, wb_ref, o_ref, acc):
    b = pl.program_id(2)
    @pl.when(b == 0)
    def _():
        acc[...] = jnp.zeros(acc.shape, F32)

    gate = jnp.dot(xn_ref[...], wg_ref[...].astype(BF16), preferred_element_type=F32)
    term = jax.nn.sigmoid(gate) * jnp.dot(y_ref[...], wb_ref[...].astype(BF16), preferred_element_type=F32)
    total = acc[...] + term
    acc[...] = total
    o_ref[...] = total.astype(o_ref.dtype)


def merge_branches(xn, y_all, w_in, w_branch, layer, tm, tn):
    nb, m, kb = y_all.shape
    k = xn.shape[1]
    gate0 = COL_GATE // tn
    per = D_MODEL // tn
    return pl.pallas_call(
        _merge_kernel,
        grid=(m // tm, per, nb),
        in_specs=[pl.BlockSpec((tm, k), lambda i, j, b: (i, 0)),
                  pl.BlockSpec((None, tm, kb), lambda i, j, b: (b, i, 0)),
                  pl.BlockSpec((None, k, tn), lambda i, j, b: (layer, 0, gate0 + b * per + j)),
                  pl.BlockSpec((None, None, kb, tn), lambda i, j, b: (layer, b, 0, j))],
        out_specs=pl.BlockSpec((tm, tn), lambda i, j, b: (i, j)),
        out_shape=jax.ShapeDtypeStruct((m, D_MODEL), BF16),
        scratch_shapes=[pltpu.VMEM((tm, tn), F32)],
        compiler_params=_cp(("parallel", "parallel", "arbitrary"), 56),
        name="merge",
    )(xn, y_all, w_in, w_branch)


def _skip_ref(body, pos):
    def wrapped(*refs):
        return body(*refs[:pos], *refs[pos + 1:])
    return wrapped


def _glu_kernel(y_ref, w_ref, o_ref):
    y = y_ref[...]
    s = jnp.dot(y.astype(BF16), w_ref[...].astype(BF16), preferred_element_type=F32)
    o_ref[...] = (y * jax.nn.sigmoid(s)).astype(o_ref.dtype)


def glu(y, w, layer, tr, ybuf=None, branch=0, row0=0):
    m, d = y.shape
    in_specs = [pl.BlockSpec((tr, d), lambda i: (i, 0)), pl.BlockSpec((None, d, d), lambda i: (layer, 0, 0))]
    if ybuf is None:
        return pl.pallas_call(
            _glu_kernel,
            grid=(m // tr,),
            in_specs=in_specs,
            out_specs=pl.BlockSpec((tr, d), lambda i: (i, 0)),
            out_shape=jax.ShapeDtypeStruct((m, d), BF16),
            compiler_params=_cp(("parallel",), 40),
            name="glu",
        )(y, w)
    return pl.pallas_call(
        _skip_ref(_glu_kernel, 2),
        grid=(m // tr,),
        in_specs=in_specs + [pl.BlockSpec(memory_space=pl.ANY)],
        out_specs=pl.BlockSpec((None, tr, d), lambda i: (branch, row0 // tr + i, 0)),
        out_shape=jax.ShapeDtypeStruct(ybuf.shape, ybuf.dtype),
        input_output_aliases={2: 0},
        compiler_params=_cp(("parallel",), 40),
        name="glu_into",
    )(y, w, ybuf)


def _ln_silu(y, g, b):
    mu = jnp.mean(y, axis=-1, keepdims=True)
    yc = y - mu
    yn = yc * lax.rsqrt(jnp.mean(yc * yc, axis=-1, keepdims=True) + 1e-5)
    return jax.nn.silu(yn * g + b)


CONV_HALO = 32
CONV_LANES = 128


SUBLANES = 8


def _conv_prompt_kernel(z_ref, w_ref, b_ref, g_ref, beta_ref, y_ref, cn_ref, xx, xs, acc, *, tt):
    c = CONV_DIM_
    off = CONV_HALO - (CONV_WIDTH - 1)
    span = tt + CONV_HALO - SUBLANES

    @pl.when(pl.program_id(1) == 0)
    def _():
        xx[0:CONV_HALO, :] = jnp.zeros((CONV_HALO, c), F32)

    xx[CONV_HALO:CONV_HALO + tt, :] = z_ref[:, :c] * jax.nn.sigmoid(z_ref[:, c:])
    for r in range(1, SUBLANES):
        xs[r - 1, 0:span, :] = xx[r:r + span, :]
    def lane_chunk(lc, carry):
        ls = pl.ds(pl.multiple_of(lc * CONV_LANES, CONV_LANES), CONV_LANES)
        a = None
        for w in range(CONV_WIDTH):
            q, r = divmod(off + w, SUBLANES)
            lo = SUBLANES * q
            src = xx[lo:lo + tt, ls] if r == 0 else xs[r - 1, lo:lo + tt, ls]
            term = src * w_ref[w:w + 1, ls]
            a = term if a is None else a + term
        acc[:, ls] = a
        return carry

    lax.fori_loop(0, c // CONV_LANES, lane_chunk, 0)
    y_ref[...] = _ln_silu(acc[...] + b_ref[...], g_ref[...], beta_ref[...]).astype(y_ref.dtype)
    cn_ref[...] = xx[tt + off:tt + CONV_HALO, :]
    xx[0:CONV_HALO, :] = xx[tt:tt + CONV_HALO, :]


CONV_DIM_ = BRANCH_DIM


def conv_prompt(z, nb, seq, w, b, g, beta, tt, ybuf):
    c = CONV_DIM_
    nt = seq // tt
    vec = pl.BlockSpec((1, c), lambda bi, t: (0, 0))
    return pl.pallas_call(
        _skip_ref(functools.partial(_conv_prompt_kernel, tt=tt), 5),
        grid=(nb, nt),
        in_specs=[pl.BlockSpec((tt, 2 * c), lambda bi, t: (bi * nt + t, 0)),
                  pl.BlockSpec((CONV_WIDTH, c), lambda bi, t: (0, 0)), vec, vec, vec,
                  pl.BlockSpec(memory_space=pl.ANY)],
        out_specs=[pl.BlockSpec((None, tt, c), lambda bi, t: (0, bi * nt + t, 0)),
                   pl.BlockSpec((None, CONV_WIDTH - 1, c), lambda bi, t: (bi, 0, 0))],
        out_shape=[jax.ShapeDtypeStruct(ybuf.shape, ybuf.dtype),
                   jax.ShapeDtypeStruct((nb, CONV_WIDTH - 1, c), F32)],
        input_output_aliases={5: 0},
        scratch_shapes=[pltpu.VMEM((tt + CONV_HALO, c), F32),
                        pltpu.VMEM((SUBLANES - 1, tt + CONV_HALO - SUBLANES, c), F32),
                        pltpu.VMEM((tt, c), F32)],
        compiler_params=_cp(("parallel", "arbitrary"), 40),
        name="conv_prompt",
    )(z, w, b.reshape(1, c), g.reshape(1, c), beta.reshape(1, c), ybuf)


def _conv_sample_kernel(z_ref, cache_ref, w_ref, b_ref, g_ref, beta_ref, y_ref, cn_ref, *, steps, nb):
    c = CONV_DIM_
    hist = CONV_WIDTH - 1
    a = z_ref[:, :c] * jax.nn.sigmoid(z_ref[:, c:])
    for t in range(steps):
        acc = None
        for w in range(CONV_WIDTH):
            idx = t + w
            src = cache_ref[idx] if idx < hist else a[(idx - hist) * nb:(idx - hist + 1) * nb]
            term = src * w_ref[w:w + 1, :]
            acc = term if acc is None else acc + term
        y_ref[t * nb:(t + 1) * nb, :] = _ln_silu(acc + b_ref[...], g_ref[...], beta_ref[...]).astype(y_ref.dtype)
    for r in range(hist - steps):
        cn_ref[r] = cache_ref[r + steps]
    for t in range(steps):
        cn_ref[hist - steps + t] = a[t * nb:(t + 1) * nb]


def conv_sample(z, row0, steps, nb, cache_t, w, b, g, beta, ybuf):
    c = CONV_DIM_
    rows = steps * nb
    hist = CONV_WIDTH - 1
    vec = pl.BlockSpec((1, c), lambda i: (0, 0))
    return pl.pallas_call(
        _skip_ref(functools.partial(_conv_sample_kernel, steps=steps, nb=nb), 6),
        grid=(1,),
        in_specs=[pl.BlockSpec((rows, 2 * c), lambda i: (row0 // rows, 0)),
                  pl.BlockSpec((hist, nb, c), lambda i: (0, 0, 0)),
                  pl.BlockSpec((CONV_WIDTH, c), lambda i: (0, 0)), vec, vec, vec,
                  pl.BlockSpec(memory_space=pl.ANY)],
        out_specs=[pl.BlockSpec((None, rows, c), lambda i: (0, row0 // rows, 0)),
                   pl.BlockSpec((hist, nb, c), lambda i: (0, 0, 0))],
        out_shape=[jax.ShapeDtypeStruct(ybuf.shape, ybuf.dtype), jax.ShapeDtypeStruct((hist, nb, c), F32)],
        input_output_aliases={6: 0},
        compiler_params=_cp(("arbitrary",), 40),
        name="conv_sample",
    )(z, cache_t, w, b.reshape(1, c), g.reshape(1, c), beta.reshape(1, c), ybuf)


def _t5_bucket_np(dist):
    n = np.maximum(dist, 0)
    max_exact = N_BUCKETS // 2
    nf = np.maximum(n, 1).astype(np.float32)
    large = max_exact + (np.log(nf / np.float32(max_exact)) / np.float32(math.log(WINDOW / max_exact))
                         * np.float32(N_BUCKETS - max_exact)).astype(np.int32)
    large = np.minimum(large, N_BUCKETS - 1)
    return np.where(n < max_exact, n, large)


def _bias_kernel(t5_ref, bucket_ref, o_ref):
    h = pl.program_id(0)
    bucket = bucket_ref[...]
    acc = jnp.full(bucket.shape, NEG_INF, F32)
    for b in range(N_BUCKETS):
        acc = jnp.where(bucket == b, t5_ref[b, h], acc)
    o_ref[...] = acc


def band_bias(t5_bias, dist, mask):
    bucket = np.where(mask, _t5_bucket_np(dist), -1).astype(np.int32)
    nq, nk = bucket.shape
    return pl.pallas_call(
        _bias_kernel,
        grid=(SWA_HEADS,),
        in_specs=[pl.BlockSpec(memory_space=pltpu.SMEM), pl.BlockSpec((nq, nk), lambda h: (0, 0))],
        out_specs=pl.BlockSpec((None, nq, nk), lambda h: (h, 0, 0)),
        out_shape=jax.ShapeDtypeStruct((SWA_HEADS, nq, nk), F32),
        compiler_params=_cp(("arbitrary",), 16),
        name="band_bias",
    )(t5_bias, jnp.asarray(bucket))


def _softmax_sink_pv(s, sink, v):
    m = jnp.maximum(jnp.max(s, axis=-1, keepdims=True), sink)
    p = jnp.exp(s - m)
    den = jnp.sum(p, axis=-1, keepdims=True) + jnp.exp(sink - m)
    return jnp.dot(p.astype(BF16), v, preferred_element_type=F32) / den


def _swa_prompt_kernel(sink_ref, q_ref, kp_ref, kc_ref, vp_ref, vc_ref, bias_ref, o_ref):
    blk = WINDOW
    scale = SWA_HEAD_DIM ** -0.5
    k = jnp.concatenate([kp_ref[...], kc_ref[...]], axis=0)
    v = jnp.concatenate([vp_ref[...], vc_ref[...]], axis=0)
    col = lax.broadcasted_iota(jnp.int32, (blk, 2 * blk), 1)
    no_prev = jnp.logical_and(pl.program_id(1) == 0, col < blk)
    q = q_ref[...]
    for hp in range(SWA_HEADS // 2):
        outs = []
        for h in (2 * hp, 2 * hp + 1):
            g = h // SWA_REP
            hs = slice(h * SWA_HEAD_DIM, (h + 1) * SWA_HEAD_DIM)
            gs = slice(g * SWA_HEAD_DIM, (g + 1) * SWA_HEAD_DIM)
            s = lax.dot_general(q[:, hs].astype(BF16), k[:, gs].astype(BF16), (((1,), (1,)), ((), ())),
                                preferred_element_type=F32) * scale
            s = jnp.where(no_prev, NEG_INF, s + bias_ref[h])
            outs.append(_softmax_sink_pv(s, sink_ref[h], v[:, gs].astype(BF16)))
        o_ref[:, 2 * hp * SWA_HEAD_DIM:(2 * hp + 2) * SWA_HEAD_DIM] = (
            jnp.concatenate(outs, axis=1).astype(o_ref.dtype))


def swa_prompt(z, nb, seq, sinks, bias, ybuf):
    blk = WINDOW
    nj = seq // blk
    qw = SWA_HEADS * SWA_HEAD_DIM
    kw = SWA_KV_HEADS * SWA_HEAD_DIM
    cur = lambda col: (lambda b, j: (b * nj + j, col))
    prev = lambda col: (lambda b, j: (b * nj + jnp.maximum(j - 1, 0), col))
    return pl.pallas_call(
        _skip_ref(_swa_prompt_kernel, 7),
        grid=(nb, nj),
        in_specs=[pl.BlockSpec(memory_space=pltpu.SMEM),
                  pl.BlockSpec((blk, qw), cur(COL_Q // qw)),
                  pl.BlockSpec((blk, kw), prev(COL_K // kw)), pl.BlockSpec((blk, kw), cur(COL_K // kw)),
                  pl.BlockSpec((blk, kw), prev(COL_V // kw)), pl.BlockSpec((blk, kw), cur(COL_V // kw)),
                  pl.BlockSpec((SWA_HEADS, blk, 2 * blk), lambda b, j: (0, 0, 0)),
                  pl.BlockSpec(memory_space=pl.ANY)],
        out_specs=pl.BlockSpec((None, blk, qw), lambda b, j: (1, b * nj + j, 0)),
        out_shape=jax.ShapeDtypeStruct(ybuf.shape, ybuf.dtype),
        input_output_aliases={7: 0},
        compiler_params=_cp(("parallel", "arbitrary"), 32),
        name="swa_prompt",
    )(sinks, z, z, z, z, z, bias, ybuf)


def _swa_sample_kernel(q_ref, kc_ref, vc_ref, kn_ref, vn_ref, bias_ref, sink_ref, o_ref, wk_ref, wv_ref):
    scale = SWA_HEAD_DIM ** -0.5
    nbat, win, ng, hd = kc_ref.shape
    new_rows = kn_ref.shape[1]
    for bi in range(nbat):
        kk = jnp.concatenate([kc_ref[bi].reshape(win * ng, hd), kn_ref[bi]], axis=0)
        vv = jnp.concatenate([vc_ref[bi].reshape(win * ng, hd), vn_ref[bi]], axis=0)
        wk_ref[bi] = kk[new_rows:, :].reshape(win, ng, hd)
        wv_ref[bi] = vv[new_rows:, :].reshape(win, ng, hd)
        s = lax.dot_general(q_ref[bi].astype(BF16), kk.astype(BF16), (((1,), (1,)), ((), ())),
                            preferred_element_type=F32) * scale + bias_ref[...]
        o_ref[bi] = _softmax_sink_pv(s, sink_ref[...], vv.astype(BF16))


def swa_sample(q, k_new, v_new, cache_k, cache_v, layer, bias, sink_col, bb):
    nb, rows, hd = q.shape
    new_rows = k_new.shape[1]
    _, _, win, ng, _ = cache_k.shape
    nk = win * ng + new_rows
    cache_spec = pl.BlockSpec((None, bb, win, ng, hd), lambda b: (layer, b, 0, 0, 0))
    new_spec = pl.BlockSpec((bb, new_rows, hd), lambda b: (b, 0, 0))
    win_spec = pl.BlockSpec((bb, win, ng, hd), lambda b: (b, 0, 0, 0))
    qo_spec = pl.BlockSpec((bb, rows, hd), lambda b: (b, 0, 0))
    win_shape = jax.ShapeDtypeStruct((nb, win, ng, hd), F32)
    return pl.pallas_call(
        _swa_sample_kernel,
        grid=(nb // bb,),
        in_specs=[qo_spec, cache_spec, cache_spec, new_spec, new_spec,
                  pl.BlockSpec((rows, nk), lambda b: (0, 0)), pl.BlockSpec((rows, 1), lambda b: (0, 0))],
        out_specs=[qo_spec, win_spec, win_spec],
        out_shape=[jax.ShapeDtypeStruct(q.shape, F32), win_shape, win_shape],
        compiler_params=_cp(("parallel",), 32),
        name="swa_sample",
    )(q, cache_k, cache_v, k_new, v_new, bias, sink_col)


def _attend(q, mk, mv, allowed=None):
    s = lax.dot_general(q.astype(BF16), mk.astype(BF16), (((1,), (1,)), ((), ())),
                        preferred_element_type=F32) * (MEM_HEAD_DIM ** -0.5)
    if allowed is not None:
        s = jnp.where(allowed, s, NEG_INF)
    m = jnp.max(s, axis=-1, keepdims=True)
    p = jnp.exp(s - m)
    den = jnp.sum(p, axis=-1, keepdims=True)
    return jnp.dot(p.astype(BF16), mv.astype(BF16), preferred_element_type=F32) / den


def _mem_attn_kernel(q0_ref, q1_ref, q2_ref, q3_ref, mk_ref, mv_ref, o_ref):
    for h, q_ref in enumerate((q0_ref, q1_ref, q2_ref, q3_ref)):
        hs = slice(h * MEM_HEAD_DIM, (h + 1) * MEM_HEAD_DIM)
        o_ref[:, hs] = _attend(q_ref[...], mk_ref[:, hs], mv_ref[:, hs]).astype(o_ref.dtype)


def _mem_attn_sample_kernel(q_ref, mk_ref, mv_ref, o_ref):
    nbat, rows, hd = q_ref.shape
    nkeys = mk_ref.shape[1] * MEM_HEADS
    same_head = (lax.broadcasted_iota(jnp.int32, (rows, nkeys), 0) % MEM_HEADS
                 == lax.broadcasted_iota(jnp.int32, (rows, nkeys), 1) % MEM_HEADS)
    for bi in range(nbat):
        mk = mk_ref[bi].reshape(nkeys, hd)
        mv = mv_ref[bi].reshape(nkeys, hd)
        o_ref[bi] = _attend(q_ref[bi], mk, mv, same_head).astype(o_ref.dtype)


def mem_attn_prompt(z, nb, seq, kv, tq, ybuf):
    nt = seq // tq
    mlen = kv.shape[1]
    hd = MEM_HEAD_DIM
    q_specs = [pl.BlockSpec((tq, hd), functools.partial(lambda b, t, h: (b * nt + t, COL_XQ // hd + h), h=h))
               for h in range(MEM_HEADS)]
    return pl.pallas_call(
        _skip_ref(_mem_attn_kernel, 6),
        grid=(nb, nt),
        in_specs=q_specs + [pl.BlockSpec((None, mlen, BRANCH_DIM), lambda b, t: (b, 0, 0)),
                            pl.BlockSpec((None, mlen, BRANCH_DIM), lambda b, t: (b, 0, 1)),
                            pl.BlockSpec(memory_space=pl.ANY)],
        out_specs=pl.BlockSpec((None, tq, BRANCH_DIM), lambda b, t: (3, b * nt + t, 0)),
        out_shape=jax.ShapeDtypeStruct(ybuf.shape, ybuf.dtype),
        input_output_aliases={6: 0},
        compiler_params=_cp(("parallel", "arbitrary"), 32),
        name="mem_attn_prompt",
    )(z, z, z, z, kv, kv, ybuf)


def mem_attn_sample(q, cache_k, cache_v, layer, bb):
    nb, rows, hd = q.shape
    mlen = cache_k.shape[2]
    qo_spec = pl.BlockSpec((bb, rows, hd), lambda b: (b, 0, 0))
    cache_spec = pl.BlockSpec((None, bb, mlen, MEM_HEADS, hd), lambda b: (layer, b, 0, 0, 0))
    return pl.pallas_call(
        _mem_attn_sample_kernel,
        grid=(nb // bb,),
        in_specs=[qo_spec, cache_spec, cache_spec],
        out_specs=qo_spec,
        out_shape=jax.ShapeDtypeStruct(q.shape, BF16),
        compiler_params=_cp(("parallel",), 40),
        name="mem_attn_sample",
    )(q, cache_k, cache_v)


def _ssm_param_kernel(are_ref, aim_ref, ldt_ref, bre_ref, bim_ref, abr_ref, abi_ref, bbr_ref, bbi_ref):
    dt = jnp.exp(ldt_ref[...])
    ar, ai = are_ref[...], aim_ref[...]
    mag = jnp.exp(dt * ar)
    abr, abi = mag * jnp.cos(dt * ai), mag * jnp.sin(dt * ai)
    den = ar * ar + ai * ai
    nr, ni = abr - 1.0, abi
    fre, fim = (nr * ar + ni * ai) / den, (ni * ar - nr * ai) / den
    abr_ref[...] = abr
    abi_ref[...] = abi
    for c in range(SSM_GROUP):
        br, bi = bre_ref[c], bim_ref[c]
        bbr_ref[c] = fre * br - fim * bi
        bbi_ref[c] = fre * bi + fim * br


def ssm_params(a_re, a_im, log_dt, b_re_t, b_im_t):
    d, g, n = a_re.shape
    c = b_re_t.shape[1]
    gn = pl.BlockSpec((None, g, n), lambda l: (l, 0, 0))
    cgn = pl.BlockSpec((None, c, g, n), lambda l: (l, 0, 0, 0))
    return pl.pallas_call(
        _ssm_param_kernel,
        grid=(d,),
        in_specs=[gn, gn, pl.BlockSpec((None, g, 1), lambda l: (l, 0, 0)), cgn, cgn],
        out_specs=[gn, gn, cgn, cgn],
        out_shape=[jax.ShapeDtypeStruct((d, g, n), F32)] * 2 + [jax.ShapeDtypeStruct((d, c, g, n), F32)] * 2,
        compiler_params=_cp(("arbitrary",), 16),
        name="ssm_params",
    )(a_re, a_im, log_dt.reshape(d, g, 1), b_re_t, b_im_t)


def _cmul_add(ar, ai, hr, hi, br, bi):
    return ar * hr - ai * hi + br, ar * hi + ai * hr + bi


def _ssm_out(u, sre, sim, cre_ref, cim_ref, d_ref):
    y = (jnp.dot(sre[...].astype(BF16), cre_ref[...], preferred_element_type=F32)
         - jnp.dot(sim[...].astype(BF16), cim_ref[...], preferred_element_type=F32)
         + d_ref[...] * u)
    return jax.nn.gelu(y)


def _ssm_prompt_kernel(u_ref, bre_ref, bim_ref, cre_ref, cim_ref, ar_ref, ai_ref, d_ref,
                       y_ref, hr_ref, hi_ref, sre, sim, *, clen):
    np_ = SCAN_LANES
    w = sre.shape[1]
    u = u_ref[...]
    ub = u.astype(BF16)
    sre[...] = jnp.dot(ub, bre_ref[...], preferred_element_type=F32)
    sim[...] = jnp.dot(ub, bim_ref[...], preferred_element_type=F32)
    ar1, ai1 = ar_ref[...], ai_ref[...]
    ar = jnp.broadcast_to(ar1, (np_, w))
    ai = jnp.broadcast_to(ai1, (np_, w))

    def rows(t):
        return pl.ds(pl.multiple_of(t * np_, np_), np_)

    def local_step(t, carry):
        return _cmul_add(ar, ai, carry[0], carry[1], sre[rows(t), :], sim[rows(t), :])

    zero = jnp.zeros((np_, w), F32)
    fr, fi = lax.fori_loop(0, clen, local_step, (zero, zero), unroll=SCAN_UNROLL)

    pr, pi = ar1, ai1
    for _ in range(int(math.log2(clen))):
        pr, pi = pr * pr - pi * pi, 2.0 * pr * pi
    row = lax.broadcasted_iota(jnp.int32, (np_, w), 0)
    cr = jnp.zeros((1, w), F32)
    ci = jnp.zeros((1, w), F32)
    hr0, hi0 = zero, zero
    for p in range(1, np_):
        cr, ci = _cmul_add(pr, pi, cr, ci, fr[p - 1:p], fi[p - 1:p])
        hr0 = jnp.where(row == p, cr, hr0)
        hi0 = jnp.where(row == p, ci, hi0)
    fin_r, fin_i = _cmul_add(pr, pi, cr, ci, fr[np_ - 1:np_], fi[np_ - 1:np_])
    hr_ref[...] = fin_r
    hi_ref[...] = fin_i

    def full_step(t, carry):
        nr, ni = _cmul_add(ar, ai, carry[0], carry[1], sre[rows(t), :], sim[rows(t), :])
        sre[rows(t), :] = nr
        sim[rows(t), :] = ni
        return nr, ni

    lax.fori_loop(0, clen, full_step, (hr0, hi0), unroll=SCAN_UNROLL)
    y_ref[...] = _ssm_out(u, sre, sim, cre_ref, cim_ref, d_ref)


def _ssm_specs(kdim):
    kmap = lambda *idx: (idx[kdim], 0, 0)
    return [pl.BlockSpec((None, SSM_UW, SSM_CW), kmap), pl.BlockSpec((None, SSM_UW, SSM_CW), kmap),
            pl.BlockSpec((None, SSM_CW, SSM_UW), kmap), pl.BlockSpec((None, SSM_CW, SSM_UW), kmap),
            pl.BlockSpec((None, 1, SSM_CW), kmap), pl.BlockSpec((None, 1, SSM_CW), kmap),
            pl.BlockSpec((None, 1, SSM_UW), kmap)]


def ssm_prompt(u_perm, mats):
    nb, seq, _ = u_perm.shape
    clen = seq // SCAN_LANES
    nstate = SSM_GROUPS * SSM_STATE
    st = pl.BlockSpec((None, 1, SSM_CW), lambda b, k: (b, 0, k))
    return pl.pallas_call(
        functools.partial(_ssm_prompt_kernel, clen=clen),
        grid=(nb, SSM_CHUNKS),
        in_specs=[pl.BlockSpec((None, seq, SSM_UW), lambda b, k: (b, 0, k))] + _ssm_specs(1),
        out_specs=[pl.BlockSpec((None, seq, SSM_UW), lambda b, k: (b, 0, k)), st, st],
        out_shape=[jax.ShapeDtypeStruct((nb, seq, BRANCH_DIM), F32),
                   jax.ShapeDtypeStruct((nb, 1, nstate), F32), jax.ShapeDtypeStruct((nb, 1, nstate), F32)],
        scratch_shapes=[pltpu.VMEM((seq, SSM_CW), F32), pltpu.VMEM((seq, SSM_CW), F32)],
        compiler_params=_cp(("parallel", "parallel"), 48),
        name="ssm_prompt",
    )(u_perm, *mats)


def _ssm_sample_kernel(u_ref, bre_ref, bim_ref, cre_ref, cim_ref, ar_ref, ai_ref, d_ref, h0r_ref, h0i_ref,
                       y_ref, hr_ref, hi_ref, sre, sim, *, steps, nb):
    w = sre.shape[1]
    u = u_ref[...]
    ub = u.astype(BF16)
    bur = jnp.dot(ub, bre_ref[...], preferred_element_type=F32)
    bui = jnp.dot(ub, bim_ref[...], preferred_element_type=F32)
    ar = jnp.broadcast_to(ar_ref[...], (nb, w))
    ai = jnp.broadcast_to(ai_ref[...], (nb, w))
    hr, hi = h0r_ref[...], h0i_ref[...]
    for t in range(steps):
        rs = slice(t * nb, (t + 1) * nb)
        hr, hi = _cmul_add(ar, ai, hr, hi, bur[rs], bui[rs])
        sre[rs, :] = hr
        sim[rs, :] = hi
    hr_ref[...] = hr
    hi_ref[...] = hi
    y_ref[...] = _ssm_out(u, sre, sim, cre_ref, cim_ref, d_ref)


def ssm_sample(z, row0, steps, nb, mats, h0_re, h0_im, layer):
    rows = steps * nb
    nstate = SSM_GROUPS * SSM_STATE
    st_in = pl.BlockSpec((None, nb, SSM_CW), lambda k: (layer, 0, k))
    st_out = pl.BlockSpec((nb, SSM_CW), lambda k: (0, k))
    return pl.pallas_call(
        functools.partial(_ssm_sample_kernel, steps=steps, nb=nb),
        grid=(SSM_CHUNKS,),
        in_specs=[pl.BlockSpec((rows, SSM_UW), lambda k: (row0 // rows, COL_U // SSM_UW + k))]
                 + _ssm_specs(0) + [st_in, st_in],
        out_specs=[pl.BlockSpec((rows, SSM_UW), lambda k: (0, k)), st_out, st_out],
        out_shape=[jax.ShapeDtypeStruct((rows, BRANCH_DIM), F32),
                   jax.ShapeDtypeStruct((nb, nstate), F32), jax.ShapeDtypeStruct((nb, nstate), F32)],
        scratch_shapes=[pltpu.VMEM((rows, SSM_CW), F32), pltpu.VMEM((rows, SSM_CW), F32)],
        compiler_params=_cp(("parallel",), 32),
        name="ssm_sample",
    )(z, *mats, h0_re, h0_im)


def _block_diag(x, pattern):
    eye = jnp.eye(SSM_GROUPS // SSM_CHUNKS, dtype=x.dtype)
    return jnp.einsum(pattern, x, eye)


def kernel(x_prompt, x_sample, cache_conv, cache_win_k, cache_win_v, state_ssm_re, state_ssm_im, cache_mem_k, cache_mem_v, mem_prompt, t5_bias, norm_mix_pre, norm_mix_post, norm_ffn_pre, norm_ffn_post, norm_mem, w_in, conv_w, conv_b, conv_ln_g, conv_ln_b, attn_sinks, ssm_a_re, ssm_a_im, ssm_log_dt, ssm_b_re, ssm_b_im, ssm_c_re, ssm_c_im, ssm_d, ssm_w_glu, w_mem_kv, w_branch, w_out, w_ffn_in, w_ffn_out):
    bp, seq, d = x_prompt.shape
    bs, steps, _ = x_sample.shape
    rows_p = bp * seq
    rows_s = bs * steps
    rows = rows_p + rows_s
    mlen = mem_prompt.shape[1]
    hist = CONV_WIDTH - 1
    kw = SWA_KV_HEADS * SWA_HEAD_DIM
    gpc = SSM_GROUPS // SSM_CHUNKS
    clen = seq // SCAN_LANES

    tm = rows // 5
    tm_merge = rows // 8
    tm_ffn_out = rows // 8
    tr = 320

    qi = np.arange(WINDOW)
    ki = np.arange(2 * WINDOW) - WINDOW
    dist_p = qi[:, None] - ki[None, :]
    bias_p = band_bias(t5_bias, dist_p, (dist_p >= 0) & (dist_p < WINDOW))
    nk_s = WINDOW + steps
    dist_s = (WINDOW + np.arange(SUBLANES))[:, None] - np.arange(nk_s)[None, :]
    bias_s = band_bias(t5_bias, dist_s, (dist_s >= 0) & (dist_s < WINDOW))[:, :steps]
    bias_s = bias_s.reshape(SWA_KV_HEADS, SWA_REP, steps, nk_s).transpose(0, 2, 1, 3)
    same_group = jnp.eye(SWA_KV_HEADS, dtype=bool)[:, None, None, None, :]
    bias_s = jnp.where(same_group, bias_s[..., None], NEG_INF)
    bias_s = bias_s.reshape(SWA_HEADS * steps, nk_s * SWA_KV_HEADS)

    abar_re, abar_im, bbar_re, bbar_im = ssm_params(
        ssm_a_re, ssm_a_im, ssm_log_dt, ssm_b_re.transpose(0, 3, 1, 2), ssm_b_im.transpose(0, 3, 1, 2))

    def in_mat(x):
        x = x.reshape(DEPTH, SSM_GROUP, SSM_CHUNKS, gpc, SSM_STATE)
        return _block_diag(x, 'lckgn,gh->lkgchn').reshape(DEPTH, SSM_CHUNKS, SSM_UW, SSM_CW).astype(BF16)

    def out_mat(x):
        x = x.reshape(DEPTH, SSM_CHUNKS, gpc, SSM_GROUP, SSM_STATE)
        return _block_diag(x, 'lkgcn,gh->lkgnhc').reshape(DEPTH, SSM_CHUNKS, SSM_CW, SSM_UW).astype(BF16)

    bmat_re, bmat_im = in_mat(bbar_re), in_mat(bbar_im)
    cmat_re, cmat_im = out_mat(ssm_c_re), out_mat(ssm_c_im)
    abar_re = abar_re.reshape(DEPTH, SSM_CHUNKS, 1, SSM_CW)
    abar_im = abar_im.reshape(DEPTH, SSM_CHUNKS, 1, SSM_CW)
    dvec = ssm_d.reshape(DEPTH, SSM_CHUNKS, 1, SSM_UW)

    cache_conv_t = cache_conv.transpose(0, 2, 1, 3)
    h0_re = state_ssm_re.reshape(DEPTH, bs, -1)
    h0_im = state_ssm_im.reshape(DEPTH, bs, -1)
    mem_rows = mem_prompt.reshape(bp * mlen, d)

    x = jnp.concatenate([x_prompt.reshape(rows_p, d), x_sample.transpose(1, 0, 2).reshape(rows_s, d)], axis=0)
    xn = rmsnorm_bf16(x, norm_mix_pre[0], tr)

    outs = [[] for _ in range(12)]
    for l in range(DEPTH):
        z = matmul(xn, w_in, F32, tm, 512, layer=l, n=COL_GATE)
        z_s = lax.slice(z, (rows_p, 0), (rows, COL_GATE))
        ybuf = jnp.zeros((4, rows, BRANCH_DIM), BF16)

        ybuf, conv_p = conv_prompt(z, bp, seq, conv_w[l], conv_b[l], conv_ln_g[l], conv_ln_b[l], 256, ybuf)
        ybuf, conv_s_t = conv_sample(z, rows_p, steps, bs, cache_conv_t[l], conv_w[l], conv_b[l],
                                     conv_ln_g[l], conv_ln_b[l], ybuf)

        ybuf = swa_prompt(z, bp, seq, attn_sinks[l], bias_p, ybuf)
        q_s = z_s[:, COL_Q:COL_K].reshape(steps, bs, SWA_KV_HEADS, SWA_REP, SWA_HEAD_DIM)
        q_s = q_s.transpose(1, 2, 0, 3, 4).reshape(bs, SWA_HEADS * steps, SWA_HEAD_DIM)
        k_s = z_s[:, COL_K:COL_V].reshape(steps, bs, SWA_KV_HEADS, SWA_HEAD_DIM).transpose(1, 0, 2, 3)
        v_s = z_s[:, COL_V:COL_U].reshape(steps, bs, SWA_KV_HEADS, SWA_HEAD_DIM).transpose(1, 0, 2, 3)
        k_s = k_s.reshape(bs, steps * SWA_KV_HEADS, SWA_HEAD_DIM)
        v_s = v_s.reshape(bs, steps * SWA_KV_HEADS, SWA_HEAD_DIM)
        sink_col = jnp.tile(attn_sinks[l].reshape(SWA_KV_HEADS, 1, SWA_REP), (1, steps, 1))
        sink_col = sink_col.reshape(SWA_HEADS * steps, 1)
        ob_s, wk_s, wv_s = swa_sample(q_s, k_s, v_s, cache_win_k, cache_win_v, l, bias_s, sink_col, 4)
        yb_s = ob_s.reshape(bs, SWA_KV_HEADS, steps, SWA_REP, SWA_HEAD_DIM).transpose(2, 0, 1, 3, 4)
        yb_s = yb_s.reshape(rows_s, BRANCH_DIM).astype(BF16)

        mats = (bmat_re[l], bmat_im[l], cmat_re[l], cmat_im[l], abar_re[l], abar_im[l], dvec[l])
        u_perm = lax.slice(z, (0, COL_U), (rows_p, COL_XQ))
        u_perm = u_perm.reshape(bp, SCAN_LANES, clen, BRANCH_DIM).transpose(0, 2, 1, 3)
        yg_perm, hr_p, hi_p = ssm_prompt(u_perm.reshape(bp, seq, BRANCH_DIM), mats)
        yc_perm = glu(yg_perm.reshape(rows_p, BRANCH_DIM), ssm_w_glu, l, 1024)
        yc_p = yc_perm.reshape(bp, clen, SCAN_LANES, BRANCH_DIM).transpose(0, 2, 1, 3).reshape(rows_p, BRANCH_DIM)
        yg_s, hr_s, hi_s = ssm_sample(z, rows_p, steps, bs, mats, h0_re, h0_im, l)
        ybuf = glu(yg_s, ssm_w_glu, l, rows_s, ybuf, 2, rows_p)

        kv = matmul(rmsnorm_bf16(mem_rows, norm_mem[l], 256), w_mem_kv, F32, bp * mlen, 512, layer=l)
        kv = kv.reshape(bp, mlen, 2 * BRANCH_DIM)
        ybuf = mem_attn_prompt(z, bp, seq, kv, 512, ybuf)
        xq_s = z_s[:, COL_XQ:COL_GATE].reshape(steps, bs, MEM_HEADS, MEM_HEAD_DIM).transpose(1, 0, 2, 3)
        yx_s = mem_attn_sample(xq_s.reshape(bs, steps * MEM_HEADS, MEM_HEAD_DIM), cache_mem_k, cache_mem_v, l, 2)
        yx_s = yx_s.reshape(bs, steps, BRANCH_DIM).transpose(1, 0, 2).reshape(rows_s, BRANCH_DIM)

        ybuf = lax.dynamic_update_slice(ybuf, yb_s[None], (1, rows_p, 0))
        ybuf = lax.dynamic_update_slice(ybuf, yc_p[None], (2, 0, 0))
        ybuf = lax.dynamic_update_slice(ybuf, yx_s[None], (3, rows_p, 0))
        merged = merge_branches(xn, ybuf, w_in, w_branch, l, tm_merge, 512)
        mix = matmul(merged, w_out, BF16, tm, 512, layer=l)
        x, hn = resid_norm(x, mix, norm_mix_post[l], norm_ffn_pre[l], tr)

        hid = ffn_in(hn, w_ffn_in, l, tm, 256)
        f = matmul(hid, w_ffn_out, BF16, tm_ffn_out, 256, layer=l, single_a=True)
        x, xn = resid_norm(x, f, norm_ffn_post[l], norm_mix_pre[l + 1] if l + 1 < DEPTH else None, tr)

        kv_win = jnp.stack([lax.slice(z, ((b + 1) * seq - WINDOW, COL_K), ((b + 1) * seq, COL_U))
                            for b in range(bp)])
        new = (conv_p, conv_s_t.transpose(1, 0, 2),
               kv_win[..., :kw].reshape(bp, WINDOW, SWA_KV_HEADS, SWA_HEAD_DIM),
               kv_win[..., kw:].reshape(bp, WINDOW, SWA_KV_HEADS, SWA_HEAD_DIM),
               wk_s, wv_s,
               hr_p.reshape(bp, SSM_GROUPS, SSM_STATE), hi_p.reshape(bp, SSM_GROUPS, SSM_STATE),
               hr_s.reshape(bs, SSM_GROUPS, SSM_STATE), hi_s.reshape(bs, SSM_GROUPS, SSM_STATE),
               kv[..., :BRANCH_DIM].reshape(bp, mlen, MEM_HEADS, MEM_HEAD_DIM),
               kv[..., BRANCH_DIM:].reshape(bp, mlen, MEM_HEADS, MEM_HEAD_DIM))
        for acc, val in zip(outs, new):
            acc.append(val)

    y_prompt = x[:rows_p].reshape(bp, seq, d)
    y_sample = x[rows_p:].reshape(steps, bs, d).transpose(1, 0, 2)
    return (y_prompt, y_sample) + tuple(jnp.stack(o) for o in outs)
```

```python
import functools
import math

import jax
import jax.numpy as jnp
import numpy as np
from jax import lax
from jax.experimental import pallas as pl
from jax.experimental.pallas import tpu as pltpu

F32 = jnp.float32
BF16 = jnp.bfloat16

D_MODEL = 4096
DEPTH = 4
BRANCH_DIM = 1024
CONV_WIDTH = 31
SWA_HEAD_DIM = 64
SWA_HEADS = 16
SWA_KV_HEADS = 4
SWA_REP = 4
WINDOW = 128
N_BUCKETS = 32
SSM_GROUP = 16
SSM_GROUPS = 64
SSM_STATE = 64
MEM_HEADS = 4
MEM_HEAD_DIM = 256
D_FF = 11008
COL_Q = 2048
COL_K = 3072
COL_V = 3328
COL_U = 3584
COL_XQ = 4608
COL_GATE = 5632
N_IN = COL_GATE + 4 * D_MODEL
NEG_INF = -1e30

SSM_CHUNKS = 4
SSM_CW = SSM_GROUPS // SSM_CHUNKS * SSM_STATE
SSM_UW = SSM_GROUPS // SSM_CHUNKS * SSM_GROUP
SCAN_LANES = 8
SCAN_UNROLL = 8


def _cp(sem, vmem_mb):
    return pltpu.CompilerParams(dimension_semantics=sem, vmem_limit_bytes=vmem_mb << 20)


def _rmsnorm_kernel(x_ref, g_ref, o_ref):
    x = x_ref[...]
    y = x * lax.rsqrt(jnp.mean(x * x, axis=-1, keepdims=True) + 1e-6)
    o_ref[...] = (y * g_ref[...]).astype(o_ref.dtype)


def rmsnorm_bf16(x, g, tr):
    m, d = x.shape
    return pl.pallas_call(
        _rmsnorm_kernel,
        grid=(m // tr,),
        in_specs=[pl.BlockSpec((tr, d), lambda i: (i, 0)), pl.BlockSpec((1, d), lambda i: (0, 0))],
        out_specs=pl.BlockSpec((tr, d), lambda i: (i, 0)),
        out_shape=jax.ShapeDtypeStruct((m, d), BF16),
        compiler_params=_cp(("parallel",), 40),
        name="rmsnorm",
    )(x, g.reshape(1, d))


def _stacked_rows(nfull, p_ref, s_ref):
    return jnp.where(pl.program_id(0) < nfull, p_ref[...], s_ref[...])


def _stacked_specs(tr, d, nfull):
    return [pl.BlockSpec((tr, d), lambda i: (jnp.minimum(i, nfull - 1), 0)), pl.BlockSpec((tr, d), lambda i: (0, 0))]


def _rmsnorm_stacked_kernel(xp_ref, xs_ref, g_ref, o_ref, *, nfull):
    x = _stacked_rows(nfull, xp_ref, xs_ref)
    y = x * lax.rsqrt(jnp.mean(x * x, axis=-1, keepdims=True) + 1e-6)
    o_ref[...] = (y * g_ref[...]).astype(o_ref.dtype)


def rmsnorm_stacked(x_p, x_s, g):
    tr, d = x_s.shape
    nfull = x_p.shape[0] // tr
    return pl.pallas_call(
        functools.partial(_rmsnorm_stacked_kernel, nfull=nfull),
        grid=(nfull + 1,),
        in_specs=_stacked_specs(tr, d, nfull) + [pl.BlockSpec((1, d), lambda i: (0, 0))],
        out_specs=pl.BlockSpec((tr, d), lambda i: (i, 0)),
        out_shape=jax.ShapeDtypeStruct((x_p.shape[0] + tr, d), BF16),
        compiler_params=_cp(("parallel",), 40),
        name="rmsnorm_stacked",
    )(x_p, x_s, g.reshape(1, d))


def _resid_norm_stacked_kernel(xp_ref, xs_ref, y_ref, gp_ref, gn_ref, xo_ref, hn_ref, *, nfull):
    y = y_ref[...].astype(F32)
    yn = y * lax.rsqrt(jnp.mean(y * y, axis=-1, keepdims=True) + 1e-6) * gp_ref[...]
    x = _stacked_rows(nfull, xp_ref, xs_ref) + yn
    xo_ref[...] = x
    h = x * lax.rsqrt(jnp.mean(x * x, axis=-1, keepdims=True) + 1e-6)
    hn_ref[...] = (h * gn_ref[...]).astype(hn_ref.dtype)


def resid_norm_stacked(x_p, x_s, y, g_post, g_next):
    tr, d = x_s.shape
    nfull = x_p.shape[0] // tr
    m = y.shape[0]
    row = pl.BlockSpec((tr, d), lambda i: (i, 0))
    vec = pl.BlockSpec((1, d), lambda i: (0, 0))
    return pl.pallas_call(
        functools.partial(_resid_norm_stacked_kernel, nfull=nfull),
        grid=(nfull + 1,),
        in_specs=_stacked_specs(tr, d, nfull) + [row, vec, vec],
        out_specs=[row, row],
        out_shape=[jax.ShapeDtypeStruct((m, d), F32), jax.ShapeDtypeStruct((m, d), BF16)],
        compiler_params=_cp(("parallel",), 48),
        name="resid_norm_stacked",
    )(x_p, x_s, y, g_post.reshape(1, d), g_next.reshape(1, d))


def _resid_split_kernel(x_ref, y_ref, gp_ref, op_ref, os_ref, *, nfull):
    y = y_ref[...].astype(F32)
    out = x_ref[...] + y * lax.rsqrt(jnp.mean(y * y, axis=-1, keepdims=True) + 1e-6) * gp_ref[...]

    @pl.when(pl.program_id(0) < nfull)
    def _():
        op_ref[...] = out

    @pl.when(pl.program_id(0) == nfull)
    def _():
        os_ref[...] = out


def resid_split(x, y, g_post, rows_s):
    m, d = x.shape
    tr = rows_s
    nfull = (m - rows_s) // tr
    row = pl.BlockSpec((tr, d), lambda i: (i, 0))
    return pl.pallas_call(
        functools.partial(_resid_split_kernel, nfull=nfull),
        grid=(nfull + 1,),
        in_specs=[row, row, pl.BlockSpec((1, d), lambda i: (0, 0))],
        out_specs=_stacked_specs(tr, d, nfull),
        out_shape=[jax.ShapeDtypeStruct((m - rows_s, d), F32), jax.ShapeDtypeStruct((rows_s, d), F32)],
        compiler_params=_cp(("arbitrary",), 48),
        name="resid_split",
    )(x, y, g_post.reshape(1, d))


def _resid_norm_kernel(x_ref, y_ref, gp_ref, gn_ref, xo_ref, hn_ref):
    y = y_ref[...].astype(F32)
    yn = y * lax.rsqrt(jnp.mean(y * y, axis=-1, keepdims=True) + 1e-6) * gp_ref[...]
    x = x_ref[...] + yn
    xo_ref[...] = x
    h = x * lax.rsqrt(jnp.mean(x * x, axis=-1, keepdims=True) + 1e-6)
    hn_ref[...] = (h * gn_ref[...]).astype(hn_ref.dtype)


def resid_norm(x, y, g_post, g_next, tr):
    m, d = x.shape
    row = pl.BlockSpec((tr, d), lambda i: (i, 0))
    vec = pl.BlockSpec((1, d), lambda i: (0, 0))
    return pl.pallas_call(
        _resid_norm_kernel,
        grid=(m // tr,),
        in_specs=[row, row, vec, vec],
        out_specs=[row, row],
        out_shape=[jax.ShapeDtypeStruct((m, d), F32), jax.ShapeDtypeStruct((m, d), BF16)],
        compiler_params=_cp(("parallel",), 48),
        name="resid_norm",
    )(x, y, g_post.reshape(1, d), g_next.reshape(1, d))


def _mm_kernel(a_ref, w_ref, o_ref):
    o_ref[...] = jnp.dot(a_ref[...], w_ref[...].astype(BF16), preferred_element_type=F32).astype(o_ref.dtype)


def _row_resident_spec(tm, k, single):
    if single:
        return pl.BlockSpec((tm, k), lambda i, j: (i, 0), pipeline_mode=pl.Buffered(1))
    return pl.BlockSpec((tm, k), lambda i, j: (i, 0))


def matmul(a, w, out_dtype, tm, tn, layer=None, n=None, single_a=False, vmem_mb=56):
    m, k = a.shape
    n = w.shape[-1] if n is None else n
    if layer is None:
        w_spec = pl.BlockSpec((k, tn), lambda i, j: (0, j))
    else:
        w_spec = pl.BlockSpec((None, k, tn), lambda i, j: (layer, 0, j))
    return pl.pallas_call(
        _mm_kernel,
        grid=(m // tm, n // tn),
        in_specs=[_row_resident_spec(tm, k, single_a), w_spec],
        out_specs=pl.BlockSpec((tm, tn), lambda i, j: (i, j)),
        out_shape=jax.ShapeDtypeStruct((m, n), out_dtype),
        compiler_params=_cp(("parallel", "parallel"), vmem_mb),
        name="matmul",
    )(a, w)


def _ffn_in_kernel(a_ref, wg_ref, wu_ref, o_ref):
    a = a_ref[...]
    g = jnp.dot(a, wg_ref[...].astype(BF16), preferred_element_type=F32)
    u = jnp.dot(a, wu_ref[...].astype(BF16), preferred_element_type=F32)
    o_ref[...] = (jax.nn.silu(g) * u).astype(o_ref.dtype)


def ffn_in(a, w, layer, tm, tn):
    m, k = a.shape
    nt = D_FF // tn
    return pl.pallas_call(
        _ffn_in_kernel,
        grid=(m // tm, nt),
        in_specs=[_row_resident_spec(tm, k, False),
                  pl.BlockSpec((None, k, tn), lambda i, j: (layer, 0, j)),
                  pl.BlockSpec((None, k, tn), lambda i, j: (layer, 0, j + nt))],
        out_specs=pl.BlockSpec((tm, tn), lambda i, j: (i, j)),
        out_shape=jax.ShapeDtypeStruct((m, D_FF), BF16),
        compiler_params=_cp(("parallel", "parallel"), 56),
        name="ffn_in",
    )(a, w, w)


def _merge_kernel(xn_ref, y_ref, wg_ref, wb_ref, o_ref, acc):
    b = pl.program_id(2)
    @pl.when(b == 0)
    def _():
        acc[...] = jnp.zeros(acc.shape, F32)

    gate = jnp.dot(xn_ref[...], wg_ref[...].astype(BF16), preferred_element_type=F32)
    term = jax.nn.sigmoid(gate) * jnp.dot(y_ref[...], wb_ref[...].astype(BF16), preferred_element_type=F32)
    total = acc[...] + term
    acc[...] = total
    o_ref[...] = total.astype(o_ref.dtype)


def merge_branches(xn, y_all, w_in, w_branch, layer, tm, tn):
    nb, m, kb = y_all.shape
    k = xn.shape[1]
    gate0 = COL_GATE // tn
    per = D_MODEL // tn
    return pl.pallas_call(
        _merge_kernel,
        grid=(m // tm, per, nb),
        in_specs=[pl.BlockSpec((tm, k), lambda i, j, b: (i, 0)),
                  pl.BlockSpec((None, tm, kb), lambda i, j, b: (b, i, 0)),
                  pl.BlockSpec((None, k, tn), lambda i, j, b: (layer, 0, gate0 + b * per + j)),
                  pl.BlockSpec((None, None, kb, tn), lambda i, j, b: (layer, b, 0, j))],
        out_specs=pl.BlockSpec((tm, tn), lambda i, j, b: (i, j)),
        out_shape=jax.ShapeDtypeStruct((m, D_MODEL), BF16),
        scratch_shapes=[pltpu.VMEM((tm, tn), F32)],
        compiler_params=_cp(("parallel", "parallel", "arbitrary"), 56),
        name="merge",
    )(xn, y_all, w_in, w_branch)


def _skip_ref(body, pos):
    def wrapped(*refs):
        return body(*refs[:pos], *refs[pos + 1:])
    return wrapped


def _glu_kernel(y_ref, w_ref, o_ref):
    y = y_ref[...]
    s = jnp.dot(y.astype(BF16), w_ref[...].astype(BF16), preferred_element_type=F32)
    o_ref[...] = (y * jax.nn.sigmoid(s)).astype(o_ref.dtype)


def glu(y, w, layer, tr, ybuf=None, branch=0, row0=0):
    m, d = y.shape
    in_specs = [pl.BlockSpec((tr, d), lambda i: (i, 0)), pl.BlockSpec((None, d, d), lambda i: (layer, 0, 0))]
    if ybuf is None:
        return pl.pallas_call(
            _glu_kernel,
            grid=(m // tr,),
            in_specs=in_specs,
            out_specs=pl.BlockSpec((tr, d), lambda i: (i, 0)),
            out_shape=jax.ShapeDtypeStruct((m, d), BF16),
            compiler_params=_cp(("parallel",), 40),
            name="glu",
        )(y, w)
    return pl.pallas_call(
        _skip_ref(_glu_kernel, 2),
        grid=(m // tr,),
        in_specs=in_specs + [pl.BlockSpec(memory_space=pl.ANY)],
        out_specs=pl.BlockSpec((None, tr, d), lambda i: (branch, row0 // tr + i, 0)),
        out_shape=jax.ShapeDtypeStruct(ybuf.shape, ybuf.dtype),
        input_output_aliases={2: 0},
        compiler_params=_cp(("parallel",), 40),
        name="glu_into",
    )(y, w, ybuf)


def _ln_silu(y, g, b):
    mu = jnp.mean(y, axis=-1, keepdims=True)
    yc = y - mu
    yn = yc * lax.rsqrt(jnp.mean(yc * yc, axis=-1, keepdims=True) + 1e-5)
    return jax.nn.silu(yn * g + b)


CONV_HALO = 32
CONV_LANES = 128


SUBLANES = 8


def _conv_prompt_kernel(z_ref, w_ref, b_ref, g_ref, beta_ref, y_ref, cn_ref, xx, xs, acc, *, tt):
    c = CONV_DIM_
    off = CONV_HALO - (CONV_WIDTH - 1)
    span = tt + CONV_HALO - SUBLANES

    @pl.when(pl.program_id(1) == 0)
    def _():
        xx[0:CONV_HALO, :] = jnp.zeros((CONV_HALO, c), F32)

    xx[CONV_HALO:CONV_HALO + tt, :] = z_ref[:, :c] * jax.nn.sigmoid(z_ref[:, c:])
    for r in range(1, SUBLANES):
        xs[r - 1, 0:span, :] = xx[r:r + span, :]
    def lane_chunk(lc, carry):
        ls = pl.ds(pl.multiple_of(lc * CONV_LANES, CONV_LANES), CONV_LANES)
        a = None
        for w in range(CONV_WIDTH):
            q, r = divmod(off + w, SUBLANES)
            lo = SUBLANES * q
            src = xx[lo:lo + tt, ls] if r == 0 else xs[r - 1, lo:lo + tt, ls]
            term = src * w_ref[w:w + 1, ls]
            a = term if a is None else a + term
        acc[:, ls] = a
        return carry

    lax.fori_loop(0, c // CONV_LANES, lane_chunk, 0)
    y_ref[...] = _ln_silu(acc[...] + b_ref[...], g_ref[...], beta_ref[...]).astype(y_ref.dtype)
    cn_ref[...] = xx[tt + off:tt + CONV_HALO, :]
    xx[0:CONV_HALO, :] = xx[tt:tt + CONV_HALO, :]


CONV_DIM_ = BRANCH_DIM


def conv_prompt(z, nb, seq, w, b, g, beta, tt, ybuf):
    c = CONV_DIM_
    nt = seq // tt
    vec = pl.BlockSpec((1, c), lambda bi, t: (0, 0))
    return pl.pallas_call(
        _skip_ref(functools.partial(_conv_prompt_kernel, tt=tt), 5),
        grid=(nb, nt),
        in_specs=[pl.BlockSpec((tt, 2 * c), lambda bi, t: (bi * nt + t, 0)),
                  pl.BlockSpec((CONV_WIDTH, c), lambda bi, t: (0, 0)), vec, vec, vec,
                  pl.BlockSpec(memory_space=pl.ANY)],
        out_specs=[pl.BlockSpec((None, tt, c), lambda bi, t: (0, bi * nt + t, 0)),
                   pl.BlockSpec((None, CONV_WIDTH - 1, c), lambda bi, t: (bi, 0, 0))],
        out_shape=[jax.ShapeDtypeStruct(ybuf.shape, ybuf.dtype),
                   jax.ShapeDtypeStruct((nb, CONV_WIDTH - 1, c), F32)],
        input_output_aliases={5: 0},
        scratch_shapes=[pltpu.VMEM((tt + CONV_HALO, c), F32),
                        pltpu.VMEM((SUBLANES - 1, tt + CONV_HALO - SUBLANES, c), F32),
                        pltpu.VMEM((tt, c), F32)],
        compiler_params=_cp(("parallel", "arbitrary"), 40),
        name="conv_prompt",
    )(z, w, b.reshape(1, c), g.reshape(1, c), beta.reshape(1, c), ybuf)


def _conv_sample_kernel(z_ref, cache_ref, w_ref, b_ref, g_ref, beta_ref, y_ref, cn_ref, *, steps, nb):
    c = CONV_DIM_
    hist = CONV_WIDTH - 1
    a = z_ref[:, :c] * jax.nn.sigmoid(z_ref[:, c:])
    for t in range(steps):
        acc = None
        for w in range(CONV_WIDTH):
            idx = t + w
            src = cache_ref[idx] if idx < hist else a[(idx - hist) * nb:(idx - hist + 1) * nb]
            term = src * w_ref[w:w + 1, :]
            acc = term if acc is None else acc + term
        y_ref[t * nb:(t + 1) * nb, :] = _ln_silu(acc + b_ref[...], g_ref[...], beta_ref[...]).astype(y_ref.dtype)
    for r in range(hist - steps):
        cn_ref[r] = cache_ref[r + steps]
    for t in range(steps):
        cn_ref[hist - steps + t] = a[t * nb:(t + 1) * nb]


def conv_sample(z, row0, steps, nb, cache_t, w, b, g, beta, ybuf):
    c = CONV_DIM_
    rows = steps * nb
    hist = CONV_WIDTH - 1
    vec = pl.BlockSpec((1, c), lambda i: (0, 0))
    return pl.pallas_call(
        _skip_ref(functools.partial(_conv_sample_kernel, steps=steps, nb=nb), 6),
        grid=(1,),
        in_specs=[pl.BlockSpec((rows, 2 * c), lambda i: (row0 // rows, 0)),
                  pl.BlockSpec((hist, nb, c), lambda i: (0, 0, 0)),
                  pl.BlockSpec((CONV_WIDTH, c), lambda i: (0, 0)), vec, vec, vec,
                  pl.BlockSpec(memory_space=pl.ANY)],
        out_specs=[pl.BlockSpec((None, rows, c), lambda i: (0, row0 // rows, 0)),
                   pl.BlockSpec((hist, nb, c), lambda i: (0, 0, 0))],
        out_shape=[jax.ShapeDtypeStruct(ybuf.shape, ybuf.dtype), jax.ShapeDtypeStruct((hist, nb, c), F32)],
        input_output_aliases={6: 0},
        compiler_params=_cp(("arbitrary",), 40),
        name="conv_sample",
    )(z, cache_t, w, b.reshape(1, c), g.reshape(1, c), beta.reshape(1, c), ybuf)


def _t5_bucket_np(dist):
    n = np.maximum(dist, 0)
    max_exact = N_BUCKETS // 2
    nf = np.maximum(n, 1).astype(np.float32)
    large = max_exact + (np.log(nf / np.float32(max_exact)) / np.float32(math.log(WINDOW / max_exact))
                         * np.float32(N_BUCKETS - max_exact)).astype(np.int32)
    large = np.minimum(large, N_BUCKETS - 1)
    return np.where(n < max_exact, n, large)


def _bias_kernel(t5_ref, bucket_ref, o_ref):
    h = pl.program_id(0)
    bucket = bucket_ref[...]
    acc = jnp.full(bucket.shape, NEG_INF, F32)
    for b in range(N_BUCKETS):
        acc = jnp.where(bucket == b, t5_ref[b, h], acc)
    o_ref[...] = acc


def band_bias(t5_bias, dist, mask):
    bucket = np.where(mask, _t5_bucket_np(dist), -1).astype(np.int32)
    nq, nk = bucket.shape
    return pl.pallas_call(
        _bias_kernel,
        grid=(SWA_HEADS,),
        in_specs=[pl.BlockSpec(memory_space=pltpu.SMEM), pl.BlockSpec((nq, nk), lambda h: (0, 0))],
        out_specs=pl.BlockSpec((None, nq, nk), lambda h: (h, 0, 0)),
        out_shape=jax.ShapeDtypeStruct((SWA_HEADS, nq, nk), F32),
        compiler_params=_cp(("arbitrary",), 16),
        name="band_bias",
    )(t5_bias, jnp.asarray(bucket))


def _softmax_sink_pv(s, sink, v):
    m = jnp.maximum(jnp.max(s, axis=-1, keepdims=True), sink)
    p = jnp.exp(s - m)
    den = jnp.sum(p, axis=-1, keepdims=True) + jnp.exp(sink - m)
    return jnp.dot(p.astype(BF16), v, preferred_element_type=F32) / den


def _swa_prompt_kernel(sink_ref, q_ref, kp_ref, kc_ref, vp_ref, vc_ref, bias_ref, o_ref):
    blk = WINDOW
    scale = SWA_HEAD_DIM ** -0.5
    k = jnp.concatenate([kp_ref[...], kc_ref[...]], axis=0)
    v = jnp.concatenate([vp_ref[...], vc_ref[...]], axis=0)
    col = lax.broadcasted_iota(jnp.int32, (blk, 2 * blk), 1)
    no_prev = jnp.logical_and(pl.program_id(1) == 0, col < blk)
    q = q_ref[...]
    for hp in range(SWA_HEADS // 2):
        outs = []
        for h in (2 * hp, 2 * hp + 1):
            g = h // SWA_REP
            hs = slice(h * SWA_HEAD_DIM, (h + 1) * SWA_HEAD_DIM)
            gs = slice(g * SWA_HEAD_DIM, (g + 1) * SWA_HEAD_DIM)
            s = lax.dot_general(q[:, hs].astype(BF16), k[:, gs].astype(BF16), (((1,), (1,)), ((), ())),
                                preferred_element_type=F32) * scale
            s = jnp.where(no_prev, NEG_INF, s + bias_ref[h])
            outs.append(_softmax_sink_pv(s, sink_ref[h], v[:, gs].astype(BF16)))
        o_ref[:, 2 * hp * SWA_HEAD_DIM:(2 * hp + 2) * SWA_HEAD_DIM] = (
            jnp.concatenate(outs, axis=1).astype(o_ref.dtype))


def swa_prompt(z, nb, seq, sinks, bias, ybuf):
    blk = WINDOW
    nj = seq // blk
    qw = SWA_HEADS * SWA_HEAD_DIM
    kw = SWA_KV_HEADS * SWA_HEAD_DIM
    cur = lambda col: (lambda b, j: (b * nj + j, col))
    prev = lambda col: (lambda b, j: (b * nj + jnp.maximum(j - 1, 0), col))
    return pl.pallas_call(
        _skip_ref(_swa_prompt_kernel, 7),
        grid=(nb, nj),
        in_specs=[pl.BlockSpec(memory_space=pltpu.SMEM),
                  pl.BlockSpec((blk, qw), cur(COL_Q // qw)),
                  pl.BlockSpec((blk, kw), prev(COL_K // kw)), pl.BlockSpec((blk, kw), cur(COL_K // kw)),
                  pl.BlockSpec((blk, kw), prev(COL_V // kw)), pl.BlockSpec((blk, kw), cur(COL_V // kw)),
                  pl.BlockSpec((SWA_HEADS, blk, 2 * blk), lambda b, j: (0, 0, 0)),
                  pl.BlockSpec(memory_space=pl.ANY)],
        out_specs=pl.BlockSpec((None, blk, qw), lambda b, j: (1, b * nj + j, 0)),
        out_shape=jax.ShapeDtypeStruct(ybuf.shape, ybuf.dtype),
        input_output_aliases={7: 0},
        compiler_params=_cp(("parallel", "arbitrary"), 32),
        name="swa_prompt",
    )(sinks, z, z, z, z, z, bias, ybuf)


def _swa_sample_kernel(q_ref, kc_ref, vc_ref, kn_ref, vn_ref, bias_ref, sink_ref, o_ref, wk_ref, wv_ref):
    scale = SWA_HEAD_DIM ** -0.5
    nbat, win, ng, hd = kc_ref.shape
    new_rows = kn_ref.shape[1]
    for bi in range(nbat):
        kk = jnp.concatenate([kc_ref[bi].reshape(win * ng, hd), kn_ref[bi]], axis=0)
        vv = jnp.concatenate([vc_ref[bi].reshape(win * ng, hd), vn_ref[bi]], axis=0)
        wk_ref[bi] = kk[new_rows:, :].reshape(win, ng, hd)
        wv_ref[bi] = vv[new_rows:, :].reshape(win, ng, hd)
        s = lax.dot_general(q_ref[bi].astype(BF16), kk.astype(BF16), (((1,), (1,)), ((), ())),
                            preferred_element_type=F32) * scale + bias_ref[...]
        o_ref[bi] = _softmax_sink_pv(s, sink_ref[...], vv.astype(BF16))


def swa_sample(q, k_new, v_new, cache_k, cache_v, layer, bias, sink_col, bb):
    nb, rows, hd = q.shape
    new_rows = k_new.shape[1]
    _, _, win, ng, _ = cache_k.shape
    nk = win * ng + new_rows
    cache_spec = pl.BlockSpec((None, bb, win, ng, hd), lambda b: (layer, b, 0, 0, 0))
    new_spec = pl.BlockSpec((bb, new_rows, hd), lambda b: (b, 0, 0))
    win_spec = pl.BlockSpec((bb, win, ng, hd), lambda b: (b, 0, 0, 0))
    qo_spec = pl.BlockSpec((bb, rows, hd), lambda b: (b, 0, 0))
    win_shape = jax.ShapeDtypeStruct((nb, win, ng, hd), F32)
    return pl.pallas_call(
        _swa_sample_kernel,
        grid=(nb // bb,),
        in_specs=[qo_spec, cache_spec, cache_spec, new_spec, new_spec,
                  pl.BlockSpec((rows, nk), lambda b: (0, 0)), pl.BlockSpec((rows, 1), lambda b: (0, 0))],
        out_specs=[qo_spec, win_spec, win_spec],
        out_shape=[jax.ShapeDtypeStruct(q.shape, F32), win_shape, win_shape],
        compiler_params=_cp(("parallel",), 32),
        name="swa_sample",
    )(q, cache_k, cache_v, k_new, v_new, bias, sink_col)


def _attend(q, mk, mv, allowed=None):
    s = lax.dot_general(q.astype(BF16), mk.astype(BF16), (((1,), (1,)), ((), ())),
                        preferred_element_type=F32) * (MEM_HEAD_DIM ** -0.5)
    if allowed is not None:
        s = jnp.where(allowed, s, NEG_INF)
    m = jnp.max(s, axis=-1, keepdims=True)
    p = jnp.exp(s - m)
    den = jnp.sum(p, axis=-1, keepdims=True)
    return jnp.dot(p.astype(BF16), mv.astype(BF16), preferred_element_type=F32) / den


def _mem_attn_kernel(q0_ref, q1_ref, q2_ref, q3_ref, mk_ref, mv_ref, o_ref):
    for h, q_ref in enumerate((q0_ref, q1_ref, q2_ref, q3_ref)):
        hs = slice(h * MEM_HEAD_DIM, (h + 1) * MEM_HEAD_DIM)
        o_ref[:, hs] = _attend(q_ref[...], mk_ref[:, hs], mv_ref[:, hs]).astype(o_ref.dtype)


def _mem_attn_sample_kernel(q_ref, mk_ref, mv_ref, o_ref):
    nbat, rows, hd = q_ref.shape
    nkeys = mk_ref.shape[1] * MEM_HEADS
    same_head = (lax.broadcasted_iota(jnp.int32, (rows, nkeys), 0) % MEM_HEADS
                 == lax.broadcasted_iota(jnp.int32, (rows, nkeys), 1) % MEM_HEADS)
    for bi in range(nbat):
        mk = mk_ref[bi].reshape(nkeys, hd)
        mv = mv_ref[bi].reshape(nkeys, hd)
        o_ref[bi] = _attend(q_ref[bi], mk, mv, same_head).astype(o_ref.dtype)


def mem_attn_prompt(z, nb, seq, kv, tq, ybuf):
    nt = seq // tq
    mlen = kv.shape[1]
    hd = MEM_HEAD_DIM
    q_specs = [pl.BlockSpec((tq, hd), functools.partial(lambda b, t, h: (b * nt + t, COL_XQ // hd + h), h=h))
               for h in range(MEM_HEADS)]
    return pl.pallas_call(
        _skip_ref(_mem_attn_kernel, 6),
        grid=(nb, nt),
        in_specs=q_specs + [pl.BlockSpec((None, mlen, BRANCH_DIM), lambda b, t: (b, 0, 0)),
                            pl.BlockSpec((None, mlen, BRANCH_DIM), lambda b, t: (b, 0, 1)),
                            pl.BlockSpec(memory_space=pl.ANY)],
        out_specs=pl.BlockSpec((None, tq, BRANCH_DIM), lambda b, t: (3, b * nt + t, 0)),
        out_shape=jax.ShapeDtypeStruct(ybuf.shape, ybuf.dtype),
        input_output_aliases={6: 0},
        compiler_params=_cp(("parallel", "arbitrary"), 32),
        name="mem_attn_prompt",
    )(z, z, z, z, kv, kv, ybuf)


def mem_attn_sample(q, cache_k, cache_v, layer, bb):
    nb, rows, hd = q.shape
    mlen = cache_k.shape[2]
    qo_spec = pl.BlockSpec((bb, rows, hd), lambda b: (b, 0, 0))
    cache_spec = pl.BlockSpec((None, bb, mlen, MEM_HEADS, hd), lambda b: (layer, b, 0, 0, 0))
    return pl.pallas_call(
        _mem_attn_sample_kernel,
        grid=(nb // bb,),
        in_specs=[qo_spec, cache_spec, cache_spec],
        out_specs=qo_spec,
        out_shape=jax.ShapeDtypeStruct(q.shape, BF16),
        compiler_params=_cp(("parallel",), 40),
        name="mem_attn_sample",
    )(q, cache_k, cache_v)


def _ssm_param_kernel(are_ref, aim_ref, ldt_ref, bre_ref, bim_ref, abr_ref, abi_ref, bbr_ref, bbi_ref):
    dt = jnp.exp(ldt_ref[...])
    ar, ai = are_ref[...], aim_ref[...]
    mag = jnp.exp(dt * ar)
    abr, abi = mag * jnp.cos(dt * ai), mag * jnp.sin(dt * ai)
    den = ar * ar + ai * ai
    nr, ni = abr - 1.0, abi
    fre, fim = (nr * ar + ni * ai) / den, (ni * ar - nr * ai) / den
    abr_ref[...] = abr
    abi_ref[...] = abi
    for c in range(SSM_GROUP):
        br, bi = bre_ref[c], bim_ref[c]
        bbr_ref[c] = fre * br - fim * bi
        bbi_ref[c] = fre * bi + fim * br


def ssm_params(a_re, a_im, log_dt, b_re_t, b_im_t):
    d, g, n = a_re.shape
    c = b_re_t.shape[1]
    gn = pl.BlockSpec((None, g, n), lambda l: (l, 0, 0))
    cgn = pl.BlockSpec((None, c, g, n), lambda l: (l, 0, 0, 0))
    return pl.pallas_call(
        _ssm_param_kernel,
        grid=(d,),
        in_specs=[gn, gn, pl.BlockSpec((None, g, 1), lambda l: (l, 0, 0)), cgn, cgn],
        out_specs=[gn, gn, cgn, cgn],
        out_shape=[jax.ShapeDtypeStruct((d, g, n), F32)] * 2 + [jax.ShapeDtypeStruct((d, c, g, n), F32)] * 2,
        compiler_params=_cp(("arbitrary",), 16),
        name="ssm_params",
    )(a_re, a_im, log_dt.reshape(d, g, 1), b_re_t, b_im_t)


def _cmul_add(ar, ai, hr, hi, br, bi):
    return ar * hr - ai * hi + br, ar * hi + ai * hr + bi


def _ssm_out(u, sre, sim, cre_ref, cim_ref, d_ref):
    y = (jnp.dot(sre[...].astype(BF16), cre_ref[...], preferred_element_type=F32)
         - jnp.dot(sim[...].astype(BF16), cim_ref[...], preferred_element_type=F32)
         + d_ref[...] * u)
    return jax.nn.gelu(y)


LANES = 128
PIECE_PITCH = 264


def _ssm_prompt_kernel(u_ref, bre_ref, bim_ref, cre_ref, cim_ref, ar_ref, ai_ref, d_ref,
                       y_ref, hr_ref, hi_ref, sre, sim, upad, uperm, ypad, *, clen):
    np_ = SCAN_LANES
    w = sre.shape[1]
    slabs = [slice(s * LANES, (s + 1) * LANES) for s in range(u_ref.shape[1] // LANES)]

    def rows(t):
        return pl.ds(pl.multiple_of(t * np_, np_), np_)

    def piece_rows(t):
        return pl.ds(t, np_, stride=PIECE_PITCH)

    for s, ls in enumerate(slabs):
        for p in range(np_):
            upad[s, p * PIECE_PITCH:p * PIECE_PITCH + clen, :] = u_ref[p * clen:(p + 1) * clen, ls]

    def gather(t, carry):
        for s, ls in enumerate(slabs):
            uperm[rows(t), ls] = upad[s, piece_rows(t), :]
        return carry

    lax.fori_loop(0, clen, gather, 0, unroll=SCAN_UNROLL)
    u = uperm[...]
    ub = u.astype(BF16)
    sre[...] = jnp.dot(ub, bre_ref[...], preferred_element_type=F32)
    sim[...] = jnp.dot(ub, bim_ref[...], preferred_element_type=F32)
    ar1, ai1 = ar_ref[...], ai_ref[...]
    ar = jnp.broadcast_to(ar1, (np_, w))
    ai = jnp.broadcast_to(ai1, (np_, w))

    def local_step(t, carry):
        return _cmul_add(ar, ai, carry[0], carry[1], sre[rows(t), :], sim[rows(t), :])

    zero = jnp.zeros((np_, w), F32)
    fr, fi = lax.fori_loop(0, clen, local_step, (zero, zero), unroll=SCAN_UNROLL)

    pr, pi = ar1, ai1
    for _ in range(int(math.log2(clen))):
        pr, pi = pr * pr - pi * pi, 2.0 * pr * pi
    row = lax.broadcasted_iota(jnp.int32, (np_, w), 0)
    cr = jnp.zeros((1, w), F32)
    ci = jnp.zeros((1, w), F32)
    hr0, hi0 = zero, zero
    for p in range(1, np_):
        cr, ci = _cmul_add(pr, pi, cr, ci, fr[p - 1:p], fi[p - 1:p])
        hr0 = jnp.where(row == p, cr, hr0)
        hi0 = jnp.where(row == p, ci, hi0)
    fin_r, fin_i = _cmul_add(pr, pi, cr, ci, fr[np_ - 1:np_], fi[np_ - 1:np_])
    hr_ref[...] = fin_r
    hi_ref[...] = fin_i

    def full_step(t, carry):
        nr, ni = _cmul_add(ar, ai, carry[0], carry[1], sre[rows(t), :], sim[rows(t), :])
        sre[rows(t), :] = nr
        sim[rows(t), :] = ni
        return nr, ni

    lax.fori_loop(0, clen, full_step, (hr0, hi0), unroll=SCAN_UNROLL)
    uperm[...] = _ssm_out(u, sre, sim, cre_ref, cim_ref, d_ref)

    def scatter(t, carry):
        for s, ls in enumerate(slabs):
            ypad[s, piece_rows(t), :] = uperm[rows(t), ls]
        return carry

    lax.fori_loop(0, clen, scatter, 0, unroll=SCAN_UNROLL)
    for s, ls in enumerate(slabs):
        for p in range(np_):
            y_ref[p * clen:(p + 1) * clen, ls] = ypad[s, p * PIECE_PITCH:p * PIECE_PITCH + clen, :]


def _ssm_specs(kdim):
    kmap = lambda *idx: (idx[kdim], 0, 0)
    return [pl.BlockSpec((None, SSM_UW, SSM_CW), kmap), pl.BlockSpec((None, SSM_UW, SSM_CW), kmap),
            pl.BlockSpec((None, SSM_CW, SSM_UW), kmap), pl.BlockSpec((None, SSM_CW, SSM_UW), kmap),
            pl.BlockSpec((None, 1, SSM_CW), kmap), pl.BlockSpec((None, 1, SSM_CW), kmap),
            pl.BlockSpec((None, 1, SSM_UW), kmap)]


def ssm_prompt(z, nb, seq, mats):
    clen = seq // SCAN_LANES
    nstate = SSM_GROUPS * SSM_STATE
    st = pl.BlockSpec((None, 1, SSM_CW), lambda b, k: (b, 0, k))
    pad_shape = (SSM_UW // LANES, SCAN_LANES * PIECE_PITCH, LANES)
    return pl.pallas_call(
        functools.partial(_ssm_prompt_kernel, clen=clen),
        grid=(nb, SSM_CHUNKS),
        in_specs=[pl.BlockSpec((seq, SSM_UW), lambda b, k: (b, COL_U // SSM_UW + k))] + _ssm_specs(1),
        out_specs=[pl.BlockSpec((seq, SSM_UW), lambda b, k: (b, k)), st, st],
        out_shape=[jax.ShapeDtypeStruct((nb * seq, BRANCH_DIM), F32),
                   jax.ShapeDtypeStruct((nb, 1, nstate), F32), jax.ShapeDtypeStruct((nb, 1, nstate), F32)],
        scratch_shapes=[pltpu.VMEM((seq, SSM_CW), F32), pltpu.VMEM((seq, SSM_CW), F32),
                        pltpu.VMEM(pad_shape, F32), pltpu.VMEM((seq, SSM_UW), F32), pltpu.VMEM(pad_shape, F32)],
        compiler_params=_cp(("parallel", "parallel"), 48),
        name="ssm_prompt",
    )(z, *mats)


def _ssm_sample_kernel(u_ref, bre_ref, bim_ref, cre_ref, cim_ref, ar_ref, ai_ref, d_ref, h0r_ref, h0i_ref,
                       y_ref, hr_ref, hi_ref, sre, sim, *, steps, nb):
    w = sre.shape[1]
    u = u_ref[...]
    ub = u.astype(BF16)
    bur = jnp.dot(ub, bre_ref[...], preferred_element_type=F32)
    bui = jnp.dot(ub, bim_ref[...], preferred_element_type=F32)
    ar = jnp.broadcast_to(ar_ref[...], (nb, w))
    ai = jnp.broadcast_to(ai_ref[...], (nb, w))
    hr, hi = h0r_ref[...], h0i_ref[...]
    for t in range(steps):
        rs = slice(t * nb, (t + 1) * nb)
        hr, hi = _cmul_add(ar, ai, hr, hi, bur[rs], bui[rs])
        sre[rs, :] = hr
        sim[rs, :] = hi
    hr_ref[...] = hr
    hi_ref[...] = hi
    y_ref[...] = _ssm_out(u, sre, sim, cre_ref, cim_ref, d_ref)


def ssm_sample(z, row0, steps, nb, mats, h0_re, h0_im, layer):
    rows = steps * nb
    nstate = SSM_GROUPS * SSM_STATE
    st_in = pl.BlockSpec((None, nb, SSM_CW), lambda k: (layer, 0, k))
    st_out = pl.BlockSpec((nb, SSM_CW), lambda k: (0, k))
    return pl.pallas_call(
        functools.partial(_ssm_sample_kernel, steps=steps, nb=nb),
        grid=(SSM_CHUNKS,),
        in_specs=[pl.BlockSpec((rows, SSM_UW), lambda k: (row0 // rows, COL_U // SSM_UW + k))]
                 + _ssm_specs(0) + [st_in, st_in],
        out_specs=[pl.BlockSpec((rows, SSM_UW), lambda k: (0, k)), st_out, st_out],
        out_shape=[jax.ShapeDtypeStruct((rows, BRANCH_DIM), F32),
                   jax.ShapeDtypeStruct((nb, nstate), F32), jax.ShapeDtypeStruct((nb, nstate), F32)],
        scratch_shapes=[pltpu.VMEM((rows, SSM_CW), F32), pltpu.VMEM((rows, SSM_CW), F32)],
        compiler_params=_cp(("parallel",), 32),
        name="ssm_sample",
    )(z, *mats, h0_re, h0_im)


def _block_diag(x, pattern):
    eye = jnp.eye(SSM_GROUPS // SSM_CHUNKS, dtype=x.dtype)
    return jnp.einsum(pattern, x, eye)


def kernel(x_prompt, x_sample, cache_conv, cache_win_k, cache_win_v, state_ssm_re, state_ssm_im, cache_mem_k, cache_mem_v, mem_prompt, t5_bias, norm_mix_pre, norm_mix_post, norm_ffn_pre, norm_ffn_post, norm_mem, w_in, conv_w, conv_b, conv_ln_g, conv_ln_b, attn_sinks, ssm_a_re, ssm_a_im, ssm_log_dt, ssm_b_re, ssm_b_im, ssm_c_re, ssm_c_im, ssm_d, ssm_w_glu, w_mem_kv, w_branch, w_out, w_ffn_in, w_ffn_out):
    bp, seq, d = x_prompt.shape
    bs, steps, _ = x_sample.shape
    rows_p = bp * seq
    rows_s = bs * steps
    rows = rows_p + rows_s
    mlen = mem_prompt.shape[1]
    kw = SWA_KV_HEADS * SWA_HEAD_DIM
    gpc = SSM_GROUPS // SSM_CHUNKS

    tm = rows // 5
    tm_merge = rows // 8
    tm_ffn_out = rows // 8
    tr = 320

    qi = np.arange(WINDOW)
    ki = np.arange(2 * WINDOW) - WINDOW
    dist_p = qi[:, None] - ki[None, :]
    bias_p = band_bias(t5_bias, dist_p, (dist_p >= 0) & (dist_p < WINDOW))
    nk_s = WINDOW + steps
    dist_s = (WINDOW + np.arange(SUBLANES))[:, None] - np.arange(nk_s)[None, :]
    bias_s = band_bias(t5_bias, dist_s, (dist_s >= 0) & (dist_s < WINDOW))[:, :steps]
    bias_s = bias_s.reshape(SWA_KV_HEADS, SWA_REP, steps, nk_s).transpose(0, 2, 1, 3)
    same_group = jnp.eye(SWA_KV_HEADS, dtype=bool)[:, None, None, None, :]
    bias_s = jnp.where(same_group, bias_s[..., None], NEG_INF)
    bias_s = bias_s.reshape(SWA_HEADS * steps, nk_s * SWA_KV_HEADS)

    abar_re, abar_im, bbar_re, bbar_im = ssm_params(
        ssm_a_re, ssm_a_im, ssm_log_dt, ssm_b_re.transpose(0, 3, 1, 2), ssm_b_im.transpose(0, 3, 1, 2))

    def in_mat(x):
        x = x.reshape(DEPTH, SSM_GROUP, SSM_CHUNKS, gpc, SSM_STATE)
        return _block_diag(x, 'lckgn,gh->lkgchn').reshape(DEPTH, SSM_CHUNKS, SSM_UW, SSM_CW).astype(BF16)

    def out_mat(x):
        x = x.reshape(DEPTH, SSM_CHUNKS, gpc, SSM_GROUP, SSM_STATE)
        return _block_diag(x, 'lkgcn,gh->lkgnhc').reshape(DEPTH, SSM_CHUNKS, SSM_CW, SSM_UW).astype(BF16)

    bmat_re, bmat_im = in_mat(bbar_re), in_mat(bbar_im)
    cmat_re, cmat_im = out_mat(ssm_c_re), out_mat(ssm_c_im)
    abar_re = abar_re.reshape(DEPTH, SSM_CHUNKS, 1, SSM_CW)
    abar_im = abar_im.reshape(DEPTH, SSM_CHUNKS, 1, SSM_CW)
    dvec = ssm_d.reshape(DEPTH, SSM_CHUNKS, 1, SSM_UW)

    cache_conv_t = cache_conv.transpose(0, 2, 1, 3)
    h0_re = state_ssm_re.reshape(DEPTH, bs, -1)
    h0_im = state_ssm_im.reshape(DEPTH, bs, -1)
    mem_rows = mem_prompt.reshape(bp * mlen, d)

    x_p0 = x_prompt.reshape(rows_p, d)
    x_s0 = x_sample.transpose(1, 0, 2).reshape(rows_s, d)
    x = None
    xn = rmsnorm_stacked(x_p0, x_s0, norm_mix_pre[0])

    outs = [[] for _ in range(12)]
    for l in range(DEPTH):
        z = matmul(xn, w_in, F32, tm, 512, layer=l, n=COL_GATE)
        z_s = lax.slice(z, (rows_p, 0), (rows, COL_GATE))
        ybuf = jnp.zeros((4, rows, BRANCH_DIM), BF16)

        ybuf, conv_p = conv_prompt(z, bp, seq, conv_w[l], conv_b[l], conv_ln_g[l], conv_ln_b[l], 256, ybuf)
        ybuf, conv_s_t = conv_sample(z, rows_p, steps, bs, cache_conv_t[l], conv_w[l], conv_b[l],
                                     conv_ln_g[l], conv_ln_b[l], ybuf)

        ybuf = swa_prompt(z, bp, seq, attn_sinks[l], bias_p, ybuf)
        q_s = z_s[:, COL_Q:COL_K].reshape(steps, bs, SWA_KV_HEADS, SWA_REP, SWA_HEAD_DIM)
        q_s = q_s.transpose(1, 2, 0, 3, 4).reshape(bs, SWA_HEADS * steps, SWA_HEAD_DIM)
        k_s = z_s[:, COL_K:COL_V].reshape(steps, bs, SWA_KV_HEADS, SWA_HEAD_DIM).transpose(1, 0, 2, 3)
        v_s = z_s[:, COL_V:COL_U].reshape(steps, bs, SWA_KV_HEADS, SWA_HEAD_DIM).transpose(1, 0, 2, 3)
        k_s = k_s.reshape(bs, steps * SWA_KV_HEADS, SWA_HEAD_DIM)
        v_s = v_s.reshape(bs, steps * SWA_KV_HEADS, SWA_HEAD_DIM)
        sink_col = jnp.tile(attn_sinks[l].reshape(SWA_KV_HEADS, 1, SWA_REP), (1, steps, 1))
        sink_col = sink_col.reshape(SWA_HEADS * steps, 1)
        ob_s, wk_s, wv_s = swa_sample(q_s, k_s, v_s, cache_win_k, cache_win_v, l, bias_s, sink_col, 4)
        yb_s = ob_s.reshape(bs, SWA_KV_HEADS, steps, SWA_REP, SWA_HEAD_DIM).transpose(2, 0, 1, 3, 4)
        yb_s = yb_s.reshape(rows_s, BRANCH_DIM).astype(BF16)

        mats = (bmat_re[l], bmat_im[l], cmat_re[l], cmat_im[l], abar_re[l], abar_im[l], dvec[l])
        yg_p, hr_p, hi_p = ssm_prompt(z, bp, seq, mats)
        ybuf = glu(yg_p, ssm_w_glu, l, 1024, ybuf, 2, 0)
        yg_s, hr_s, hi_s = ssm_sample(z, rows_p, steps, bs, mats, h0_re, h0_im, l)
        ybuf = glu(yg_s, ssm_w_glu, l, rows_s, ybuf, 2, rows_p)

        kv = matmul(rmsnorm_bf16(mem_rows, norm_mem[l], 256), w_mem_kv, F32, bp * mlen, 512, layer=l)
        kv = kv.reshape(bp, mlen, 2 * BRANCH_DIM)
        ybuf = mem_attn_prompt(z, bp, seq, kv, 512, ybuf)
        xq_s = z_s[:, COL_XQ:COL_GATE].reshape(steps, bs, MEM_HEADS, MEM_HEAD_DIM).transpose(1, 0, 2, 3)
        yx_s = mem_attn_sample(xq_s.reshape(bs, steps * MEM_HEADS, MEM_HEAD_DIM), cache_mem_k, cache_mem_v, l, 2)
        yx_s = yx_s.reshape(bs, steps, BRANCH_DIM).transpose(1, 0, 2).reshape(rows_s, BRANCH_DIM)

        ybuf = lax.dynamic_update_slice(ybuf, yb_s[None], (1, rows_p, 0))
        ybuf = lax.dynamic_update_slice(ybuf, yx_s[None], (3, rows_p, 0))
        merged = merge_branches(xn, ybuf, w_in, w_branch, l, tm_merge, 512)
        mix = matmul(merged, w_out, BF16, tm, 512, layer=l)
        if l == 0:
            x, hn = resid_norm_stacked(x_p0, x_s0, mix, norm_mix_post[l], norm_ffn_pre[l])
        else:
            x, hn = resid_norm(x, mix, norm_mix_post[l], norm_ffn_pre[l], tr)

        hid = ffn_in(hn, w_ffn_in, l, tm, 256)
        f = matmul(hid, w_ffn_out, BF16, tm_ffn_out, 256, layer=l, single_a=True)
        if l + 1 < DEPTH:
            x, xn = resid_norm(x, f, norm_ffn_post[l], norm_mix_pre[l + 1], tr)
        else:
            y_p, y_s = resid_split(x, f, norm_ffn_post[l], rows_s)

        kv_win = jnp.stack([lax.slice(z, ((b + 1) * seq - WINDOW, COL_K), ((b + 1) * seq, COL_U))
                            for b in range(bp)])
        new = (conv_p, conv_s_t.transpose(1, 0, 2),
               kv_win[..., :kw].reshape(bp, WINDOW, SWA_KV_HEADS, SWA_HEAD_DIM),
               kv_win[..., kw:].reshape(bp, WINDOW, SWA_KV_HEADS, SWA_HEAD_DIM),
               wk_s, wv_s,
               hr_p.reshape(bp, SSM_GROUPS, SSM_STATE), hi_p.reshape(bp, SSM_GROUPS, SSM_STATE),
               hr_s.reshape(bs, SSM_GROUPS, SSM_STATE), hi_s.reshape(bs, SSM_GROUPS, SSM_STATE),
               kv[..., :BRANCH_DIM].reshape(bp, mlen, MEM_HEADS, MEM_HEAD_DIM),
               kv[..., BRANCH_DIM:].reshape(bp, mlen, MEM_HEADS, MEM_HEAD_DIM))
        for acc, val in zip(outs, new):
            acc.append(val)

    y_prompt = y_p.reshape(bp, seq, d)
    y_sample = y_s.reshape(steps, bs, d).transpose(1, 0, 2)
    return (y_prompt, y_sample) + tuple(jnp.stack(o) for o in outs)
```

```python
import functools
import math

import jax
import jax.numpy as jnp
import numpy as np
from jax import lax
from jax.experimental import pallas as pl
from jax.experimental.pallas import tpu as pltpu

F32 = jnp.float32
BF16 = jnp.bfloat16

D_MODEL = 4096
DEPTH = 4
BRANCH_DIM = 1024
CONV_WIDTH = 31
SWA_HEAD_DIM = 64
SWA_HEADS = 16
SWA_KV_HEADS = 4
SWA_REP = 4
WINDOW = 128
N_BUCKETS = 32
SSM_GROUP = 16
SSM_GROUPS = 64
SSM_STATE = 64
MEM_HEADS = 4
MEM_HEAD_DIM = 256
D_FF = 11008
COL_Q = 2048
COL_K = 3072
COL_V = 3328
COL_U = 3584
COL_XQ = 4608
COL_GATE = 5632
N_IN = COL_GATE + 4 * D_MODEL
NEG_INF = -1e30

SWA_SCALE = SWA_HEAD_DIM ** -0.5
assert math.log2(SWA_HEAD_DIM) % 2 == 0

SSM_CHUNKS = 4
SSM_CW = SSM_GROUPS // SSM_CHUNKS * SSM_STATE
SSM_UW = SSM_GROUPS // SSM_CHUNKS * SSM_GROUP
SCAN_LANES = 8
SCAN_UNROLL = 8


def _cp(sem, vmem_mb):
    return pltpu.CompilerParams(dimension_semantics=sem, vmem_limit_bytes=vmem_mb << 20)


def _rmsnorm_kernel(x_ref, g_ref, o_ref):
    x = x_ref[...]
    y = x * lax.rsqrt(jnp.mean(x * x, axis=-1, keepdims=True) + 1e-6)
    o_ref[...] = (y * g_ref[...]).astype(o_ref.dtype)


def rmsnorm_bf16(x, g, tr):
    m, d = x.shape
    return pl.pallas_call(
        _rmsnorm_kernel,
        grid=(m // tr,),
        in_specs=[pl.BlockSpec((tr, d), lambda i: (i, 0)), pl.BlockSpec((1, d), lambda i: (0, 0))],
        out_specs=pl.BlockSpec((tr, d), lambda i: (i, 0)),
        out_shape=jax.ShapeDtypeStruct((m, d), BF16),
        compiler_params=_cp(("parallel",), 40),
        name="rmsnorm",
    )(x, g.reshape(1, d))


def _stacked_rows(nfull, p_ref, s_ref):
    return jnp.where(pl.program_id(0) < nfull, p_ref[...], s_ref[...])


def _stacked_specs(tr, d, nfull):
    return [pl.BlockSpec((tr, d), lambda i: (jnp.minimum(i, nfull - 1), 0)), pl.BlockSpec((tr, d), lambda i: (0, 0))]


def _rmsnorm_stacked_kernel(xp_ref, xs_ref, g_ref, o_ref, *, nfull):
    x = _stacked_rows(nfull, xp_ref, xs_ref)
    y = x * lax.rsqrt(jnp.mean(x * x, axis=-1, keepdims=True) + 1e-6)
    o_ref[...] = (y * g_ref[...]).astype(o_ref.dtype)


def rmsnorm_stacked(x_p, x_s, g):
    tr, d = x_s.shape
    nfull = x_p.shape[0] // tr
    return pl.pallas_call(
        functools.partial(_rmsnorm_stacked_kernel, nfull=nfull),
        grid=(nfull + 1,),
        in_specs=_stacked_specs(tr, d, nfull) + [pl.BlockSpec((1, d), lambda i: (0, 0))],
        out_specs=pl.BlockSpec((tr, d), lambda i: (i, 0)),
        out_shape=jax.ShapeDtypeStruct((x_p.shape[0] + tr, d), BF16),
        compiler_params=_cp(("parallel",), 40),
        name="rmsnorm_stacked",
    )(x_p, x_s, g.reshape(1, d))


def _resid_norm_stacked_kernel(xp_ref, xs_ref, y_ref, gp_ref, gn_ref, xo_ref, hn_ref, *, nfull):
    y = y_ref[...].astype(F32)
    yn = y * lax.rsqrt(jnp.mean(y * y, axis=-1, keepdims=True) + 1e-6) * gp_ref[...]
    x = _stacked_rows(nfull, xp_ref, xs_ref) + yn
    xo_ref[...] = x
    h = x * lax.rsqrt(jnp.mean(x * x, axis=-1, keepdims=True) + 1e-6)
    hn_ref[...] = (h * gn_ref[...]).astype(hn_ref.dtype)


def resid_norm_stacked(x_p, x_s, y, g_post, g_next):
    tr, d = x_s.shape
    nfull = x_p.shape[0] // tr
    m = y.shape[0]
    row = pl.BlockSpec((tr, d), lambda i: (i, 0))
    vec = pl.BlockSpec((1, d), lambda i: (0, 0))
    return pl.pallas_call(
        functools.partial(_resid_norm_stacked_kernel, nfull=nfull),
        grid=(nfull + 1,),
        in_specs=_stacked_specs(tr, d, nfull) + [row, vec, vec],
        out_specs=[row, row],
        out_shape=[jax.ShapeDtypeStruct((m, d), F32), jax.ShapeDtypeStruct((m, d), BF16)],
        compiler_params=_cp(("parallel",), 48),
        name="resid_norm_stacked",
    )(x_p, x_s, y, g_post.reshape(1, d), g_next.reshape(1, d))


def _resid_split_kernel(x_ref, y_ref, gp_ref, op_ref, os_ref, *, nfull):
    y = y_ref[...].astype(F32)
    out = x_ref[...] + y * lax.rsqrt(jnp.mean(y * y, axis=-1, keepdims=True) + 1e-6) * gp_ref[...]

    @pl.when(pl.program_id(0) < nfull)
    def _():
        op_ref[...] = out

    @pl.when(pl.program_id(0) == nfull)
    def _():
        os_ref[...] = out


def resid_split(x, y, g_post, rows_s):
    m, d = x.shape
    tr = rows_s
    nfull = (m - rows_s) // tr
    row = pl.BlockSpec((tr, d), lambda i: (i, 0))
    return pl.pallas_call(
        functools.partial(_resid_split_kernel, nfull=nfull),
        grid=(nfull + 1,),
        in_specs=[row, row, pl.BlockSpec((1, d), lambda i: (0, 0))],
        out_specs=_stacked_specs(tr, d, nfull),
        out_shape=[jax.ShapeDtypeStruct((m - rows_s, d), F32), jax.ShapeDtypeStruct((rows_s, d), F32)],
        compiler_params=_cp(("arbitrary",), 48),
        name="resid_split",
    )(x, y, g_post.reshape(1, d))


def _resid_norm_kernel(x_ref, y_ref, gp_ref, gn_ref, xo_ref, hn_ref):
    y = y_ref[...].astype(F32)
    yn = y * lax.rsqrt(jnp.mean(y * y, axis=-1, keepdims=True) + 1e-6) * gp_ref[...]
    x = x_ref[...] + yn
    xo_ref[...] = x
    h = x * lax.rsqrt(jnp.mean(x * x, axis=-1, keepdims=True) + 1e-6)
    hn_ref[...] = (h * gn_ref[...]).astype(hn_ref.dtype)


def resid_norm(x, y, g_post, g_next, tr):
    m, d = x.shape
    row = pl.BlockSpec((tr, d), lambda i: (i, 0))
    vec = pl.BlockSpec((1, d), lambda i: (0, 0))
    return pl.pallas_call(
        _resid_norm_kernel,
        grid=(m // tr,),
        in_specs=[row, row, vec, vec],
        out_specs=[row, row],
        out_shape=[jax.ShapeDtypeStruct((m, d), F32), jax.ShapeDtypeStruct((m, d), BF16)],
        compiler_params=_cp(("parallel",), 48),
        name="resid_norm",
    )(x, y, g_post.reshape(1, d), g_next.reshape(1, d))


def _mm_kernel(a_ref, w_ref, o_ref):
    o_ref[...] = jnp.dot(a_ref[...], w_ref[...].astype(BF16), preferred_element_type=F32).astype(o_ref.dtype)


def matmul(a, w, layer, out_dtype, tm, tn, n=None, single_a=False):
    m, k = a.shape
    n = w.shape[-1] if n is None else n
    a_mode = dict(pipeline_mode=pl.Buffered(1)) if single_a else {}
    return pl.pallas_call(
        _mm_kernel,
        grid=(m // tm, n // tn),
        in_specs=[pl.BlockSpec((tm, k), lambda i, j: (i, 0), **a_mode),
                  pl.BlockSpec((None, k, tn), lambda i, j: (layer, 0, j))],
        out_specs=pl.BlockSpec((tm, tn), lambda i, j: (i, j)),
        out_shape=jax.ShapeDtypeStruct((m, n), out_dtype),
        compiler_params=_cp(("parallel", "parallel"), 56),
        name="matmul",
    )(a, w)


def _ffn_in_kernel(a_ref, wg_ref, wu_ref, o_ref):
    a = a_ref[...]
    g = jnp.dot(a, wg_ref[...].astype(BF16), preferred_element_type=F32)
    u = jnp.dot(a, wu_ref[...].astype(BF16), preferred_element_type=F32)
    o_ref[...] = (jax.nn.silu(g) * u).astype(o_ref.dtype)


def ffn_in(a, w, layer, tm, tn):
    m, k = a.shape
    nt = D_FF // tn
    return pl.pallas_call(
        _ffn_in_kernel,
        grid=(m // tm, nt),
        in_specs=[pl.BlockSpec((tm, k), lambda i, j: (i, 0)),
                  pl.BlockSpec((None, k, tn), lambda i, j: (layer, 0, j)),
                  pl.BlockSpec((None, k, tn), lambda i, j: (layer, 0, j + nt))],
        out_specs=pl.BlockSpec((tm, tn), lambda i, j: (i, j)),
        out_shape=jax.ShapeDtypeStruct((m, D_FF), BF16),
        compiler_params=_cp(("parallel", "parallel"), 56),
        name="ffn_in",
    )(a, w, w)


def _merge_kernel(xn_ref, y_ref, wg_ref, wb_ref, o_ref, acc):
    b = pl.program_id(2)
    @pl.when(b == 0)
    def _():
        acc[...] = jnp.zeros(acc.shape, F32)

    gate = jnp.dot(xn_ref[...], wg_ref[...].astype(BF16), preferred_element_type=F32)
    term = jax.nn.sigmoid(gate) * jnp.dot(y_ref[...], wb_ref[...].astype(BF16), preferred_element_type=F32)
    total = acc[...] + term
    acc[...] = total
    o_ref[...] = total.astype(o_ref.dtype)


def merge_branches(xn, y_all, w_in, w_branch, layer, tm, tn):
    nb, m, kb = y_all.shape
    k = xn.shape[1]
    gate0 = COL_GATE // tn
    per = D_MODEL // tn
    return pl.pallas_call(
        _merge_kernel,
        grid=(m // tm, per, nb),
        in_specs=[pl.BlockSpec((tm, k), lambda i, j, b: (i, 0)),
                  pl.BlockSpec((None, tm, kb), lambda i, j, b: (b, i, 0)),
                  pl.BlockSpec((None, k, tn), lambda i, j, b: (layer, 0, gate0 + b * per + j)),
                  pl.BlockSpec((None, None, kb, tn), lambda i, j, b: (layer, b, 0, j))],
        out_specs=pl.BlockSpec((tm, tn), lambda i, j, b: (i, j)),
        out_shape=jax.ShapeDtypeStruct((m, D_MODEL), BF16),
        scratch_shapes=[pltpu.VMEM((tm, tn), F32)],
        compiler_params=_cp(("parallel", "parallel", "arbitrary"), 56),
        name="merge",
    )(xn, y_all, w_in, w_branch)


def _skip_ref(body, pos):
    def wrapped(*refs):
        return body(*refs[:pos], *refs[pos + 1:])
    return wrapped


def _glu_kernel(y_ref, w_ref, o_ref):
    y = y_ref[...]
    s = jnp.dot(y.astype(BF16), w_ref[...].astype(BF16), preferred_element_type=F32)
    o_ref[...] = (y * jax.nn.sigmoid(s)).astype(o_ref.dtype)


def glu(y, w, layer, tr, ybuf=None, branch=0, row0=0):
    m, d = y.shape
    in_specs = [pl.BlockSpec((tr, d), lambda i: (i, 0)), pl.BlockSpec((None, d, d), lambda i: (layer, 0, 0))]
    if ybuf is None:
        return pl.pallas_call(
            _glu_kernel,
            grid=(m // tr,),
            in_specs=in_specs,
            out_specs=pl.BlockSpec((tr, d), lambda i: (i, 0)),
            out_shape=jax.ShapeDtypeStruct((m, d), BF16),
            compiler_params=_cp(("parallel",), 40),
            name="glu",
        )(y, w)
    return pl.pallas_call(
        _skip_ref(_glu_kernel, 2),
        grid=(m // tr,),
        in_specs=in_specs + [pl.BlockSpec(memory_space=pl.ANY)],
        out_specs=pl.BlockSpec((None, tr, d), lambda i: (branch, row0 // tr + i, 0)),
        out_shape=jax.ShapeDtypeStruct(ybuf.shape, ybuf.dtype),
        input_output_aliases={2: 0},
        compiler_params=_cp(("parallel",), 40),
        name="glu_into",
    )(y, w, ybuf)


def _ln_silu(y, g, b):
    mu = jnp.mean(y, axis=-1, keepdims=True)
    yc = y - mu
    yn = yc * lax.rsqrt(jnp.mean(yc * yc, axis=-1, keepdims=True) + 1e-5)
    return jax.nn.silu(yn * g + b)


CONV_HALO = 32
CONV_LANES = 128


SUBLANES = 8


def _conv_prompt_kernel(z_ref, w_ref, b_ref, g_ref, beta_ref, y_ref, cn_ref, xx, xs, acc, *, tt):
    c = CONV_DIM_
    off = CONV_HALO - (CONV_WIDTH - 1)
    span = tt + CONV_HALO - SUBLANES

    @pl.when(pl.program_id(1) == 0)
    def _():
        xx[0:CONV_HALO, :] = jnp.zeros((CONV_HALO, c), F32)

    xx[CONV_HALO:CONV_HALO + tt, :] = z_ref[:, :c] * jax.nn.sigmoid(z_ref[:, c:])
    for r in range(1, SUBLANES):
        xs[r - 1, 0:span, :] = xx[r:r + span, :]
    def lane_chunk(lc, carry):
        ls = pl.ds(pl.multiple_of(lc * CONV_LANES, CONV_LANES), CONV_LANES)
        a = None
        for w in range(CONV_WIDTH):
            q, r = divmod(off + w, SUBLANES)
            lo = SUBLANES * q
            src = xx[lo:lo + tt, ls] if r == 0 else xs[r - 1, lo:lo + tt, ls]
            term = src * w_ref[w:w + 1, ls]
            a = term if a is None else a + term
        acc[:, ls] = a
        return carry

    lax.fori_loop(0, c // CONV_LANES, lane_chunk, 0)
    y_ref[...] = _ln_silu(acc[...] + b_ref[...], g_ref[...], beta_ref[...]).astype(y_ref.dtype)
    cn_ref[...] = xx[tt + off:tt + CONV_HALO, :]
    xx[0:CONV_HALO, :] = xx[tt:tt + CONV_HALO, :]


CONV_DIM_ = BRANCH_DIM


def conv_prompt(z, nb, seq, w, b, g, beta, tt, ybuf):
    c = CONV_DIM_
    nt = seq // tt
    vec = pl.BlockSpec((1, c), lambda bi, t: (0, 0))
    return pl.pallas_call(
        _skip_ref(functools.partial(_conv_prompt_kernel, tt=tt), 5),
        grid=(nb, nt),
        in_specs=[pl.BlockSpec((tt, 2 * c), lambda bi, t: (bi * nt + t, 0)),
                  pl.BlockSpec((CONV_WIDTH, c), lambda bi, t: (0, 0)), vec, vec, vec,
                  pl.BlockSpec(memory_space=pl.ANY)],
        out_specs=[pl.BlockSpec((None, tt, c), lambda bi, t: (0, bi * nt + t, 0)),
                   pl.BlockSpec((None, CONV_WIDTH - 1, c), lambda bi, t: (bi, 0, 0))],
        out_shape=[jax.ShapeDtypeStruct(ybuf.shape, ybuf.dtype),
                   jax.ShapeDtypeStruct((nb, CONV_WIDTH - 1, c), F32)],
        input_output_aliases={5: 0},
        scratch_shapes=[pltpu.VMEM((tt + CONV_HALO, c), F32),
                        pltpu.VMEM((SUBLANES - 1, tt + CONV_HALO - SUBLANES, c), F32),
                        pltpu.VMEM((tt, c), F32)],
        compiler_params=_cp(("parallel", "arbitrary"), 40),
        name="conv_prompt",
    )(z, w, b.reshape(1, c), g.reshape(1, c), beta.reshape(1, c), ybuf)


def _conv_sample_kernel(z_ref, cache_ref, w_ref, b_ref, g_ref, beta_ref, y_ref, cn_ref, past, *, steps, nb):
    c = CONV_DIM_
    hist = CONV_WIDTH - 1
    for r in range(hist):
        past[r] = cache_ref[:, r, :]
    a = z_ref[:, :c] * jax.nn.sigmoid(z_ref[:, c:])
    for t in range(steps):
        acc = None
        for w in range(CONV_WIDTH):
            idx = t + w
            src = past[idx] if idx < hist else a[(idx - hist) * nb:(idx - hist + 1) * nb]
            term = src * w_ref[w:w + 1, :]
            acc = term if acc is None else acc + term
        y_ref[t * nb:(t + 1) * nb, :] = _ln_silu(acc + b_ref[...], g_ref[...], beta_ref[...]).astype(y_ref.dtype)
    for r in range(hist - steps):
        cn_ref[:, r, :] = past[r + steps]
    for t in range(steps):
        cn_ref[:, hist - steps + t, :] = a[t * nb:(t + 1) * nb]


def conv_sample(z, row0, steps, nb, cache, layer, w, b, g, beta, ybuf):
    c = CONV_DIM_
    rows = steps * nb
    hist = CONV_WIDTH - 1
    vec = pl.BlockSpec((1, c), lambda i: (0, 0))
    return pl.pallas_call(
        _skip_ref(functools.partial(_conv_sample_kernel, steps=steps, nb=nb), 6),
        grid=(1,),
        in_specs=[pl.BlockSpec((rows, 2 * c), lambda i: (row0 // rows, 0)),
                  pl.BlockSpec((None, nb, hist, c), lambda i: (layer, 0, 0, 0)),
                  pl.BlockSpec((CONV_WIDTH, c), lambda i: (0, 0)), vec, vec, vec,
                  pl.BlockSpec(memory_space=pl.ANY)],
        out_specs=[pl.BlockSpec((None, rows, c), lambda i: (0, row0 // rows, 0)),
                   pl.BlockSpec((nb, hist, c), lambda i: (0, 0, 0))],
        out_shape=[jax.ShapeDtypeStruct(ybuf.shape, ybuf.dtype), jax.ShapeDtypeStruct((nb, hist, c), F32)],
        input_output_aliases={6: 0},
        scratch_shapes=[pltpu.VMEM((hist, nb, c), F32)],
        compiler_params=_cp(("arbitrary",), 40),
        name="conv_sample",
    )(z, cache, w, b.reshape(1, c), g.reshape(1, c), beta.reshape(1, c), ybuf)


def _t5_bucket_np(dist):
    n = np.maximum(dist, 0)
    max_exact = N_BUCKETS // 2
    nf = np.maximum(n, 1).astype(np.float32)
    large = max_exact + (np.log(nf / np.float32(max_exact)) / np.float32(math.log(WINDOW / max_exact))
                         * np.float32(N_BUCKETS - max_exact)).astype(np.int32)
    large = np.minimum(large, N_BUCKETS - 1)
    return np.where(n < max_exact, n, large)


def _bias_kernel(t5_ref, bucket_ref, o_ref):
    h = pl.program_id(0)
    bucket = bucket_ref[...]
    acc = jnp.full(bucket.shape, NEG_INF, F32)
    for b in range(N_BUCKETS):
        acc = jnp.where(bucket == b, t5_ref[b, h], acc)
    o_ref[...] = acc


def band_bias(t5_bias, dist, mask):
    bucket = np.where(mask, _t5_bucket_np(dist), -1).astype(np.int32)
    nq, nk = bucket.shape
    return pl.pallas_call(
        _bias_kernel,
        grid=(SWA_HEADS,),
        in_specs=[pl.BlockSpec(memory_space=pltpu.SMEM), pl.BlockSpec((nq, nk), lambda h: (0, 0))],
        out_specs=pl.BlockSpec((None, nq, nk), lambda h: (h, 0, 0)),
        out_shape=jax.ShapeDtypeStruct((SWA_HEADS, nq, nk), F32),
        compiler_params=_cp(("arbitrary",), 16),
        name="band_bias",
    )(t5_bias, jnp.asarray(bucket))


def _softmax_sink_pv(s, sink, v):
    m = jnp.maximum(jnp.max(s, axis=-1, keepdims=True), sink)
    p = jnp.exp(s - m)
    den = jnp.sum(p, axis=-1, keepdims=True) + jnp.exp(sink - m)
    return jnp.dot(p.astype(BF16), v, preferred_element_type=F32) / den


def _swa_prompt_kernel(sink_ref, q_ref, kp_ref, kc_ref, vp_ref, vc_ref, bias_ref, o_ref):
    k = jnp.concatenate([kp_ref[...], kc_ref[...]], axis=0)
    v = jnp.concatenate([vp_ref[...], vc_ref[...]], axis=0)
    q = q_ref[...] * SWA_SCALE
    for hp in range(SWA_HEADS // 2):
        outs = []
        for h in (2 * hp, 2 * hp + 1):
            g = h // SWA_REP
            hs = slice(h * SWA_HEAD_DIM, (h + 1) * SWA_HEAD_DIM)
            gs = slice(g * SWA_HEAD_DIM, (g + 1) * SWA_HEAD_DIM)
            s = lax.dot_general(q[:, hs].astype(BF16), k[:, gs].astype(BF16), (((1,), (1,)), ((), ())),
                                preferred_element_type=F32) + bias_ref[h]
            outs.append(_softmax_sink_pv(s, sink_ref[h], v[:, gs].astype(BF16)))
        o_ref[:, 2 * hp * SWA_HEAD_DIM:(2 * hp + 2) * SWA_HEAD_DIM] = (
            jnp.concatenate(outs, axis=1).astype(o_ref.dtype))


def swa_prompt(z, nb, seq, sinks, bias, ybuf):
    blk = WINDOW
    nj = seq // blk
    qw = SWA_HEADS * SWA_HEAD_DIM
    kw = SWA_KV_HEADS * SWA_HEAD_DIM
    cur = lambda col: (lambda b, j: (b * nj + j, col))
    prev = lambda col: (lambda b, j: (b * nj + jnp.maximum(j - 1, 0), col))
    return pl.pallas_call(
        _skip_ref(_swa_prompt_kernel, 7),
        grid=(nb, nj),
        in_specs=[pl.BlockSpec(memory_space=pltpu.SMEM),
                  pl.BlockSpec((blk, qw), cur(COL_Q // qw)),
                  pl.BlockSpec((blk, kw), prev(COL_K // kw)), pl.BlockSpec((blk, kw), cur(COL_K // kw)),
                  pl.BlockSpec((blk, kw), prev(COL_V // kw)), pl.BlockSpec((blk, kw), cur(COL_V // kw)),
                  pl.BlockSpec((None, SWA_HEADS, blk, 2 * blk), lambda b, j: (jnp.minimum(j, 1), 0, 0, 0)),
                  pl.BlockSpec(memory_space=pl.ANY)],
        out_specs=pl.BlockSpec((None, blk, qw), lambda b, j: (1, b * nj + j, 0)),
        out_shape=jax.ShapeDtypeStruct(ybuf.shape, ybuf.dtype),
        input_output_aliases={7: 0},
        compiler_params=_cp(("parallel", "arbitrary"), 32),
        name="swa_prompt",
    )(sinks, z, z, z, z, z, bias, ybuf)


def _swa_sample_kernel(q_ref, kc_ref, vc_ref, kn_ref, vn_ref, bias_ref, sink_ref, o_ref, wk_ref, wv_ref):
    scale = SWA_HEAD_DIM ** -0.5
    nbat, win, ng, hd = kc_ref.shape
    new_rows = kn_ref.shape[1]
    for bi in range(nbat):
        kk = jnp.concatenate([kc_ref[bi].reshape(win * ng, hd), kn_ref[bi]], axis=0)
        vv = jnp.concatenate([vc_ref[bi].reshape(win * ng, hd), vn_ref[bi]], axis=0)
        wk_ref[bi] = kk[new_rows:, :].reshape(win, ng, hd)
        wv_ref[bi] = vv[new_rows:, :].reshape(win, ng, hd)
        s = lax.dot_general(q_ref[bi].astype(BF16), kk.astype(BF16), (((1,), (1,)), ((), ())),
                            preferred_element_type=F32) * scale + bias_ref[...]
        o_ref[bi] = _softmax_sink_pv(s, sink_ref[...], vv.astype(BF16))


def swa_sample(q, k_new, v_new, cache_k, cache_v, layer, bias, sink_col, bb):
    nb, rows, hd = q.shape
    new_rows = k_new.shape[1]
    _, _, win, ng, _ = cache_k.shape
    nk = win * ng + new_rows
    cache_spec = pl.BlockSpec((None, bb, win, ng, hd), lambda b: (layer, b, 0, 0, 0))
    new_spec = pl.BlockSpec((bb, new_rows, hd), lambda b: (b, 0, 0))
    win_spec = pl.BlockSpec((bb, win, ng, hd), lambda b: (b, 0, 0, 0))
    qo_spec = pl.BlockSpec((bb, rows, hd), lambda b: (b, 0, 0))
    win_shape = jax.ShapeDtypeStruct((nb, win, ng, hd), F32)
    return pl.pallas_call(
        _swa_sample_kernel,
        grid=(nb // bb,),
        in_specs=[qo_spec, cache_spec, cache_spec, new_spec, new_spec,
                  pl.BlockSpec((rows, nk), lambda b: (0, 0)), pl.BlockSpec((rows, 1), lambda b: (0, 0))],
        out_specs=[qo_spec, win_spec, win_spec],
        out_shape=[jax.ShapeDtypeStruct(q.shape, F32), win_shape, win_shape],
        compiler_params=_cp(("parallel",), 32),
        name="swa_sample",
    )(q, cache_k, cache_v, k_new, v_new, bias, sink_col)


def _attend(q, mk, mv, allowed=None):
    s = lax.dot_general(q.astype(BF16), mk.astype(BF16), (((1,), (1,)), ((), ())),
                        preferred_element_type=F32) * (MEM_HEAD_DIM ** -0.5)
    if allowed is not None:
        s = jnp.where(allowed, s, NEG_INF)
    m = jnp.max(s, axis=-1, keepdims=True)
    p = jnp.exp(s - m)
    den = jnp.sum(p, axis=-1, keepdims=True)
    return jnp.dot(p.astype(BF16), mv.astype(BF16), preferred_element_type=F32) / den


def _mem_attn_kernel(q0_ref, q1_ref, q2_ref, q3_ref, mk_ref, mv_ref, o_ref):
    for h, q_ref in enumerate((q0_ref, q1_ref, q2_ref, q3_ref)):
        hs = slice(h * MEM_HEAD_DIM, (h + 1) * MEM_HEAD_DIM)
        o_ref[:, hs] = _attend(q_ref[...], mk_ref[:, hs], mv_ref[:, hs]).astype(o_ref.dtype)


def _mem_attn_sample_kernel(q_ref, mk_ref, mv_ref, o_ref):
    nbat, rows, hd = q_ref.shape
    nkeys = mk_ref.shape[1] * MEM_HEADS
    same_head = (lax.broadcasted_iota(jnp.int32, (rows, nkeys), 0) % MEM_HEADS
                 == lax.broadcasted_iota(jnp.int32, (rows, nkeys), 1) % MEM_HEADS)
    for bi in range(nbat):
        mk = mk_ref[bi].reshape(nkeys, hd)
        mv = mv_ref[bi].reshape(nkeys, hd)
        o_ref[bi] = _attend(q_ref[bi], mk, mv, same_head).astype(o_ref.dtype)


def mem_attn_prompt(z, nb, seq, kv, tq, ybuf):
    nt = seq // tq
    mlen = kv.shape[1]
    hd = MEM_HEAD_DIM
    q_specs = [pl.BlockSpec((tq, hd), functools.partial(lambda b, t, h: (b * nt + t, COL_XQ // hd + h), h=h))
               for h in range(MEM_HEADS)]
    return pl.pallas_call(
        _skip_ref(_mem_attn_kernel, 6),
        grid=(nb, nt),
        in_specs=q_specs + [pl.BlockSpec((None, mlen, BRANCH_DIM), lambda b, t: (b, 0, 0)),
                            pl.BlockSpec((None, mlen, BRANCH_DIM), lambda b, t: (b, 0, 1)),
                            pl.BlockSpec(memory_space=pl.ANY)],
        out_specs=pl.BlockSpec((None, tq, BRANCH_DIM), lambda b, t: (3, b * nt + t, 0)),
        out_shape=jax.ShapeDtypeStruct(ybuf.shape, ybuf.dtype),
        input_output_aliases={6: 0},
        compiler_params=_cp(("parallel", "arbitrary"), 32),
        name="mem_attn_prompt",
    )(z, z, z, z, kv, kv, ybuf)


def mem_attn_sample(q, cache_k, cache_v, layer, bb):
    nb, rows, hd = q.shape
    mlen = cache_k.shape[2]
    qo_spec = pl.BlockSpec((bb, rows, hd), lambda b: (b, 0, 0))
    cache_spec = pl.BlockSpec((None, bb, mlen, MEM_HEADS, hd), lambda b: (layer, b, 0, 0, 0))
    return pl.pallas_call(
        _mem_attn_sample_kernel,
        grid=(nb // bb,),
        in_specs=[qo_spec, cache_spec, cache_spec],
        out_specs=qo_spec,
        out_shape=jax.ShapeDtypeStruct(q.shape, BF16),
        compiler_params=_cp(("parallel",), 40),
        name="mem_attn_sample",
    )(q, cache_k, cache_v)


def _ssm_param_kernel(are_ref, aim_ref, ldt_ref, bre_ref, bim_ref, abr_ref, abi_ref, bbr_ref, bbi_ref):
    dt = jnp.exp(ldt_ref[...])
    ar, ai = are_ref[...], aim_ref[...]
    mag = jnp.exp(dt * ar)
    abr, abi = mag * jnp.cos(dt * ai), mag * jnp.sin(dt * ai)
    den = ar * ar + ai * ai
    nr, ni = abr - 1.0, abi
    fre, fim = (nr * ar + ni * ai) / den, (ni * ar - nr * ai) / den
    abr_ref[...] = abr
    abi_ref[...] = abi
    for c in range(SSM_GROUP):
        br, bi = bre_ref[c], bim_ref[c]
        bbr_ref[c] = fre * br - fim * bi
        bbi_ref[c] = fre * bi + fim * br


def ssm_params(a_re, a_im, log_dt, b_re_t, b_im_t):
    d, g, n = a_re.shape
    c = b_re_t.shape[1]
    gn = pl.BlockSpec((None, g, n), lambda l: (l, 0, 0))
    cgn = pl.BlockSpec((None, c, g, n), lambda l: (l, 0, 0, 0))
    return pl.pallas_call(
        _ssm_param_kernel,
        grid=(d,),
        in_specs=[gn, gn, pl.BlockSpec((None, g, 1), lambda l: (l, 0, 0)), cgn, cgn],
        out_specs=[gn, gn, cgn, cgn],
        out_shape=[jax.ShapeDtypeStruct((d, g, n), F32)] * 2 + [jax.ShapeDtypeStruct((d, c, g, n), F32)] * 2,
        compiler_params=_cp(("arbitrary",), 16),
        name="ssm_params",
    )(a_re, a_im, log_dt.reshape(d, g, 1), b_re_t, b_im_t)


def _cmul_add(ar, ai, hr, hi, br, bi):
    return ar * hr - ai * hi + br, ar * hi + ai * hr + bi


def _ssm_out(u, sre, sim, cre_ref, cim_ref, d_ref):
    y = (jnp.dot(sre[...].astype(BF16), cre_ref[...], preferred_element_type=F32)
         - jnp.dot(sim[...].astype(BF16), cim_ref[...], preferred_element_type=F32)
         + d_ref[...] * u)
    return jax.nn.gelu(y)


LANES = 128
PIECE_PITCH = 264


def _ssm_prompt_kernel(u_ref, bre_ref, bim_ref, cre_ref, cim_ref, ar_ref, ai_ref, d_ref,
                       y_ref, hr_ref, hi_ref, sre, sim, upad, uperm, ypad, *, clen):
    np_ = SCAN_LANES
    w = sre.shape[1]
    slabs = [slice(s * LANES, (s + 1) * LANES) for s in range(u_ref.shape[1] // LANES)]

    def rows(t):
        return pl.ds(pl.multiple_of(t * np_, np_), np_)

    def piece_rows(t):
        return pl.ds(t, np_, stride=PIECE_PITCH)

    for s, ls in enumerate(slabs):
        for p in range(np_):
            upad[s, p * PIECE_PITCH:p * PIECE_PITCH + clen, :] = u_ref[p * clen:(p + 1) * clen, ls]

    def gather(t, carry):
        for s, ls in enumerate(slabs):
            uperm[rows(t), ls] = upad[s, piece_rows(t), :]
        return carry

    lax.fori_loop(0, clen, gather, 0, unroll=SCAN_UNROLL)
    u = uperm[...]
    ub = u.astype(BF16)
    sre[...] = jnp.dot(ub, bre_ref[...], preferred_element_type=F32)
    sim[...] = jnp.dot(ub, bim_ref[...], preferred_element_type=F32)
    ar1, ai1 = ar_ref[...], ai_ref[...]
    ar = jnp.broadcast_to(ar1, (np_, w))
    ai = jnp.broadcast_to(ai1, (np_, w))

    def local_step(t, carry):
        return _cmul_add(ar, ai, carry[0], carry[1], sre[rows(t), :], sim[rows(t), :])

    zero = jnp.zeros((np_, w), F32)
    fr, fi = lax.fori_loop(0, clen, local_step, (zero, zero), unroll=SCAN_UNROLL)

    pr, pi = ar1, ai1
    for _ in range(int(math.log2(clen))):
        pr, pi = pr * pr - pi * pi, 2.0 * pr * pi
    row = lax.broadcasted_iota(jnp.int32, (np_, w), 0)
    cr = jnp.zeros((1, w), F32)
    ci = jnp.zeros((1, w), F32)
    hr0, hi0 = zero, zero
    for p in range(1, np_):
        cr, ci = _cmul_add(pr, pi, cr, ci, fr[p - 1:p], fi[p - 1:p])
        hr0 = jnp.where(row == p, cr, hr0)
        hi0 = jnp.where(row == p, ci, hi0)
    fin_r, fin_i = _cmul_add(pr, pi, cr, ci, fr[np_ - 1:np_], fi[np_ - 1:np_])
    hr_ref[...] = fin_r
    hi_ref[...] = fin_i

    def full_step(t, carry):
        nr, ni = _cmul_add(ar, ai, carry[0], carry[1], sre[rows(t), :], sim[rows(t), :])
        sre[rows(t), :] = nr
        sim[rows(t), :] = ni
        return nr, ni

    lax.fori_loop(0, clen, full_step, (hr0, hi0), unroll=SCAN_UNROLL)
    uperm[...] = _ssm_out(u, sre, sim, cre_ref, cim_ref, d_ref)

    def scatter(t, carry):
        for s, ls in enumerate(slabs):
            ypad[s, piece_rows(t), :] = uperm[rows(t), ls]
        return carry

    lax.fori_loop(0, clen, scatter, 0, unroll=SCAN_UNROLL)
    for s, ls in enumerate(slabs):
        for p in range(np_):
            y_ref[p * clen:(p + 1) * clen, ls] = ypad[s, p * PIECE_PITCH:p * PIECE_PITCH + clen, :]


def _ssm_specs(kdim):
    kmap = lambda *idx: (idx[kdim], 0, 0)
    return [pl.BlockSpec((None, SSM_UW, SSM_CW), kmap), pl.BlockSpec((None, SSM_UW, SSM_CW), kmap),
            pl.BlockSpec((None, SSM_CW, SSM_UW), kmap), pl.BlockSpec((None, SSM_CW, SSM_UW), kmap),
            pl.BlockSpec((None, 1, SSM_CW), kmap), pl.BlockSpec((None, 1, SSM_CW), kmap),
            pl.BlockSpec((None, 1, SSM_UW), kmap)]


def ssm_prompt(z, nb, seq, mats):
    clen = seq // SCAN_LANES
    nstate = SSM_GROUPS * SSM_STATE
    st = pl.BlockSpec((None, 1, SSM_CW), lambda b, k: (b, 0, k))
    pad_shape = (SSM_UW // LANES, SCAN_LANES * PIECE_PITCH, LANES)
    return pl.pallas_call(
        functools.partial(_ssm_prompt_kernel, clen=clen),
        grid=(nb, SSM_CHUNKS),
        in_specs=[pl.BlockSpec((seq, SSM_UW), lambda b, k: (b, COL_U // SSM_UW + k))] + _ssm_specs(1),
        out_specs=[pl.BlockSpec((seq, SSM_UW), lambda b, k: (b, k)), st, st],
        out_shape=[jax.ShapeDtypeStruct((nb * seq, BRANCH_DIM), F32),
                   jax.ShapeDtypeStruct((nb, 1, nstate), F32), jax.ShapeDtypeStruct((nb, 1, nstate), F32)],
        scratch_shapes=[pltpu.VMEM((seq, SSM_CW), F32), pltpu.VMEM((seq, SSM_CW), F32),
                        pltpu.VMEM(pad_shape, F32), pltpu.VMEM((seq, SSM_UW), F32), pltpu.VMEM(pad_shape, F32)],
        compiler_params=_cp(("parallel", "parallel"), 48),
        name="ssm_prompt",
    )(z, *mats)


def _ssm_sample_kernel(u_ref, bre_ref, bim_ref, cre_ref, cim_ref, ar_ref, ai_ref, d_ref, h0r_ref, h0i_ref,
                       y_ref, hr_ref, hi_ref, sre, sim, *, steps, nb):
    w = sre.shape[1]
    u = u_ref[...]
    ub = u.astype(BF16)
    bur = jnp.dot(ub, bre_ref[...], preferred_element_type=F32)
    bui = jnp.dot(ub, bim_ref[...], preferred_element_type=F32)
    ar = jnp.broadcast_to(ar_ref[...], (nb, w))
    ai = jnp.broadcast_to(ai_ref[...], (nb, w))
    hr, hi = h0r_ref[...], h0i_ref[...]
    for t in range(steps):
        rs = slice(t * nb, (t + 1) * nb)
        hr, hi = _cmul_add(ar, ai, hr, hi, bur[rs], bui[rs])
        sre[rs, :] = hr
        sim[rs, :] = hi
    hr_ref[...] = hr
    hi_ref[...] = hi
    y_ref[...] = _ssm_out(u, sre, sim, cre_ref, cim_ref, d_ref)


def ssm_sample(z, row0, steps, nb, mats, h0_re, h0_im, layer):
    rows = steps * nb
    nstate = SSM_GROUPS * SSM_STATE
    st_in = pl.BlockSpec((None, nb, SSM_CW), lambda k: (layer, 0, k))
    st_out = pl.BlockSpec((nb, SSM_CW), lambda k: (0, k))
    return pl.pallas_call(
        functools.partial(_ssm_sample_kernel, steps=steps, nb=nb),
        grid=(SSM_CHUNKS,),
        in_specs=[pl.BlockSpec((rows, SSM_UW), lambda k: (row0 // rows, COL_U // SSM_UW + k))]
                 + _ssm_specs(0) + [st_in, st_in],
        out_specs=[pl.BlockSpec((rows, SSM_UW), lambda k: (0, k)), st_out, st_out],
        out_shape=[jax.ShapeDtypeStruct((rows, BRANCH_DIM), F32),
                   jax.ShapeDtypeStruct((nb, nstate), F32), jax.ShapeDtypeStruct((nb, nstate), F32)],
        scratch_shapes=[pltpu.VMEM((rows, SSM_CW), F32), pltpu.VMEM((rows, SSM_CW), F32)],
        compiler_params=_cp(("parallel",), 32),
        name="ssm_sample",
    )(z, *mats, h0_re, h0_im)


def _block_diag(x, pattern):
    eye = jnp.eye(SSM_GROUPS // SSM_CHUNKS, dtype=x.dtype)
    return jnp.einsum(pattern, x, eye)


def kernel(x_prompt, x_sample, cache_conv, cache_win_k, cache_win_v, state_ssm_re, state_ssm_im, cache_mem_k, cache_mem_v, mem_prompt, t5_bias, norm_mix_pre, norm_mix_post, norm_ffn_pre, norm_ffn_post, norm_mem, w_in, conv_w, conv_b, conv_ln_g, conv_ln_b, attn_sinks, ssm_a_re, ssm_a_im, ssm_log_dt, ssm_b_re, ssm_b_im, ssm_c_re, ssm_c_im, ssm_d, ssm_w_glu, w_mem_kv, w_branch, w_out, w_ffn_in, w_ffn_out):
    bp, seq, d = x_prompt.shape
    bs, steps, _ = x_sample.shape
    rows_p = bp * seq
    rows_s = bs * steps
    rows = rows_p + rows_s
    mlen = mem_prompt.shape[1]
    kw = SWA_KV_HEADS * SWA_HEAD_DIM
    gpc = SSM_GROUPS // SSM_CHUNKS

    tm = rows // 5
    tm_merge = rows // 8
    tm_ffn_out = rows // 8
    tr = 320

    qi = np.arange(WINDOW)
    ki = np.arange(2 * WINDOW) - WINDOW
    dist_p = qi[:, None] - ki[None, :]
    bias_p = band_bias(t5_bias, dist_p, (dist_p >= 0) & (dist_p < WINDOW))
    bias_p = jnp.stack([jnp.where(jnp.asarray(ki < 0), NEG_INF, bias_p), bias_p])
    nk_s = WINDOW + steps
    dist_s = (WINDOW + np.arange(SUBLANES))[:, None] - np.arange(nk_s)[None, :]
    bias_s = band_bias(t5_bias, dist_s, (dist_s >= 0) & (dist_s < WINDOW))[:, :steps]
    bias_s = bias_s.reshape(SWA_KV_HEADS, SWA_REP, steps, nk_s).transpose(0, 2, 1, 3)
    same_group = jnp.eye(SWA_KV_HEADS, dtype=bool)[:, None, None, None, :]
    bias_s = jnp.where(same_group, bias_s[..., None], NEG_INF)
    bias_s = bias_s.reshape(SWA_HEADS * steps, nk_s * SWA_KV_HEADS)

    abar_re, abar_im, bbar_re, bbar_im = ssm_params(
        ssm_a_re, ssm_a_im, ssm_log_dt, ssm_b_re.transpose(0, 3, 1, 2), ssm_b_im.transpose(0, 3, 1, 2))

    def in_mat(x):
        x = x.reshape(DEPTH, SSM_GROUP, SSM_CHUNKS, gpc, SSM_STATE)
        return _block_diag(x, 'lckgn,gh->lkgchn').reshape(DEPTH, SSM_CHUNKS, SSM_UW, SSM_CW).astype(BF16)

    def out_mat(x):
        x = x.reshape(DEPTH, SSM_CHUNKS, gpc, SSM_GROUP, SSM_STATE)
        return _block_diag(x, 'lkgcn,gh->lkgnhc').reshape(DEPTH, SSM_CHUNKS, SSM_CW, SSM_UW).astype(BF16)

    bmat_re, bmat_im = in_mat(bbar_re), in_mat(bbar_im)
    cmat_re, cmat_im = out_mat(ssm_c_re), out_mat(ssm_c_im)
    abar_re = abar_re.reshape(DEPTH, SSM_CHUNKS, 1, SSM_CW)
    abar_im = abar_im.reshape(DEPTH, SSM_CHUNKS, 1, SSM_CW)
    dvec = ssm_d.reshape(DEPTH, SSM_CHUNKS, 1, SSM_UW)

    h0_re = state_ssm_re.reshape(DEPTH, bs, -1)
    h0_im = state_ssm_im.reshape(DEPTH, bs, -1)
    mem_rows = mem_prompt.reshape(bp * mlen, d)

    x_p0 = x_prompt.reshape(rows_p, d)
    x_s0 = x_sample.transpose(1, 0, 2).reshape(rows_s, d)
    x = None
    xn = rmsnorm_stacked(x_p0, x_s0, norm_mix_pre[0])

    ybuf = jnp.zeros((4, rows, BRANCH_DIM), BF16)
    outs = [[] for _ in range(12)]
    for l in range(DEPTH):
        z = matmul(xn, w_in, l, F32, tm, 512, n=COL_GATE)
        z_s = lax.slice(z, (rows_p, 0), (rows, COL_GATE))

        ybuf, conv_p = conv_prompt(z, bp, seq, conv_w[l], conv_b[l], conv_ln_g[l], conv_ln_b[l], 256, ybuf)
        ybuf, conv_s = conv_sample(z, rows_p, steps, bs, cache_conv, l, conv_w[l], conv_b[l],
                                   conv_ln_g[l], conv_ln_b[l], ybuf)

        ybuf = swa_prompt(z, bp, seq, attn_sinks[l], bias_p, ybuf)
        q_s = z_s[:, COL_Q:COL_K].reshape(steps, bs, SWA_KV_HEADS, SWA_REP, SWA_HEAD_DIM)
        q_s = q_s.transpose(1, 2, 0, 3, 4).reshape(bs, SWA_HEADS * steps, SWA_HEAD_DIM)
        k_s = z_s[:, COL_K:COL_V].reshape(steps, bs, SWA_KV_HEADS, SWA_HEAD_DIM).transpose(1, 0, 2, 3)
        v_s = z_s[:, COL_V:COL_U].reshape(steps, bs, SWA_KV_HEADS, SWA_HEAD_DIM).transpose(1, 0, 2, 3)
        k_s = k_s.reshape(bs, steps * SWA_KV_HEADS, SWA_HEAD_DIM)
        v_s = v_s.reshape(bs, steps * SWA_KV_HEADS, SWA_HEAD_DIM)
        sink_col = jnp.tile(attn_sinks[l].reshape(SWA_KV_HEADS, 1, SWA_REP), (1, steps, 1))
        sink_col = sink_col.reshape(SWA_HEADS * steps, 1)
        ob_s, wk_s, wv_s = swa_sample(q_s, k_s, v_s, cache_win_k, cache_win_v, l, bias_s, sink_col, 4)
        yb_s = ob_s.reshape(bs, SWA_KV_HEADS, steps, SWA_REP, SWA_HEAD_DIM).transpose(2, 0, 1, 3, 4)
        yb_s = yb_s.reshape(rows_s, BRANCH_DIM).astype(BF16)

        mats = (bmat_re[l], bmat_im[l], cmat_re[l], cmat_im[l], abar_re[l], abar_im[l], dvec[l])
        yg_p, hr_p, hi_p = ssm_prompt(z, bp, seq, mats)
        ybuf = glu(yg_p, ssm_w_glu, l, 1024, ybuf, 2, 0)
        yg_s, hr_s, hi_s = ssm_sample(z, rows_p, steps, bs, mats, h0_re, h0_im, l)
        ybuf = glu(yg_s, ssm_w_glu, l, rows_s, ybuf, 2, rows_p)

        kv = matmul(rmsnorm_bf16(mem_rows, norm_mem[l], 256), w_mem_kv, l, F32, bp * mlen, 512)
        kv = kv.reshape(bp, mlen, 2 * BRANCH_DIM)
        ybuf = mem_attn_prompt(z, bp, seq, kv, 512, ybuf)
        xq_s = z_s[:, COL_XQ:COL_GATE].reshape(steps, bs, MEM_HEADS, MEM_HEAD_DIM).transpose(1, 0, 2, 3)
        yx_s = mem_attn_sample(xq_s.reshape(bs, steps * MEM_HEADS, MEM_HEAD_DIM), cache_mem_k, cache_mem_v, l, 2)
        yx_s = yx_s.reshape(bs, steps, BRANCH_DIM).transpose(1, 0, 2).reshape(rows_s, BRANCH_DIM)

        ybuf = lax.dynamic_update_slice(ybuf, yb_s[None], (1, rows_p, 0))
        ybuf = lax.dynamic_update_slice(ybuf, yx_s[None], (3, rows_p, 0))
        merged = merge_branches(xn, ybuf, w_in, w_branch, l, tm_merge, 512)
        mix = matmul(merged, w_out, l, BF16, tm, 512)
        if l == 0:
            x, hn = resid_norm_stacked(x_p0, x_s0, mix, norm_mix_post[l], norm_ffn_pre[l])
        else:
            x, hn = resid_norm(x, mix, norm_mix_post[l], norm_ffn_pre[l], tr)

        hid = ffn_in(hn, w_ffn_in, l, tm, 256)
        f = matmul(hid, w_ffn_out, l, BF16, tm_ffn_out, 256, single_a=True)
        if l + 1 < DEPTH:
            x, xn = resid_norm(x, f, norm_ffn_post[l], norm_mix_pre[l + 1], tr)
        else:
            y_p, y_s = resid_split(x, f, norm_ffn_post[l], rows_s)

        kv_win = jnp.stack([lax.slice(z, ((b + 1) * seq - WINDOW, COL_K), ((b + 1) * seq, COL_U))
                            for b in range(bp)])
        new = (conv_p, conv_s,
               kv_win[..., :kw].reshape(bp, WINDOW, SWA_KV_HEADS, SWA_HEAD_DIM),
               kv_win[..., kw:].reshape(bp, WINDOW, SWA_KV_HEADS, SWA_HEAD_DIM),
               wk_s, wv_s,
               hr_p.reshape(bp, SSM_GROUPS, SSM_STATE), hi_p.reshape(bp, SSM_GROUPS, SSM_STATE),
               hr_s.reshape(bs, SSM_GROUPS, SSM_STATE), hi_s.reshape(bs, SSM_GROUPS, SSM_STATE),
               kv[..., :BRANCH_DIM].reshape(bp, mlen, MEM_HEADS, MEM_HEAD_DIM),
               kv[..., BRANCH_DIM:].reshape(bp, mlen, MEM_HEADS, MEM_HEAD_DIM))
        for acc, val in zip(outs, new):
            acc.append(val)

    y_prompt = y_p.reshape(bp, seq, d)
    y_sample = y_s.reshape(steps, bs, d).transpose(1, 0, 2)
    return (y_prompt, y_sample) + tuple(jnp.stack(o) for o in outs)
```

```python
import functools
import math

import jax
import jax.numpy as jnp
import numpy as np
from jax import lax
from jax.experimental import pallas as pl
from jax.experimental.pallas import tpu as pltpu

F32 = jnp.float32
BF16 = jnp.bfloat16

D_MODEL = 4096
DEPTH = 4
BRANCH_DIM = 1024
CONV_WIDTH = 31
SWA_HEAD_DIM = 64
SWA_HEADS = 16
SWA_KV_HEADS = 4
SWA_REP = 4
WINDOW = 128
N_BUCKETS = 32
SSM_GROUP = 16
SSM_GROUPS = 64
SSM_STATE = 64
MEM_HEADS = 4
MEM_HEAD_DIM = 256
D_FF = 11008
COL_Q = 2048
COL_K = 3072
COL_V = 3328
COL_U = 3584
COL_XQ = 4608
COL_GATE = 5632
N_IN = COL_GATE + 4 * D_MODEL
NEG_INF = -1e30

SWA_SCALE = SWA_HEAD_DIM ** -0.5
assert math.log2(SWA_HEAD_DIM) % 2 == 0

SSM_CHUNKS = 4
SSM_CW = SSM_GROUPS // SSM_CHUNKS * SSM_STATE
SSM_UW = SSM_GROUPS // SSM_CHUNKS * SSM_GROUP
SCAN_LANES = 8
SCAN_UNROLL = 8


def _cp(sem, vmem_mb):
    return pltpu.CompilerParams(dimension_semantics=sem, vmem_limit_bytes=vmem_mb << 20)


def _layer_rows(stacked, layer):
    arr = stacked if stacked.ndim == 3 else stacked.reshape(stacked.shape[0], 1, stacked.shape[1])
    return arr, pl.BlockSpec((None,) + arr.shape[1:], lambda *_: (layer, 0, 0))


def _rmsnorm_kernel(x_ref, g_ref, o_ref):
    x = x_ref[...]
    y = x * lax.rsqrt(jnp.mean(x * x, axis=-1, keepdims=True) + 1e-6)
    o_ref[...] = (y * g_ref[...]).astype(o_ref.dtype)


def rmsnorm_bf16(x, g, tr):
    m, d = x.shape
    g_arr, g_spec = _layer_rows(*g)
    return pl.pallas_call(
        _rmsnorm_kernel,
        grid=(m // tr,),
        in_specs=[pl.BlockSpec((tr, d), lambda i: (i, 0)), g_spec],
        out_specs=pl.BlockSpec((tr, d), lambda i: (i, 0)),
        out_shape=jax.ShapeDtypeStruct((m, d), BF16),
        compiler_params=_cp(("parallel",), 40),
        name="rmsnorm",
    )(x, g_arr)


def _stacked_rows(nfull, p_ref, s_ref):
    return jnp.where(pl.program_id(0) < nfull, p_ref[...], s_ref[...])


def _stacked_specs(tr, d, nfull):
    return [pl.BlockSpec((tr, d), lambda i: (jnp.minimum(i, nfull - 1), 0)), pl.BlockSpec((tr, d), lambda i: (0, 0))]


def _rmsnorm_stacked_kernel(xp_ref, xs_ref, g_ref, o_ref, *, nfull):
    x = _stacked_rows(nfull, xp_ref, xs_ref)
    y = x * lax.rsqrt(jnp.mean(x * x, axis=-1, keepdims=True) + 1e-6)
    o_ref[...] = (y * g_ref[...]).astype(o_ref.dtype)


def rmsnorm_stacked(x_p, x_s, g):
    tr, d = x_s.shape
    nfull = x_p.shape[0] // tr
    g_arr, g_spec = _layer_rows(*g)
    return pl.pallas_call(
        functools.partial(_rmsnorm_stacked_kernel, nfull=nfull),
        grid=(nfull + 1,),
        in_specs=_stacked_specs(tr, d, nfull) + [g_spec],
        out_specs=pl.BlockSpec((tr, d), lambda i: (i, 0)),
        out_shape=jax.ShapeDtypeStruct((x_p.shape[0] + tr, d), BF16),
        compiler_params=_cp(("parallel",), 40),
        name="rmsnorm_stacked",
    )(x_p, x_s, g_arr)


def _resid_norm_stacked_kernel(xp_ref, xs_ref, y_ref, gp_ref, gn_ref, xo_ref, hn_ref, *, nfull):
    y = y_ref[...].astype(F32)
    yn = y * lax.rsqrt(jnp.mean(y * y, axis=-1, keepdims=True) + 1e-6) * gp_ref[...]
    x = _stacked_rows(nfull, xp_ref, xs_ref) + yn
    xo_ref[...] = x
    h = x * lax.rsqrt(jnp.mean(x * x, axis=-1, keepdims=True) + 1e-6)
    hn_ref[...] = (h * gn_ref[...]).astype(hn_ref.dtype)


def resid_norm_stacked(x_p, x_s, y, g_post, g_next):
    tr, d = x_s.shape
    nfull = x_p.shape[0] // tr
    m = y.shape[0]
    row = pl.BlockSpec((tr, d), lambda i: (i, 0))
    gp_arr, gp_spec = _layer_rows(*g_post)
    gn_arr, gn_spec = _layer_rows(*g_next)
    return pl.pallas_call(
        functools.partial(_resid_norm_stacked_kernel, nfull=nfull),
        grid=(nfull + 1,),
        in_specs=_stacked_specs(tr, d, nfull) + [row, gp_spec, gn_spec],
        out_specs=[row, row],
        out_shape=[jax.ShapeDtypeStruct((m, d), F32), jax.ShapeDtypeStruct((m, d), BF16)],
        compiler_params=_cp(("parallel",), 48),
        name="resid_norm_stacked",
    )(x_p, x_s, y, gp_arr, gn_arr)


def _resid_split_kernel(x_ref, y_ref, gp_ref, op_ref, os_ref, *, nfull):
    y = y_ref[...].astype(F32)
    out = x_ref[...] + y * lax.rsqrt(jnp.mean(y * y, axis=-1, keepdims=True) + 1e-6) * gp_ref[...]

    @pl.when(pl.program_id(0) < nfull)
    def _():
        op_ref[...] = out

    @pl.when(pl.program_id(0) == nfull)
    def _():
        os_ref[...] = out


def resid_split(x, y, g_post, rows_s):
    m, d = x.shape
    tr = rows_s
    nfull = (m - rows_s) // tr
    row = pl.BlockSpec((tr, d), lambda i: (i, 0))
    gp_arr, gp_spec = _layer_rows(*g_post)
    return pl.pallas_call(
        functools.partial(_resid_split_kernel, nfull=nfull),
        grid=(nfull + 1,),
        in_specs=[row, row, gp_spec],
        out_specs=_stacked_specs(tr, d, nfull),
        out_shape=[jax.ShapeDtypeStruct((m - rows_s, d), F32), jax.ShapeDtypeStruct((rows_s, d), F32)],
        compiler_params=_cp(("arbitrary",), 48),
        name="resid_split",
    )(x, y, gp_arr)


def _resid_norm_kernel(x_ref, y_ref, gp_ref, gn_ref, xo_ref, hn_ref):
    y = y_ref[...].astype(F32)
    yn = y * lax.rsqrt(jnp.mean(y * y, axis=-1, keepdims=True) + 1e-6) * gp_ref[...]
    x = x_ref[...] + yn
    xo_ref[...] = x
    h = x * lax.rsqrt(jnp.mean(x * x, axis=-1, keepdims=True) + 1e-6)
    hn_ref[...] = (h * gn_ref[...]).astype(hn_ref.dtype)


def resid_norm(x, y, g_post, g_next, tr):
    m, d = x.shape
    row = pl.BlockSpec((tr, d), lambda i: (i, 0))
    gp_arr, gp_spec = _layer_rows(*g_post)
    gn_arr, gn_spec = _layer_rows(*g_next)
    return pl.pallas_call(
        _resid_norm_kernel,
        grid=(m // tr,),
        in_specs=[row, row, gp_spec, gn_spec],
        out_specs=[row, row],
        out_shape=[jax.ShapeDtypeStruct((m, d), F32), jax.ShapeDtypeStruct((m, d), BF16)],
        compiler_params=_cp(("parallel",), 48),
        name="resid_norm",
    )(x, y, gp_arr, gn_arr)


def _mm_kernel(a_ref, w_ref, o_ref):
    o_ref[...] = jnp.dot(a_ref[...], w_ref[...].astype(BF16), preferred_element_type=F32).astype(o_ref.dtype)


def matmul(a, w, layer, out_dtype, tm, tn, n=None, single_a=False):
    m, k = a.shape
    n = w.shape[-1] if n is None else n
    a_mode = dict(pipeline_mode=pl.Buffered(1)) if single_a else {}
    return pl.pallas_call(
        _mm_kernel,
        grid=(m // tm, n // tn),
        in_specs=[pl.BlockSpec((tm, k), lambda i, j: (i, 0), **a_mode),
                  pl.BlockSpec((None, k, tn), lambda i, j: (layer, 0, j))],
        out_specs=pl.BlockSpec((tm, tn), lambda i, j: (i, j)),
        out_shape=jax.ShapeDtypeStruct((m, n), out_dtype),
        compiler_params=_cp(("parallel", "parallel"), 56),
        name="matmul",
    )(a, w)


def _ffn_in_kernel(a_ref, wg_ref, wu_ref, o_ref):
    a = a_ref[...]
    g = jnp.dot(a, wg_ref[...].astype(BF16), preferred_element_type=F32)
    u = jnp.dot(a, wu_ref[...].astype(BF16), preferred_element_type=F32)
    o_ref[...] = (jax.nn.silu(g) * u).astype(o_ref.dtype)


def ffn_in(a, w, layer, tm, tn):
    m, k = a.shape
    nt = D_FF // tn
    return pl.pallas_call(
        _ffn_in_kernel,
        grid=(m // tm, nt),
        in_specs=[pl.BlockSpec((tm, k), lambda i, j: (i, 0)),
                  pl.BlockSpec((None, k, tn), lambda i, j: (layer, 0, j)),
                  pl.BlockSpec((None, k, tn), lambda i, j: (layer, 0, j + nt))],
        out_specs=pl.BlockSpec((tm, tn), lambda i, j: (i, j)),
        out_shape=jax.ShapeDtypeStruct((m, D_FF), BF16),
        compiler_params=_cp(("parallel", "parallel"), 56),
        name="ffn_in",
    )(a, w, w)


def _merge_kernel(xn_ref, y_ref, wg_ref, wb_ref, o_ref, acc):
    b = pl.program_id(2)
    @pl.when(b == 0)
    def _():
        acc[...] = jnp.zeros(acc.shape, F32)

    gate = jnp.dot(xn_ref[...], wg_ref[...].astype(BF16), preferred_element_type=F32)
    term = jax.nn.sigmoid(gate) * jnp.dot(y_ref[...], wb_ref[...].astype(BF16), preferred_element_type=F32)
    total = acc[...] + term
    acc[...] = total
    o_ref[...] = total.astype(o_ref.dtype)


def merge_branches(xn, y_all, w_in, w_branch, layer, tm, tn):
    nb, m, kb = y_all.shape
    k = xn.shape[1]
    gate0 = COL_GATE // tn
    per = D_MODEL // tn
    return pl.pallas_call(
        _merge_kernel,
        grid=(m // tm, per, nb),
        in_specs=[pl.BlockSpec((tm, k), lambda i, j, b: (i, 0)),
                  pl.BlockSpec((None, tm, kb), lambda i, j, b: (b, i, 0)),
                  pl.BlockSpec((None, k, tn), lambda i, j, b: (layer, 0, gate0 + b * per + j)),
                  pl.BlockSpec((None, None, kb, tn), lambda i, j, b: (layer, b, 0, j))],
        out_specs=pl.BlockSpec((tm, tn), lambda i, j, b: (i, j)),
        out_shape=jax.ShapeDtypeStruct((m, D_MODEL), BF16),
        scratch_shapes=[pltpu.VMEM((tm, tn), F32)],
        compiler_params=_cp(("parallel", "parallel", "arbitrary"), 56),
        name="merge",
    )(xn, y_all, w_in, w_branch)


def _skip_ref(body, pos):
    def wrapped(*refs):
        return body(*refs[:pos], *refs[pos + 1:])
    return wrapped


def _glu_kernel(y_ref, w_ref, o_ref):
    y = y_ref[...]
    s = jnp.dot(y.astype(BF16), w_ref[...].astype(BF16), preferred_element_type=F32)
    o_ref[...] = (y * jax.nn.sigmoid(s)).astype(o_ref.dtype)


def glu(y, w, layer, tr, ybuf=None, branch=0, row0=0):
    m, d = y.shape
    in_specs = [pl.BlockSpec((tr, d), lambda i: (i, 0)), pl.BlockSpec((None, d, d), lambda i: (layer, 0, 0))]
    if ybuf is None:
        return pl.pallas_call(
            _glu_kernel,
            grid=(m // tr,),
            in_specs=in_specs,
            out_specs=pl.BlockSpec((tr, d), lambda i: (i, 0)),
            out_shape=jax.ShapeDtypeStruct((m, d), BF16),
            compiler_params=_cp(("parallel",), 40),
            name="glu",
        )(y, w)
    return pl.pallas_call(
        _skip_ref(_glu_kernel, 2),
        grid=(m // tr,),
        in_specs=in_specs + [pl.BlockSpec(memory_space=pl.ANY)],
        out_specs=pl.BlockSpec((None, tr, d), lambda i: (branch, row0 // tr + i, 0)),
        out_shape=jax.ShapeDtypeStruct(ybuf.shape, ybuf.dtype),
        input_output_aliases={2: 0},
        compiler_params=_cp(("parallel",), 40),
        name="glu_into",
    )(y, w, ybuf)


def _ln_silu(y, g, b):
    mu = jnp.mean(y, axis=-1, keepdims=True)
    yc = y - mu
    yn = yc * lax.rsqrt(jnp.mean(yc * yc, axis=-1, keepdims=True) + 1e-5)
    return jax.nn.silu(yn * g + b)


CONV_HALO = 32
CONV_LANES = 128


SUBLANES = 8


def _conv_prompt_kernel(z_ref, w_ref, b_ref, g_ref, beta_ref, y_ref, cn_ref, xx, xs, acc, *, tt):
    c = CONV_DIM_
    off = CONV_HALO - (CONV_WIDTH - 1)
    span = tt + CONV_HALO - SUBLANES

    @pl.when(pl.program_id(1) == 0)
    def _():
        xx[0:CONV_HALO, :] = jnp.zeros((CONV_HALO, c), F32)

    xx[CONV_HALO:CONV_HALO + tt, :] = z_ref[:, :c] * jax.nn.sigmoid(z_ref[:, c:])
    for r in range(1, SUBLANES):
        xs[r - 1, 0:span, :] = xx[r:r + span, :]
    def lane_chunk(lc, carry):
        ls = pl.ds(pl.multiple_of(lc * CONV_LANES, CONV_LANES), CONV_LANES)
        a = None
        for w in range(CONV_WIDTH):
            q, r = divmod(off + w, SUBLANES)
            lo = SUBLANES * q
            src = xx[lo:lo + tt, ls] if r == 0 else xs[r - 1, lo:lo + tt, ls]
            term = src * w_ref[w:w + 1, ls]
            a = term if a is None else a + term
        acc[:, ls] = a
        return carry

    lax.fori_loop(0, c // CONV_LANES, lane_chunk, 0)
    y_ref[...] = _ln_silu(acc[...] + b_ref[...], g_ref[...], beta_ref[...]).astype(y_ref.dtype)
    cn_ref[...] = xx[tt + off:tt + CONV_HALO, :]
    xx[0:CONV_HALO, :] = xx[tt:tt + CONV_HALO, :]


CONV_DIM_ = BRANCH_DIM


def conv_prompt(z, nb, seq, params, layer, tt, ybuf):
    c = CONV_DIM_
    nt = seq // tt
    p_arrs, p_specs = zip(*[_layer_rows(p, layer) for p in params])
    return pl.pallas_call(
        _skip_ref(functools.partial(_conv_prompt_kernel, tt=tt), 5),
        grid=(nb, nt),
        in_specs=[pl.BlockSpec((tt, 2 * c), lambda bi, t: (bi * nt + t, 0)), *p_specs,
                  pl.BlockSpec(memory_space=pl.ANY)],
        out_specs=[pl.BlockSpec((None, tt, c), lambda bi, t: (0, bi * nt + t, 0)),
                   pl.BlockSpec((None, CONV_WIDTH - 1, c), lambda bi, t: (bi, 0, 0))],
        out_shape=[jax.ShapeDtypeStruct(ybuf.shape, ybuf.dtype),
                   jax.ShapeDtypeStruct((nb, CONV_WIDTH - 1, c), F32)],
        input_output_aliases={5: 0},
        scratch_shapes=[pltpu.VMEM((tt + CONV_HALO, c), F32),
                        pltpu.VMEM((SUBLANES - 1, tt + CONV_HALO - SUBLANES, c), F32),
                        pltpu.VMEM((tt, c), F32)],
        compiler_params=_cp(("parallel", "arbitrary"), 40),
        name="conv_prompt",
    )(z, *p_arrs, ybuf)


def _conv_sample_kernel(z_ref, cache_ref, w_ref, b_ref, g_ref, beta_ref, y_ref, cn_ref, past, *, steps, nb):
    c = CONV_DIM_
    hist = CONV_WIDTH - 1
    for r in range(hist):
        past[r] = cache_ref[:, r, :]
    a = z_ref[:, :c] * jax.nn.sigmoid(z_ref[:, c:])
    for t in range(steps):
        acc = None
        for w in range(CONV_WIDTH):
            idx = t + w
            src = past[idx] if idx < hist else a[(idx - hist) * nb:(idx - hist + 1) * nb]
            term = src * w_ref[w:w + 1, :]
            acc = term if acc is None else acc + term
        y_ref[t * nb:(t + 1) * nb, :] = _ln_silu(acc + b_ref[...], g_ref[...], beta_ref[...]).astype(y_ref.dtype)
    for r in range(hist - steps):
        cn_ref[:, r, :] = past[r + steps]
    for t in range(steps):
        cn_ref[:, hist - steps + t, :] = a[t * nb:(t + 1) * nb]


def conv_sample(z, row0, steps, nb, cache, params, layer, ybuf):
    c = CONV_DIM_
    rows = steps * nb
    hist = CONV_WIDTH - 1
    p_arrs, p_specs = zip(*[_layer_rows(p, layer) for p in params])
    return pl.pallas_call(
        _skip_ref(functools.partial(_conv_sample_kernel, steps=steps, nb=nb), 6),
        grid=(1,),
        in_specs=[pl.BlockSpec((rows, 2 * c), lambda i: (row0 // rows, 0)),
                  pl.BlockSpec((None, nb, hist, c), lambda i: (layer, 0, 0, 0)), *p_specs,
                  pl.BlockSpec(memory_space=pl.ANY)],
        out_specs=[pl.BlockSpec((None, rows, c), lambda i: (0, row0 // rows, 0)),
                   pl.BlockSpec((nb, hist, c), lambda i: (0, 0, 0))],
        out_shape=[jax.ShapeDtypeStruct(ybuf.shape, ybuf.dtype), jax.ShapeDtypeStruct((nb, hist, c), F32)],
        input_output_aliases={6: 0},
        scratch_shapes=[pltpu.VMEM((hist, nb, c), F32)],
        compiler_params=_cp(("arbitrary",), 40),
        name="conv_sample",
    )(z, cache, *p_arrs, ybuf)


def _t5_bucket_np(dist):
    n = np.maximum(dist, 0)
    max_exact = N_BUCKETS // 2
    nf = np.maximum(n, 1).astype(np.float32)
    large = max_exact + (np.log(nf / np.float32(max_exact)) / np.float32(math.log(WINDOW / max_exact))
                         * np.float32(N_BUCKETS - max_exact)).astype(np.int32)
    large = np.minimum(large, N_BUCKETS - 1)
    return np.where(n < max_exact, n, large)


def _bias_kernel(t5_ref, bucket_ref, o_ref):
    h = pl.program_id(0)
    bucket = bucket_ref[...]
    acc = jnp.full(bucket.shape, NEG_INF, F32)
    for b in range(N_BUCKETS):
        acc = jnp.where(bucket == b, t5_ref[b, h], acc)
    o_ref[...] = acc


def band_bias(t5_bias, dist, mask):
    bucket = np.where(mask, _t5_bucket_np(dist), -1).astype(np.int32)
    nq, nk = bucket.shape
    return pl.pallas_call(
        _bias_kernel,
        grid=(SWA_HEADS,),
        in_specs=[pl.BlockSpec(memory_space=pltpu.SMEM), pl.BlockSpec((nq, nk), lambda h: (0, 0))],
        out_specs=pl.BlockSpec((None, nq, nk), lambda h: (h, 0, 0)),
        out_shape=jax.ShapeDtypeStruct((SWA_HEADS, nq, nk), F32),
        compiler_params=_cp(("arbitrary",), 16),
        name="band_bias",
    )(t5_bias, jnp.asarray(bucket))


def _softmax_sink_pv(s, sink, v):
    m = jnp.maximum(jnp.max(s, axis=-1, keepdims=True), sink)
    p = jnp.exp(s - m)
    den = jnp.sum(p, axis=-1, keepdims=True) + jnp.exp(sink - m)
    return jnp.dot(p.astype(BF16), v, preferred_element_type=F32) / den


def _swa_prompt_kernel(sink_ref, q_ref, kp_ref, kc_ref, vp_ref, vc_ref, bias_ref, o_ref, *, layer):
    k = jnp.concatenate([kp_ref[...], kc_ref[...]], axis=0)
    v = jnp.concatenate([vp_ref[...], vc_ref[...]], axis=0)
    q = q_ref[...] * SWA_SCALE
    for hp in range(SWA_HEADS // 2):
        outs = []
        for h in (2 * hp, 2 * hp + 1):
            g = h // SWA_REP
            hs = slice(h * SWA_HEAD_DIM, (h + 1) * SWA_HEAD_DIM)
            gs = slice(g * SWA_HEAD_DIM, (g + 1) * SWA_HEAD_DIM)
            s = lax.dot_general(q[:, hs].astype(BF16), k[:, gs].astype(BF16), (((1,), (1,)), ((), ())),
                                preferred_element_type=F32) + bias_ref[h]
            outs.append(_softmax_sink_pv(s, sink_ref[layer, h], v[:, gs].astype(BF16)))
        o_ref[:, 2 * hp * SWA_HEAD_DIM:(2 * hp + 2) * SWA_HEAD_DIM] = (
            jnp.concatenate(outs, axis=1).astype(o_ref.dtype))


def swa_prompt(z, nb, seq, sinks, layer, bias, ybuf):
    blk = WINDOW
    nj = seq // blk
    qw = SWA_HEADS * SWA_HEAD_DIM
    kw = SWA_KV_HEADS * SWA_HEAD_DIM
    cur = lambda col: (lambda b, j: (b * nj + j, col))
    prev = lambda col: (lambda b, j: (b * nj + jnp.maximum(j - 1, 0), col))
    return pl.pallas_call(
        _skip_ref(functools.partial(_swa_prompt_kernel, layer=layer), 7),
        grid=(nb, nj),
        in_specs=[pl.BlockSpec(memory_space=pltpu.SMEM),
                  pl.BlockSpec((blk, qw), cur(COL_Q // qw)),
                  pl.BlockSpec((blk, kw), prev(COL_K // kw)), pl.BlockSpec((blk, kw), cur(COL_K // kw)),
                  pl.BlockSpec((blk, kw), prev(COL_V // kw)), pl.BlockSpec((blk, kw), cur(COL_V // kw)),
                  pl.BlockSpec((None, SWA_HEADS, blk, 2 * blk), lambda b, j: (jnp.minimum(j, 1), 0, 0, 0)),
                  pl.BlockSpec(memory_space=pl.ANY)],
        out_specs=pl.BlockSpec((None, blk, qw), lambda b, j: (1, b * nj + j, 0)),
        out_shape=jax.ShapeDtypeStruct(ybuf.shape, ybuf.dtype),
        input_output_aliases={7: 0},
        compiler_params=_cp(("parallel", "arbitrary"), 32),
        name="swa_prompt",
    )(sinks, z, z, z, z, z, bias, ybuf)


def _swa_sample_kernel(q_ref, kc_ref, vc_ref, kn_ref, vn_ref, bias_ref, sink_ref, o_ref, wk_ref, wv_ref):
    scale = SWA_HEAD_DIM ** -0.5
    nbat, win, ng, hd = kc_ref.shape
    new_rows = kn_ref.shape[1]
    for bi in range(nbat):
        kk = jnp.concatenate([kc_ref[bi].reshape(win * ng, hd), kn_ref[bi]], axis=0)
        vv = jnp.concatenate([vc_ref[bi].reshape(win * ng, hd), vn_ref[bi]], axis=0)
        wk_ref[bi] = kk[new_rows:, :].reshape(win, ng, hd)
        wv_ref[bi] = vv[new_rows:, :].reshape(win, ng, hd)
        s = lax.dot_general(q_ref[bi].astype(BF16), kk.astype(BF16), (((1,), (1,)), ((), ())),
                            preferred_element_type=F32) * scale + bias_ref[...]
        o_ref[bi] = _softmax_sink_pv(s, sink_ref[...], vv.astype(BF16))


def swa_sample(q, k_new, v_new, cache_k, cache_v, layer, bias, sink_col, bb):
    nb, rows, hd = q.shape
    new_rows = k_new.shape[1]
    _, _, win, ng, _ = cache_k.shape
    nk = win * ng + new_rows
    cache_spec = pl.BlockSpec((None, bb, win, ng, hd), lambda b: (layer, b, 0, 0, 0))
    new_spec = pl.BlockSpec((bb, new_rows, hd), lambda b: (b, 0, 0))
    win_spec = pl.BlockSpec((bb, win, ng, hd), lambda b: (b, 0, 0, 0))
    qo_spec = pl.BlockSpec((bb, rows, hd), lambda b: (b, 0, 0))
    win_shape = jax.ShapeDtypeStruct((nb, win, ng, hd), F32)
    return pl.pallas_call(
        _swa_sample_kernel,
        grid=(nb // bb,),
        in_specs=[qo_spec, cache_spec, cache_spec, new_spec, new_spec,
                  pl.BlockSpec((rows, nk), lambda b: (0, 0)), pl.BlockSpec((None, rows, 1), lambda b: (layer, 0, 0))],
        out_specs=[qo_spec, win_spec, win_spec],
        out_shape=[jax.ShapeDtypeStruct(q.shape, F32), win_shape, win_shape],
        compiler_params=_cp(("parallel",), 32),
        name="swa_sample",
    )(q, cache_k, cache_v, k_new, v_new, bias, sink_col)


def _attend(q, mk, mv, allowed=None):
    s = lax.dot_general(q.astype(BF16), mk.astype(BF16), (((1,), (1,)), ((), ())),
                        preferred_element_type=F32) * (MEM_HEAD_DIM ** -0.5)
    if allowed is not None:
        s = jnp.where(allowed, s, NEG_INF)
    m = jnp.max(s, axis=-1, keepdims=True)
    p = jnp.exp(s - m)
    den = jnp.sum(p, axis=-1, keepdims=True)
    return jnp.dot(p.astype(BF16), mv.astype(BF16), preferred_element_type=F32) / den


def _mem_attn_kernel(q0_ref, q1_ref, q2_ref, q3_ref, mk_ref, mv_ref, o_ref):
    for h, q_ref in enumerate((q0_ref, q1_ref, q2_ref, q3_ref)):
        hs = slice(h * MEM_HEAD_DIM, (h + 1) * MEM_HEAD_DIM)
        o_ref[:, hs] = _attend(q_ref[...], mk_ref[:, hs], mv_ref[:, hs]).astype(o_ref.dtype)


def _mem_attn_sample_kernel(q_ref, mk_ref, mv_ref, o_ref):
    nbat, rows, hd = q_ref.shape
    nkeys = mk_ref.shape[1] * MEM_HEADS
    same_head = (lax.broadcasted_iota(jnp.int32, (rows, nkeys), 0) % MEM_HEADS
                 == lax.broadcasted_iota(jnp.int32, (rows, nkeys), 1) % MEM_HEADS)
    for bi in range(nbat):
        mk = mk_ref[bi].reshape(nkeys, hd)
        mv = mv_ref[bi].reshape(nkeys, hd)
        o_ref[bi] = _attend(q_ref[bi], mk, mv, same_head).astype(o_ref.dtype)


def mem_attn_prompt(z, nb, seq, kv, tq, ybuf):
    nt = seq // tq
    mlen = kv.shape[1]
    hd = MEM_HEAD_DIM
    q_specs = [pl.BlockSpec((tq, hd), functools.partial(lambda b, t, h: (b * nt + t, COL_XQ // hd + h), h=h))
               for h in range(MEM_HEADS)]
    return pl.pallas_call(
        _skip_ref(_mem_attn_kernel, 6),
        grid=(nb, nt),
        in_specs=q_specs + [pl.BlockSpec((None, mlen, BRANCH_DIM), lambda b, t: (b, 0, 0)),
                            pl.BlockSpec((None, mlen, BRANCH_DIM), lambda b, t: (b, 0, 1)),
                            pl.BlockSpec(memory_space=pl.ANY)],
        out_specs=pl.BlockSpec((None, tq, BRANCH_DIM), lambda b, t: (3, b * nt + t, 0)),
        out_shape=jax.ShapeDtypeStruct(ybuf.shape, ybuf.dtype),
        input_output_aliases={6: 0},
        compiler_params=_cp(("parallel", "arbitrary"), 32),
        name="mem_attn_prompt",
    )(z, z, z, z, kv, kv, ybuf)


def mem_attn_sample(q, cache_k, cache_v, layer, bb):
    nb, rows, hd = q.shape
    mlen = cache_k.shape[2]
    qo_spec = pl.BlockSpec((bb, rows, hd), lambda b: (b, 0, 0))
    cache_spec = pl.BlockSpec((None, bb, mlen, MEM_HEADS, hd), lambda b: (layer, b, 0, 0, 0))
    return pl.pallas_call(
        _mem_attn_sample_kernel,
        grid=(nb // bb,),
        in_specs=[qo_spec, cache_spec, cache_spec],
        out_specs=qo_spec,
        out_shape=jax.ShapeDtypeStruct(q.shape, BF16),
        compiler_params=_cp(("parallel",), 40),
        name="mem_attn_sample",
    )(q, cache_k, cache_v)


def _ssm_param_kernel(are_ref, aim_ref, ldt_ref, bre_ref, bim_ref, abr_ref, abi_ref, bbr_ref, bbi_ref):
    dt = jnp.exp(ldt_ref[...])
    ar, ai = are_ref[...], aim_ref[...]
    mag = jnp.exp(dt * ar)
    abr, abi = mag * jnp.cos(dt * ai), mag * jnp.sin(dt * ai)
    den = ar * ar + ai * ai
    nr, ni = abr - 1.0, abi
    fre, fim = (nr * ar + ni * ai) / den, (ni * ar - nr * ai) / den
    abr_ref[...] = abr
    abi_ref[...] = abi
    for c in range(SSM_GROUP):
        br, bi = bre_ref[c], bim_ref[c]
        bbr_ref[c] = fre * br - fim * bi
        bbi_ref[c] = fre * bi + fim * br


def ssm_params(a_re, a_im, log_dt, b_re_t, b_im_t):
    d, g, n = a_re.shape
    c = b_re_t.shape[1]
    gn = pl.BlockSpec((None, g, n), lambda l: (l, 0, 0))
    cgn = pl.BlockSpec((None, c, g, n), lambda l: (l, 0, 0, 0))
    return pl.pallas_call(
        _ssm_param_kernel,
        grid=(d,),
        in_specs=[gn, gn, pl.BlockSpec((None, g, 1), lambda l: (l, 0, 0)), cgn, cgn],
        out_specs=[gn, gn, cgn, cgn],
        out_shape=[jax.ShapeDtypeStruct((d, g, n), F32)] * 2 + [jax.ShapeDtypeStruct((d, c, g, n), F32)] * 2,
        compiler_params=_cp(("arbitrary",), 16),
        name="ssm_params",
    )(a_re, a_im, log_dt.reshape(d, g, 1), b_re_t, b_im_t)


def _cmul_add(ar, ai, hr, hi, br, bi):
    return ar * hr - ai * hi + br, ar * hi + ai * hr + bi


def _ssm_out(u, sre, sim, cre_ref, cim_ref, d_ref):
    y = (jnp.dot(sre[...].astype(BF16), cre_ref[...], preferred_element_type=F32)
         - jnp.dot(sim[...].astype(BF16), cim_ref[...], preferred_element_type=F32)
         + d_ref[...] * u)
    return jax.nn.gelu(y)


LANES = 128
PIECE_PITCH = 264


def _ssm_prompt_kernel(u_ref, bre_ref, bim_ref, cre_ref, cim_ref, ar_ref, ai_ref, d_ref,
                       y_ref, hr_ref, hi_ref, sre, sim, upad, uperm, ypad, *, clen):
    np_ = SCAN_LANES
    w = sre.shape[1]
    slabs = [slice(s * LANES, (s + 1) * LANES) for s in range(u_ref.shape[1] // LANES)]

    def rows(t):
        return pl.ds(pl.multiple_of(t * np_, np_), np_)

    def piece_rows(t):
        return pl.ds(t, np_, stride=PIECE_PITCH)

    for s, ls in enumerate(slabs):
        for p in range(np_):
            upad[s, p * PIECE_PITCH:p * PIECE_PITCH + clen, :] = u_ref[p * clen:(p + 1) * clen, ls]

    def gather(t, carry):
        for s, ls in enumerate(slabs):
            uperm[rows(t), ls] = upad[s, piece_rows(t), :]
        return carry

    lax.fori_loop(0, clen, gather, 0, unroll=SCAN_UNROLL)
    u = uperm[...]
    ub = u.astype(BF16)
    sre[...] = jnp.dot(ub, bre_ref[...], preferred_element_type=F32)
    sim[...] = jnp.dot(ub, bim_ref[...], preferred_element_type=F32)
    ar1, ai1 = ar_ref[...], ai_ref[...]
    ar = jnp.broadcast_to(ar1, (np_, w))
    ai = jnp.broadcast_to(ai1, (np_, w))

    def local_step(t, carry):
        return _cmul_add(ar, ai, carry[0], carry[1], sre[rows(t), :], sim[rows(t), :])

    zero = jnp.zeros((np_, w), F32)
    fr, fi = lax.fori_loop(0, clen, local_step, (zero, zero), unroll=SCAN_UNROLL)

    pr, pi = ar1, ai1
    for _ in range(int(math.log2(clen))):
        pr, pi = pr * pr - pi * pi, 2.0 * pr * pi
    row = lax.broadcasted_iota(jnp.int32, (np_, w), 0)
    cr = jnp.zeros((1, w), F32)
    ci = jnp.zeros((1, w), F32)
    hr0, hi0 = zero, zero
    for p in range(1, np_):
        cr, ci = _cmul_add(pr, pi, cr, ci, fr[p - 1:p], fi[p - 1:p])
        hr0 = jnp.where(row == p, cr, hr0)
        hi0 = jnp.where(row == p, ci, hi0)
    fin_r, fin_i = _cmul_add(pr, pi, cr, ci, fr[np_ - 1:np_], fi[np_ - 1:np_])
    hr_ref[...] = fin_r
    hi_ref[...] = fin_i

    def full_step(t, carry):
        nr, ni = _cmul_add(ar, ai, carry[0], carry[1], sre[rows(t), :], sim[rows(t), :])
        sre[rows(t), :] = nr
        sim[rows(t), :] = ni
        return nr, ni

    lax.fori_loop(0, clen, full_step, (hr0, hi0), unroll=SCAN_UNROLL)
    uperm[...] = _ssm_out(u, sre, sim, cre_ref, cim_ref, d_ref)

    def scatter(t, carry):
        for s, ls in enumerate(slabs):
            ypad[s, piece_rows(t), :] = uperm[rows(t), ls]
        return carry

    lax.fori_loop(0, clen, scatter, 0, unroll=SCAN_UNROLL)
    for s, ls in enumerate(slabs):
        for p in range(np_):
            y_ref[p * clen:(p + 1) * clen, ls] = ypad[s, p * PIECE_PITCH:p * PIECE_PITCH + clen, :]


def _ssm_specs(layer, kdim):
    kmap = lambda *idx: (layer, idx[kdim], 0, 0)
    shapes = [(SSM_UW, SSM_CW), (SSM_UW, SSM_CW), (SSM_CW, SSM_UW), (SSM_CW, SSM_UW), (1, SSM_CW), (1, SSM_CW),
              (1, SSM_UW)]
    return [pl.BlockSpec((None, None) + s, kmap) for s in shapes]


def ssm_prompt(z, nb, seq, mats, layer):
    clen = seq // SCAN_LANES
    nstate = SSM_GROUPS * SSM_STATE
    st = pl.BlockSpec((None, 1, SSM_CW), lambda b, k: (b, 0, k))
    pad_shape = (SSM_UW // LANES, SCAN_LANES * PIECE_PITCH, LANES)
    return pl.pallas_call(
        functools.partial(_ssm_prompt_kernel, clen=clen),
        grid=(nb, SSM_CHUNKS),
        in_specs=[pl.BlockSpec((seq, SSM_UW), lambda b, k: (b, COL_U // SSM_UW + k))] + _ssm_specs(layer, 1),
        out_specs=[pl.BlockSpec((seq, SSM_UW), lambda b, k: (b, k)), st, st],
        out_shape=[jax.ShapeDtypeStruct((nb * seq, BRANCH_DIM), F32),
                   jax.ShapeDtypeStruct((nb, 1, nstate), F32), jax.ShapeDtypeStruct((nb, 1, nstate), F32)],
        scratch_shapes=[pltpu.VMEM((seq, SSM_CW), F32), pltpu.VMEM((seq, SSM_CW), F32),
                        pltpu.VMEM(pad_shape, F32), pltpu.VMEM((seq, SSM_UW), F32), pltpu.VMEM(pad_shape, F32)],
        compiler_params=_cp(("parallel", "parallel"), 48),
        name="ssm_prompt",
    )(z, *mats)


def _ssm_sample_kernel(u_ref, bre_ref, bim_ref, cre_ref, cim_ref, ar_ref, ai_ref, d_ref, h0r_ref, h0i_ref,
                       y_ref, hr_ref, hi_ref, sre, sim, *, steps, nb):
    w = sre.shape[1]
    u = u_ref[...]
    ub = u.astype(BF16)
    bur = jnp.dot(ub, bre_ref[...], preferred_element_type=F32)
    bui = jnp.dot(ub, bim_ref[...], preferred_element_type=F32)
    ar = jnp.broadcast_to(ar_ref[...], (nb, w))
    ai = jnp.broadcast_to(ai_ref[...], (nb, w))
    hr, hi = h0r_ref[...], h0i_ref[...]
    for t in range(steps):
        rs = slice(t * nb, (t + 1) * nb)
        hr, hi = _cmul_add(ar, ai, hr, hi, bur[rs], bui[rs])
        sre[rs, :] = hr
        sim[rs, :] = hi
    hr_ref[...] = hr
    hi_ref[...] = hi
    y_ref[...] = _ssm_out(u, sre, sim, cre_ref, cim_ref, d_ref)


def ssm_sample(z, row0, steps, nb, mats, h0_re, h0_im, layer):
    rows = steps * nb
    nstate = SSM_GROUPS * SSM_STATE
    st_in = pl.BlockSpec((None, nb, SSM_CW), lambda k: (layer, 0, k))
    st_out = pl.BlockSpec((nb, SSM_CW), lambda k: (0, k))
    return pl.pallas_call(
        functools.partial(_ssm_sample_kernel, steps=steps, nb=nb),
        grid=(SSM_CHUNKS,),
        in_specs=[pl.BlockSpec((rows, SSM_UW), lambda k: (row0 // rows, COL_U // SSM_UW + k))]
                 + _ssm_specs(layer, 0) + [st_in, st_in],
        out_specs=[pl.BlockSpec((rows, SSM_UW), lambda k: (0, k)), st_out, st_out],
        out_shape=[jax.ShapeDtypeStruct((rows, BRANCH_DIM), F32),
                   jax.ShapeDtypeStruct((nb, nstate), F32), jax.ShapeDtypeStruct((nb, nstate), F32)],
        scratch_shapes=[pltpu.VMEM((rows, SSM_CW), F32), pltpu.VMEM((rows, SSM_CW), F32)],
        compiler_params=_cp(("parallel",), 32),
        name="ssm_sample",
    )(z, *mats, h0_re, h0_im)


def _block_diag(blocks, nblk):
    rows, c = blocks.shape[-2:]
    tiled = jnp.tile(blocks, (1,) * (blocks.ndim - 1) + (nblk,))
    row_blk = lax.broadcasted_iota(jnp.int32, (rows, nblk * c), 0) // (rows // nblk)
    col_blk = lax.broadcasted_iota(jnp.int32, (rows, nblk * c), 1) // c
    return jnp.where(row_blk == col_blk, tiled, 0.0)


def kernel(x_prompt, x_sample, cache_conv, cache_win_k, cache_win_v, state_ssm_re, state_ssm_im, cache_mem_k, cache_mem_v, mem_prompt, t5_bias, norm_mix_pre, norm_mix_post, norm_ffn_pre, norm_ffn_post, norm_mem, w_in, conv_w, conv_b, conv_ln_g, conv_ln_b, attn_sinks, ssm_a_re, ssm_a_im, ssm_log_dt, ssm_b_re, ssm_b_im, ssm_c_re, ssm_c_im, ssm_d, ssm_w_glu, w_mem_kv, w_branch, w_out, w_ffn_in, w_ffn_out):
    bp, seq, d = x_prompt.shape
    bs, steps, _ = x_sample.shape
    rows_p = bp * seq
    rows_s = bs * steps
    rows = rows_p + rows_s
    mlen = mem_prompt.shape[1]
    kw = SWA_KV_HEADS * SWA_HEAD_DIM
    gpc = SSM_GROUPS // SSM_CHUNKS

    tm = rows // 5
    tm_merge = rows // 8
    tm_ffn_out = rows // 8
    tr = 320

    qi = np.arange(WINDOW)
    ki = np.arange(2 * WINDOW) - WINDOW
    dist_p = qi[:, None] - ki[None, :]
    bias_p = band_bias(t5_bias, dist_p, (dist_p >= 0) & (dist_p < WINDOW))
    bias_p = jnp.stack([jnp.where(jnp.asarray(ki < 0), NEG_INF, bias_p), bias_p])
    nk_s = WINDOW + steps
    dist_s = (WINDOW + np.arange(SUBLANES))[:, None] - np.arange(nk_s)[None, :]
    bias_s = band_bias(t5_bias, dist_s, (dist_s >= 0) & (dist_s < WINDOW))[:, :steps]
    bias_s = bias_s.reshape(SWA_KV_HEADS, SWA_REP, steps, nk_s).transpose(0, 2, 1, 3)
    same_group = jnp.eye(SWA_KV_HEADS, dtype=bool)[:, None, None, None, :]
    bias_s = jnp.where(same_group, bias_s[..., None], NEG_INF)
    bias_s = bias_s.reshape(SWA_HEADS * steps, nk_s * SWA_KV_HEADS)

    abar_re, abar_im, bbar_re, bbar_im = ssm_params(
        ssm_a_re, ssm_a_im, ssm_log_dt, ssm_b_re.transpose(0, 3, 1, 2), ssm_b_im.transpose(0, 3, 1, 2))

    def in_mat(x):
        x = x.reshape(DEPTH, SSM_GROUP, SSM_CHUNKS, gpc, SSM_STATE).transpose(0, 2, 3, 1, 4)
        return _block_diag(x.reshape(DEPTH, SSM_CHUNKS, SSM_UW, SSM_STATE), gpc).astype(BF16)

    def out_mat(x):
        x = x.reshape(DEPTH, SSM_CHUNKS, gpc, SSM_GROUP, SSM_STATE).transpose(0, 1, 2, 4, 3)
        return _block_diag(x.reshape(DEPTH, SSM_CHUNKS, SSM_CW, SSM_GROUP), gpc).astype(BF16)

    bmat_re, bmat_im = in_mat(bbar_re), in_mat(bbar_im)
    cmat_re, cmat_im = out_mat(ssm_c_re), out_mat(ssm_c_im)
    abar_re = abar_re.reshape(DEPTH, SSM_CHUNKS, 1, SSM_CW)
    abar_im = abar_im.reshape(DEPTH, SSM_CHUNKS, 1, SSM_CW)
    dvec = ssm_d.reshape(DEPTH, SSM_CHUNKS, 1, SSM_UW)

    h0_re = state_ssm_re.reshape(DEPTH, bs, -1)
    h0_im = state_ssm_im.reshape(DEPTH, bs, -1)
    mem_rows = mem_prompt.reshape(bp * mlen, d)

    x_p0 = x_prompt.reshape(rows_p, d)
    x_s0 = x_sample.transpose(1, 0, 2).reshape(rows_s, d)
    x = None
    xn = rmsnorm_stacked(x_p0, x_s0, (norm_mix_pre, 0))

    conv_params = (conv_w, conv_b, conv_ln_g, conv_ln_b)
    mats = (bmat_re, bmat_im, cmat_re, cmat_im, abar_re, abar_im, dvec)
    sink_cols = jnp.tile(attn_sinks.reshape(DEPTH, SWA_KV_HEADS, 1, SWA_REP), (1, 1, steps, 1))
    sink_cols = sink_cols.reshape(DEPTH, SWA_HEADS * steps, 1)

    ybuf = jnp.zeros((4, rows, BRANCH_DIM), BF16)
    outs = [[] for _ in range(12)]
    for l in range(DEPTH):
        z = matmul(xn, w_in, l, F32, tm, 512, n=COL_GATE)
        z_s = lax.slice(z, (rows_p, 0), (rows, COL_GATE))

        ybuf, conv_p = conv_prompt(z, bp, seq, conv_params, l, 256, ybuf)
        ybuf, conv_s = conv_sample(z, rows_p, steps, bs, cache_conv, conv_params, l, ybuf)

        ybuf = swa_prompt(z, bp, seq, attn_sinks, l, bias_p, ybuf)
        q_s = z_s[:, COL_Q:COL_K].reshape(steps, bs, SWA_KV_HEADS, SWA_REP, SWA_HEAD_DIM)
        q_s = q_s.transpose(1, 2, 0, 3, 4).reshape(bs, SWA_HEADS * steps, SWA_HEAD_DIM)
        k_s = z_s[:, COL_K:COL_V].reshape(steps, bs, SWA_KV_HEADS, SWA_HEAD_DIM).transpose(1, 0, 2, 3)
        v_s = z_s[:, COL_V:COL_U].reshape(steps, bs, SWA_KV_HEADS, SWA_HEAD_DIM).transpose(1, 0, 2, 3)
        k_s = k_s.reshape(bs, steps * SWA_KV_HEADS, SWA_HEAD_DIM)
        v_s = v_s.reshape(bs, steps * SWA_KV_HEADS, SWA_HEAD_DIM)
        ob_s, wk_s, wv_s = swa_sample(q_s, k_s, v_s, cache_win_k, cache_win_v, l, bias_s, sink_cols, 4)
        yb_s = ob_s.reshape(bs, SWA_KV_HEADS, steps, SWA_REP, SWA_HEAD_DIM).transpose(2, 0, 1, 3, 4)
        yb_s = yb_s.reshape(rows_s, BRANCH_DIM).astype(BF16)

        yg_p, hr_p, hi_p = ssm_prompt(z, bp, seq, mats, l)
        ybuf = glu(yg_p, ssm_w_glu, l, 1024, ybuf, 2, 0)
        yg_s, hr_s, hi_s = ssm_sample(z, rows_p, steps, bs, mats, h0_re, h0_im, l)
        ybuf = glu(yg_s, ssm_w_glu, l, rows_s, ybuf, 2, rows_p)

        kv = matmul(rmsnorm_bf16(mem_rows, (norm_mem, l), 256), w_mem_kv, l, F32, bp * mlen, 512)
        kv = kv.reshape(bp, mlen, 2 * BRANCH_DIM)
        ybuf = mem_attn_prompt(z, bp, seq, kv, 512, ybuf)
        xq_s = z_s[:, COL_XQ:COL_GATE].reshape(steps, bs, MEM_HEADS, MEM_HEAD_DIM).transpose(1, 0, 2, 3)
        yx_s = mem_attn_sample(xq_s.reshape(bs, steps * MEM_HEADS, MEM_HEAD_DIM), cache_mem_k, cache_mem_v, l, 2)
        yx_s = yx_s.reshape(bs, steps, BRANCH_DIM).transpose(1, 0, 2).reshape(rows_s, BRANCH_DIM)

        ybuf = lax.dynamic_update_slice(ybuf, yb_s[None], (1, rows_p, 0))
        ybuf = lax.dynamic_update_slice(ybuf, yx_s[None], (3, rows_p, 0))
        merged = merge_branches(xn, ybuf, w_in, w_branch, l, tm_merge, 512)
        mix = matmul(merged, w_out, l, BF16, tm, 512)
        if l == 0:
            x, hn = resid_norm_stacked(x_p0, x_s0, mix, (norm_mix_post, l), (norm_ffn_pre, l))
        else:
            x, hn = resid_norm(x, mix, (norm_mix_post, l), (norm_ffn_pre, l), tr)

        hid = ffn_in(hn, w_ffn_in, l, tm, 256)
        f = matmul(hid, w_ffn_out, l, BF16, tm_ffn_out, 256, single_a=True)
        if l + 1 < DEPTH:
            x, xn = resid_norm(x, f, (norm_ffn_post, l), (norm_mix_pre, l + 1), tr)
        else:
            y_p, y_s = resid_split(x, f, (norm_ffn_post, l), rows_s)

        kv_win = jnp.stack([lax.slice(z, ((b + 1) * seq - WINDOW, COL_K), ((b + 1) * seq, COL_U))
                            for b in range(bp)])
        new = (conv_p, conv_s,
               kv_win[..., :kw].reshape(bp, WINDOW, SWA_KV_HEADS, SWA_HEAD_DIM),
               kv_win[..., kw:].reshape(bp, WINDOW, SWA_KV_HEADS, SWA_HEAD_DIM),
               wk_s, wv_s,
               hr_p.reshape(bp, SSM_GROUPS, SSM_STATE), hi_p.reshape(bp, SSM_GROUPS, SSM_STATE),
               hr_s.reshape(bs, SSM_GROUPS, SSM_STATE), hi_s.reshape(bs, SSM_GROUPS, SSM_STATE),
               kv[..., :BRANCH_DIM].reshape(bp, mlen, MEM_HEADS, MEM_HEAD_DIM),
               kv[..., BRANCH_DIM:].reshape(bp, mlen, MEM_HEADS, MEM_HEAD_DIM))
        for acc, val in zip(outs, new):
            acc.append(val)

    y_prompt = y_p.reshape(bp, seq, d)
    y_sample = y_s.reshape(steps, bs, d).transpose(1, 0, 2)
    return (y_prompt, y_sample) + tuple(jnp.stack(o) for o in outs)
```

```python
import functools
import math

import jax
import jax.numpy as jnp
import numpy as np
from jax import lax
from jax.experimental import pallas as pl
from jax.experimental.pallas import tpu as pltpu

F32 = jnp.float32
BF16 = jnp.bfloat16

D_MODEL = 4096
DEPTH = 4
BRANCH_DIM = 1024
CONV_WIDTH = 31
SWA_HEAD_DIM = 64
SWA_HEADS = 16
SWA_KV_HEADS = 4
SWA_REP = 4
WINDOW = 128
N_BUCKETS = 32
SSM_GROUP = 16
SSM_GROUPS = 64
SSM_STATE = 64
MEM_HEADS = 4
MEM_HEAD_DIM = 256
D_FF = 11008
COL_Q = 2048
COL_K = 3072
COL_V = 3328
COL_U = 3584
COL_XQ = 4608
COL_GATE = 5632
NEG_INF = -1e30

SWA_SCALE = SWA_HEAD_DIM ** -0.5
assert math.log2(SWA_HEAD_DIM) % 2 == 0

SSM_CHUNKS = 4
SSM_CW = SSM_GROUPS // SSM_CHUNKS * SSM_STATE
SSM_UW = SSM_GROUPS // SSM_CHUNKS * SSM_GROUP
SCAN_LANES = 8
SCAN_UNROLL = 16


def _cp(sem, vmem_mb):
    return pltpu.CompilerParams(dimension_semantics=sem, vmem_limit_bytes=vmem_mb << 20)


def _layer_rows(stacked, layer):
    arr = stacked if stacked.ndim == 3 else stacked.reshape(stacked.shape[0], 1, stacked.shape[1])
    return arr, pl.BlockSpec((None,) + arr.shape[1:], lambda *_: (layer, 0, 0))


def _rmsnorm_kernel(x_ref, g_ref, o_ref):
    x = x_ref[...]
    y = x * lax.rsqrt(jnp.mean(x * x, axis=-1, keepdims=True) + 1e-6)
    o_ref[...] = (y * g_ref[...]).astype(o_ref.dtype)


def rmsnorm_bf16(x, g, tr):
    m, d = x.shape
    g_arr, g_spec = _layer_rows(*g)
    return pl.pallas_call(
        _rmsnorm_kernel,
        grid=(m // tr,),
        in_specs=[pl.BlockSpec((tr, d), lambda i: (i, 0)), g_spec],
        out_specs=pl.BlockSpec((tr, d), lambda i: (i, 0)),
        out_shape=jax.ShapeDtypeStruct((m, d), BF16),
        compiler_params=_cp(("parallel",), 40),
        name="rmsnorm",
    )(x, g_arr)


def _stacked_rows(nfull, p_ref, s_ref):
    return jnp.where(pl.program_id(0) < nfull, p_ref[...], s_ref[...])


def _stacked_specs(tr, d, nfull):
    return [pl.BlockSpec((tr, d), lambda i: (jnp.minimum(i, nfull - 1), 0)), pl.BlockSpec((tr, d), lambda i: (0, 0))]


def _rmsnorm_stacked_kernel(xp_ref, xs_ref, g_ref, o_ref, *, nfull):
    x = _stacked_rows(nfull, xp_ref, xs_ref)
    y = x * lax.rsqrt(jnp.mean(x * x, axis=-1, keepdims=True) + 1e-6)
    o_ref[...] = (y * g_ref[...]).astype(o_ref.dtype)


def rmsnorm_stacked(x_p, x_s, g):
    tr, d = x_s.shape
    nfull = x_p.shape[0] // tr
    g_arr, g_spec = _layer_rows(*g)
    return pl.pallas_call(
        functools.partial(_rmsnorm_stacked_kernel, nfull=nfull),
        grid=(nfull + 1,),
        in_specs=_stacked_specs(tr, d, nfull) + [g_spec],
        out_specs=pl.BlockSpec((tr, d), lambda i: (i, 0)),
        out_shape=jax.ShapeDtypeStruct((x_p.shape[0] + tr, d), BF16),
        compiler_params=_cp(("parallel",), 40),
        name="rmsnorm_stacked",
    )(x_p, x_s, g_arr)


def _resid_norm_stacked_kernel(xp_ref, xs_ref, y_ref, gp_ref, gn_ref, xo_ref, hn_ref, *, nfull):
    y = y_ref[...].astype(F32)
    yn = y * lax.rsqrt(jnp.mean(y * y, axis=-1, keepdims=True) + 1e-6) * gp_ref[...]
    x = _stacked_rows(nfull, xp_ref, xs_ref) + yn
    xo_ref[...] = x
    h = x * lax.rsqrt(jnp.mean(x * x, axis=-1, keepdims=True) + 1e-6)
    hn_ref[...] = (h * gn_ref[...]).astype(hn_ref.dtype)


def resid_norm_stacked(x_p, x_s, y, g_post, g_next):
    tr, d = x_s.shape
    nfull = x_p.shape[0] // tr
    m = y.shape[0]
    row = pl.BlockSpec((tr, d), lambda i: (i, 0))
    gp_arr, gp_spec = _layer_rows(*g_post)
    gn_arr, gn_spec = _layer_rows(*g_next)
    return pl.pallas_call(
        functools.partial(_resid_norm_stacked_kernel, nfull=nfull),
        grid=(nfull + 1,),
        in_specs=_stacked_specs(tr, d, nfull) + [row, gp_spec, gn_spec],
        out_specs=[row, row],
        out_shape=[jax.ShapeDtypeStruct((m, d), F32), jax.ShapeDtypeStruct((m, d), BF16)],
        compiler_params=_cp(("parallel",), 48),
        name="resid_norm_stacked",
    )(x_p, x_s, y, gp_arr, gn_arr)


def _resid_split_kernel(x_ref, y_ref, gp_ref, op_ref, os_ref, *, nfull):
    y = y_ref[...].astype(F32)
    out = x_ref[...] + y * lax.rsqrt(jnp.mean(y * y, axis=-1, keepdims=True) + 1e-6) * gp_ref[...]

    @pl.when(pl.program_id(0) < nfull)
    def _():
        op_ref[...] = out

    @pl.when(pl.program_id(0) == nfull)
    def _():
        os_ref[...] = out


def resid_split(x, y, g_post, rows_s):
    m, d = x.shape
    tr = rows_s
    nfull = (m - rows_s) // tr
    row = pl.BlockSpec((tr, d), lambda i: (i, 0))
    gp_arr, gp_spec = _layer_rows(*g_post)
    return pl.pallas_call(
        functools.partial(_resid_split_kernel, nfull=nfull),
        grid=(nfull + 1,),
        in_specs=[row, row, gp_spec],
        out_specs=_stacked_specs(tr, d, nfull),
        out_shape=[jax.ShapeDtypeStruct((m - rows_s, d), F32), jax.ShapeDtypeStruct((rows_s, d), F32)],
        compiler_params=_cp(("arbitrary",), 48),
        name="resid_split",
    )(x, y, gp_arr)


def _resid_norm_kernel(x_ref, y_ref, gp_ref, gn_ref, xo_ref, hn_ref):
    y = y_ref[...].astype(F32)
    yn = y * lax.rsqrt(jnp.mean(y * y, axis=-1, keepdims=True) + 1e-6) * gp_ref[...]
    x = x_ref[...] + yn
    xo_ref[...] = x
    h = x * lax.rsqrt(jnp.mean(x * x, axis=-1, keepdims=True) + 1e-6)
    hn_ref[...] = (h * gn_ref[...]).astype(hn_ref.dtype)


def resid_norm(x, y, g_post, g_next, tr):
    m, d = x.shape
    row = pl.BlockSpec((tr, d), lambda i: (i, 0))
    gp_arr, gp_spec = _layer_rows(*g_post)
    gn_arr, gn_spec = _layer_rows(*g_next)
    return pl.pallas_call(
        _resid_norm_kernel,
        grid=(m // tr,),
        in_specs=[row, row, gp_spec, gn_spec],
        out_specs=[row, row],
        out_shape=[jax.ShapeDtypeStruct((m, d), F32), jax.ShapeDtypeStruct((m, d), BF16)],
        compiler_params=_cp(("parallel",), 48),
        name="resid_norm",
    )(x, y, gp_arr, gn_arr)


def _mm_kernel(a_ref, w_ref, o_ref):
    o_ref[...] = jnp.dot(a_ref[...], w_ref[...].astype(BF16), preferred_element_type=F32).astype(o_ref.dtype)


def matmul(a, w, layer, out_dtype, tm, tn, n=None, single_a=False):
    m, k = a.shape
    n = w.shape[-1] if n is None else n
    a_mode = dict(pipeline_mode=pl.Buffered(1)) if single_a else {}
    return pl.pallas_call(
        _mm_kernel,
        grid=(m // tm, n // tn),
        in_specs=[pl.BlockSpec((tm, k), lambda i, j: (i, 0), **a_mode),
                  pl.BlockSpec((None, k, tn), lambda i, j: (layer, 0, j))],
        out_specs=pl.BlockSpec((tm, tn), lambda i, j: (i, j)),
        out_shape=jax.ShapeDtypeStruct((m, n), out_dtype),
        compiler_params=_cp(("parallel", "parallel"), 56),
        name="matmul",
    )(a, w)


def _ffn_in_kernel(a_ref, wg_ref, wu_ref, o_ref):
    a = a_ref[...]
    g = jnp.dot(a, wg_ref[...].astype(BF16), preferred_element_type=F32)
    u = jnp.dot(a, wu_ref[...].astype(BF16), preferred_element_type=F32)
    o_ref[...] = (jax.nn.silu(g) * u).astype(o_ref.dtype)


def ffn_in(a, w, layer, tm, tn):
    m, k = a.shape
    nt = D_FF // tn
    return pl.pallas_call(
        _ffn_in_kernel,
        grid=(m // tm, nt),
        in_specs=[pl.BlockSpec((tm, k), lambda i, j: (i, 0)),
                  pl.BlockSpec((None, k, tn), lambda i, j: (layer, 0, j)),
                  pl.BlockSpec((None, k, tn), lambda i, j: (layer, 0, j + nt))],
        out_specs=pl.BlockSpec((tm, tn), lambda i, j: (i, j)),
        out_shape=jax.ShapeDtypeStruct((m, D_FF), BF16),
        compiler_params=_cp(("parallel", "parallel"), 56),
        name="ffn_in",
    )(a, w, w)


def _merge_kernel(xn_ref, y_ref, wg_ref, wb_ref, o_ref, acc):
    b = pl.program_id(2)
    @pl.when(b == 0)
    def _():
        acc[...] = jnp.zeros(acc.shape, F32)

    gate = jnp.dot(xn_ref[...], wg_ref[...].astype(BF16), preferred_element_type=F32)
    term = jax.nn.sigmoid(gate) * jnp.dot(y_ref[...], wb_ref[...].astype(BF16), preferred_element_type=F32)
    total = acc[...] + term
    acc[...] = total
    o_ref[...] = total.astype(o_ref.dtype)


def merge_branches(xn, y_all, w_in, w_branch, layer, tm, tn):
    nb, m, kb = y_all.shape
    k = xn.shape[1]
    gate0 = COL_GATE // tn
    per = D_MODEL // tn
    return pl.pallas_call(
        _merge_kernel,
        grid=(m // tm, per, nb),
        in_specs=[pl.BlockSpec((tm, k), lambda i, j, b: (i, 0)),
                  pl.BlockSpec((None, tm, kb), lambda i, j, b: (b, i, 0)),
                  pl.BlockSpec((None, k, tn), lambda i, j, b: (layer, 0, gate0 + b * per + j)),
                  pl.BlockSpec((None, None, kb, tn), lambda i, j, b: (layer, b, 0, j))],
        out_specs=pl.BlockSpec((tm, tn), lambda i, j, b: (i, j)),
        out_shape=jax.ShapeDtypeStruct((m, D_MODEL), BF16),
        scratch_shapes=[pltpu.VMEM((tm, tn), F32)],
        compiler_params=_cp(("parallel", "parallel", "arbitrary"), 56),
        name="merge",
    )(xn, y_all, w_in, w_branch)


def _skip_ref(body, pos):
    def wrapped(*refs):
        return body(*refs[:pos], *refs[pos + 1:])
    return wrapped


def _glu_kernel(y_ref, w_ref, o_ref):
    y = y_ref[...]
    s = jnp.dot(y.astype(BF16), w_ref[...].astype(BF16), preferred_element_type=F32)
    o_ref[...] = (y * jax.nn.sigmoid(s)).astype(o_ref.dtype)


def glu(y, w, layer, tr, ybuf=None, branch=0, row0=0):
    m, d = y.shape
    in_specs = [pl.BlockSpec((tr, d), lambda i: (i, 0)), pl.BlockSpec((None, d, d), lambda i: (layer, 0, 0))]
    if ybuf is None:
        return pl.pallas_call(
            _glu_kernel,
            grid=(m // tr,),
            in_specs=in_specs,
            out_specs=pl.BlockSpec((tr, d), lambda i: (i, 0)),
            out_shape=jax.ShapeDtypeStruct((m, d), BF16),
            compiler_params=_cp(("parallel",), 40),
            name="glu",
        )(y, w)
    return pl.pallas_call(
        _skip_ref(_glu_kernel, 2),
        grid=(m // tr,),
        in_specs=in_specs + [pl.BlockSpec(memory_space=pl.ANY)],
        out_specs=pl.BlockSpec((None, tr, d), lambda i: (branch, row0 // tr + i, 0)),
        out_shape=jax.ShapeDtypeStruct(ybuf.shape, ybuf.dtype),
        input_output_aliases={2: 0},
        compiler_params=_cp(("parallel",), 40),
        name="glu_into",
    )(y, w, ybuf)


def _ln_silu(y, g, b):
    mu = jnp.mean(y, axis=-1, keepdims=True)
    yc = y - mu
    yn = yc * lax.rsqrt(jnp.mean(yc * yc, axis=-1, keepdims=True) + 1e-5)
    return jax.nn.silu(yn * g + b)


CONV_HALO = 32
CONV_LANES = 128


SUBLANES = 8


def _conv_prompt_kernel(z_ref, w_ref, b_ref, g_ref, beta_ref, y_ref, cn_ref, xx, xs, acc, *, tt):
    c = CONV_DIM
    off = CONV_HALO - (CONV_WIDTH - 1)
    span = tt + CONV_HALO - SUBLANES

    @pl.when(pl.program_id(1) == 0)
    def _():
        xx[0:CONV_HALO, :] = jnp.zeros((CONV_HALO, c), F32)

    xx[CONV_HALO:CONV_HALO + tt, :] = z_ref[:, :c] * jax.nn.sigmoid(z_ref[:, c:])
    for r in range(1, SUBLANES):
        xs[r - 1, 0:span, :] = xx[r:r + span, :]
    def lane_chunk(lc, carry):
        ls = pl.ds(pl.multiple_of(lc * CONV_LANES, CONV_LANES), CONV_LANES)
        a = None
        for w in range(CONV_WIDTH):
            q, r = divmod(off + w, SUBLANES)
            lo = SUBLANES * q
            src = xx[lo:lo + tt, ls] if r == 0 else xs[r - 1, lo:lo + tt, ls]
            term = src * w_ref[w:w + 1, ls]
            a = term if a is None else a + term
        acc[:, ls] = a
        return carry

    lax.fori_loop(0, c // CONV_LANES, lane_chunk, 0)
    y_ref[...] = _ln_silu(acc[...] + b_ref[...], g_ref[...], beta_ref[...]).astype(y_ref.dtype)
    cn_ref[...] = xx[tt + off:tt + CONV_HALO, :]
    xx[0:CONV_HALO, :] = xx[tt:tt + CONV_HALO, :]


CONV_DIM = BRANCH_DIM


def conv_prompt(z, nb, seq, params, layer, tt, ybuf):
    c = CONV_DIM
    nt = seq // tt
    p_arrs, p_specs = zip(*[_layer_rows(p, layer) for p in params])
    return pl.pallas_call(
        _skip_ref(functools.partial(_conv_prompt_kernel, tt=tt), 5),
        grid=(nb, nt),
        in_specs=[pl.BlockSpec((tt, 2 * c), lambda bi, t: (bi * nt + t, 0)), *p_specs,
                  pl.BlockSpec(memory_space=pl.ANY)],
        out_specs=[pl.BlockSpec((None, tt, c), lambda bi, t: (0, bi * nt + t, 0)),
                   pl.BlockSpec((None, CONV_WIDTH - 1, c), lambda bi, t: (bi, 0, 0))],
        out_shape=[jax.ShapeDtypeStruct(ybuf.shape, ybuf.dtype),
                   jax.ShapeDtypeStruct((nb, CONV_WIDTH - 1, c), F32)],
        input_output_aliases={5: 0},
        scratch_shapes=[pltpu.VMEM((tt + CONV_HALO, c), F32),
                        pltpu.VMEM((SUBLANES - 1, tt + CONV_HALO - SUBLANES, c), F32),
                        pltpu.VMEM((tt, c), F32)],
        compiler_params=_cp(("parallel", "arbitrary"), 40),
        name="conv_prompt",
    )(z, *p_arrs, ybuf)


def _conv_sample_kernel(z_ref, cache_ref, w_ref, b_ref, g_ref, beta_ref, y_ref, cn_ref, past, *, steps, nb):
    c = CONV_DIM
    hist = CONV_WIDTH - 1
    for r in range(hist):
        past[r] = cache_ref[:, r, :]
    a = z_ref[:, :c] * jax.nn.sigmoid(z_ref[:, c:])
    for t in range(steps):
        acc = None
        for w in range(CONV_WIDTH):
            idx = t + w
            src = past[idx] if idx < hist else a[(idx - hist) * nb:(idx - hist + 1) * nb]
            term = src * w_ref[w:w + 1, :]
            acc = term if acc is None else acc + term
        y_ref[t * nb:(t + 1) * nb, :] = _ln_silu(acc + b_ref[...], g_ref[...], beta_ref[...]).astype(y_ref.dtype)
    for r in range(hist - steps):
        cn_ref[:, r, :] = past[r + steps]
    for t in range(steps):
        cn_ref[:, hist - steps + t, :] = a[t * nb:(t + 1) * nb]


def conv_sample(z, row0, steps, nb, cache, params, layer, ybuf):
    c = CONV_DIM
    rows = steps * nb
    hist = CONV_WIDTH - 1
    p_arrs, p_specs = zip(*[_layer_rows(p, layer) for p in params])
    return pl.pallas_call(
        _skip_ref(functools.partial(_conv_sample_kernel, steps=steps, nb=nb), 6),
        grid=(1,),
        in_specs=[pl.BlockSpec((rows, 2 * c), lambda i: (row0 // rows, 0)),
                  pl.BlockSpec((None, nb, hist, c), lambda i: (layer, 0, 0, 0)), *p_specs,
                  pl.BlockSpec(memory_space=pl.ANY)],
        out_specs=[pl.BlockSpec((None, rows, c), lambda i: (0, row0 // rows, 0)),
                   pl.BlockSpec((nb, hist, c), lambda i: (0, 0, 0))],
        out_shape=[jax.ShapeDtypeStruct(ybuf.shape, ybuf.dtype), jax.ShapeDtypeStruct((nb, hist, c), F32)],
        input_output_aliases={6: 0},
        scratch_shapes=[pltpu.VMEM((hist, nb, c), F32)],
        compiler_params=_cp(("arbitrary",), 40),
        name="conv_sample",
    )(z, cache, *p_arrs, ybuf)


def _t5_bucket_np(dist):
    n = np.maximum(dist, 0)
    max_exact = N_BUCKETS // 2
    nf = np.maximum(n, 1).astype(np.float32)
    large = max_exact + (np.log(nf / np.float32(max_exact)) / np.float32(math.log(WINDOW / max_exact))
                         * np.float32(N_BUCKETS - max_exact)).astype(np.int32)
    large = np.minimum(large, N_BUCKETS - 1)
    return np.where(n < max_exact, n, large)


def _bias_kernel(t5_ref, bucket_ref, o_ref):
    h = pl.program_id(0)
    bucket = bucket_ref[...]
    acc = jnp.full(bucket.shape, NEG_INF, F32)
    for b in range(N_BUCKETS):
        acc = jnp.where(bucket == b, t5_ref[b, h], acc)
    o_ref[...] = acc


def band_bias(t5_bias, dist, mask):
    bucket = np.where(mask, _t5_bucket_np(dist), -1).astype(np.int32)
    nq, nk = bucket.shape
    return pl.pallas_call(
        _bias_kernel,
        grid=(SWA_HEADS,),
        in_specs=[pl.BlockSpec(memory_space=pltpu.SMEM), pl.BlockSpec((nq, nk), lambda h: (0, 0))],
        out_specs=pl.BlockSpec((None, nq, nk), lambda h: (h, 0, 0)),
        out_shape=jax.ShapeDtypeStruct((SWA_HEADS, nq, nk), F32),
        compiler_params=_cp(("arbitrary",), 16),
        name="band_bias",
    )(t5_bias, jnp.asarray(bucket))


def _softmax_sink_pv(s, sink, v):
    m = jnp.maximum(jnp.max(s, axis=-1, keepdims=True), sink)
    p = jnp.exp(s - m)
    den = jnp.sum(p, axis=-1, keepdims=True) + jnp.exp(sink - m)
    return jnp.dot(p.astype(BF16), v, preferred_element_type=F32) / den


def _swa_prompt_kernel(sink_ref, q_ref, kp_ref, kc_ref, vp_ref, vc_ref, bias_ref, o_ref, *, layer):
    k = jnp.concatenate([kp_ref[...], kc_ref[...]], axis=0)
    v = jnp.concatenate([vp_ref[...], vc_ref[...]], axis=0)
    q = q_ref[...] * SWA_SCALE
    for hp in range(SWA_HEADS // 2):
        outs = []
        for h in (2 * hp, 2 * hp + 1):
            g = h // SWA_REP
            hs = slice(h * SWA_HEAD_DIM, (h + 1) * SWA_HEAD_DIM)
            gs = slice(g * SWA_HEAD_DIM, (g + 1) * SWA_HEAD_DIM)
            s = lax.dot_general(q[:, hs].astype(BF16), k[:, gs].astype(BF16), (((1,), (1,)), ((), ())),
                                preferred_element_type=F32) + bias_ref[h]
            outs.append(_softmax_sink_pv(s, sink_ref[layer, h], v[:, gs].astype(BF16)))
        o_ref[:, 2 * hp * SWA_HEAD_DIM:(2 * hp + 2) * SWA_HEAD_DIM] = (
            jnp.concatenate(outs, axis=1).astype(o_ref.dtype))


def swa_prompt(z, nb, seq, sinks, layer, bias, ybuf):
    blk = WINDOW
    nj = seq // blk
    qw = SWA_HEADS * SWA_HEAD_DIM
    kw = SWA_KV_HEADS * SWA_HEAD_DIM
    cur = lambda col: (lambda b, j: (b * nj + j, col))
    prev = lambda col: (lambda b, j: (b * nj + jnp.maximum(j - 1, 0), col))
    return pl.pallas_call(
        _skip_ref(functools.partial(_swa_prompt_kernel, layer=layer), 7),
        grid=(nb, nj),
        in_specs=[pl.BlockSpec(memory_space=pltpu.SMEM),
                  pl.BlockSpec((blk, qw), cur(COL_Q // qw)),
                  pl.BlockSpec((blk, kw), prev(COL_K // kw)), pl.BlockSpec((blk, kw), cur(COL_K // kw)),
                  pl.BlockSpec((blk, kw), prev(COL_V // kw)), pl.BlockSpec((blk, kw), cur(COL_V // kw)),
                  pl.BlockSpec((None, SWA_HEADS, blk, 2 * blk), lambda b, j: (jnp.minimum(j, 1), 0, 0, 0)),
                  pl.BlockSpec(memory_space=pl.ANY)],
        out_specs=pl.BlockSpec((None, blk, qw), lambda b, j: (1, b * nj + j, 0)),
        out_shape=jax.ShapeDtypeStruct(ybuf.shape, ybuf.dtype),
        input_output_aliases={7: 0},
        compiler_params=_cp(("parallel", "arbitrary"), 32),
        name="swa_prompt",
    )(sinks, z, z, z, z, z, bias, ybuf)


def _swa_sample_kernel(q_ref, kc_ref, vc_ref, kn_ref, vn_ref, bias_ref, sink_ref, o_ref, wk_ref, wv_ref):
    scale = SWA_HEAD_DIM ** -0.5
    nbat, win, ng, hd = kc_ref.shape
    new_rows = kn_ref.shape[1]
    for bi in range(nbat):
        kk = jnp.concatenate([kc_ref[bi].reshape(win * ng, hd), kn_ref[bi]], axis=0)
        vv = jnp.concatenate([vc_ref[bi].reshape(win * ng, hd), vn_ref[bi]], axis=0)
        wk_ref[bi] = kk[new_rows:, :].reshape(win, ng, hd)
        wv_ref[bi] = vv[new_rows:, :].reshape(win, ng, hd)
        s = lax.dot_general(q_ref[bi].astype(BF16), kk.astype(BF16), (((1,), (1,)), ((), ())),
                            preferred_element_type=F32) * scale + bias_ref[...]
        o_ref[bi] = _softmax_sink_pv(s, sink_ref[...], vv.astype(BF16))


def swa_sample(q, k_new, v_new, cache_k, cache_v, layer, bias, sink_col, bb):
    nb, rows, hd = q.shape
    new_rows = k_new.shape[1]
    _, _, win, ng, _ = cache_k.shape
    nk = win * ng + new_rows
    cache_spec = pl.BlockSpec((None, bb, win, ng, hd), lambda b: (layer, b, 0, 0, 0))
    new_spec = pl.BlockSpec((bb, new_rows, hd), lambda b: (b, 0, 0))
    win_spec = pl.BlockSpec((bb, win, ng, hd), lambda b: (b, 0, 0, 0))
    qo_spec = pl.BlockSpec((bb, rows, hd), lambda b: (b, 0, 0))
    win_shape = jax.ShapeDtypeStruct((nb, win, ng, hd), F32)
    return pl.pallas_call(
        _swa_sample_kernel,
        grid=(nb // bb,),
        in_specs=[qo_spec, cache_spec, cache_spec, new_spec, new_spec,
                  pl.BlockSpec((rows, nk), lambda b: (0, 0)), pl.BlockSpec((None, rows, 1), lambda b: (layer, 0, 0))],
        out_specs=[qo_spec, win_spec, win_spec],
        out_shape=[jax.ShapeDtypeStruct(q.shape, F32), win_shape, win_shape],
        compiler_params=_cp(("parallel",), 32),
        name="swa_sample",
    )(q, cache_k, cache_v, k_new, v_new, bias, sink_col)


def _attend(q, mk, mv, allowed=None):
    s = lax.dot_general(q.astype(BF16), mk.astype(BF16), (((1,), (1,)), ((), ())),
                        preferred_element_type=F32) * (MEM_HEAD_DIM ** -0.5)
    if allowed is not None:
        s = jnp.where(allowed, s, NEG_INF)
    m = jnp.max(s, axis=-1, keepdims=True)
    p = jnp.exp(s - m)
    den = jnp.sum(p, axis=-1, keepdims=True)
    return jnp.dot(p.astype(BF16), mv.astype(BF16), preferred_element_type=F32) / den


def _mem_attn_kernel(q0_ref, q1_ref, q2_ref, q3_ref, mk_ref, mv_ref, o_ref):
    for h, q_ref in enumerate((q0_ref, q1_ref, q2_ref, q3_ref)):
        hs = slice(h * MEM_HEAD_DIM, (h + 1) * MEM_HEAD_DIM)
        o_ref[:, hs] = _attend(q_ref[...], mk_ref[:, hs], mv_ref[:, hs]).astype(o_ref.dtype)


def _mem_attn_sample_kernel(q_ref, mk_ref, mv_ref, o_ref):
    nbat, rows, hd = q_ref.shape
    nkeys = mk_ref.shape[1] * MEM_HEADS
    same_head = (lax.broadcasted_iota(jnp.int32, (rows, nkeys), 0) % MEM_HEADS
                 == lax.broadcasted_iota(jnp.int32, (rows, nkeys), 1) % MEM_HEADS)
    for bi in range(nbat):
        mk = mk_ref[bi].reshape(nkeys, hd)
        mv = mv_ref[bi].reshape(nkeys, hd)
        o_ref[bi] = _attend(q_ref[bi], mk, mv, same_head).astype(o_ref.dtype)


def mem_attn_prompt(z, nb, seq, kv, tq, ybuf):
    nt = seq // tq
    mlen = kv.shape[1]
    hd = MEM_HEAD_DIM
    q_specs = [pl.BlockSpec((tq, hd), functools.partial(lambda b, t, h: (b * nt + t, COL_XQ // hd + h), h=h))
               for h in range(MEM_HEADS)]
    return pl.pallas_call(
        _skip_ref(_mem_attn_kernel, 6),
        grid=(nb, nt),
        in_specs=q_specs + [pl.BlockSpec((None, mlen, BRANCH_DIM), lambda b, t: (b, 0, 0)),
                            pl.BlockSpec((None, mlen, BRANCH_DIM), lambda b, t: (b, 0, 1)),
                            pl.BlockSpec(memory_space=pl.ANY)],
        out_specs=pl.BlockSpec((None, tq, BRANCH_DIM), lambda b, t: (3, b * nt + t, 0)),
        out_shape=jax.ShapeDtypeStruct(ybuf.shape, ybuf.dtype),
        input_output_aliases={6: 0},
        compiler_params=_cp(("parallel", "arbitrary"), 32),
        name="mem_attn_prompt",
    )(z, z, z, z, kv, kv, ybuf)


def mem_attn_sample(q, cache_k, cache_v, layer, bb):
    nb, rows, hd = q.shape
    mlen = cache_k.shape[2]
    qo_spec = pl.BlockSpec((bb, rows, hd), lambda b: (b, 0, 0))
    cache_spec = pl.BlockSpec((None, bb, mlen, MEM_HEADS, hd), lambda b: (layer, b, 0, 0, 0))
    return pl.pallas_call(
        _mem_attn_sample_kernel,
        grid=(nb // bb,),
        in_specs=[qo_spec, cache_spec, cache_spec],
        out_specs=qo_spec,
        out_shape=jax.ShapeDtypeStruct(q.shape, BF16),
        compiler_params=_cp(("parallel",), 40),
        name="mem_attn_sample",
    )(q, cache_k, cache_v)


def _ssm_param_kernel(are_ref, aim_ref, ldt_ref, bre_ref, bim_ref, abr_ref, abi_ref, bbr_ref, bbi_ref):
    dt = jnp.exp(ldt_ref[...])
    ar, ai = are_ref[...], aim_ref[...]
    mag = jnp.exp(dt * ar)
    abr, abi = mag * jnp.cos(dt * ai), mag * jnp.sin(dt * ai)
    den = ar * ar + ai * ai
    nr, ni = abr - 1.0, abi
    fre, fim = (nr * ar + ni * ai) / den, (ni * ar - nr * ai) / den
    abr_ref[...] = abr
    abi_ref[...] = abi
    for c in range(SSM_GROUP):
        br, bi = bre_ref[c], bim_ref[c]
        bbr_ref[c] = fre * br - fim * bi
        bbi_ref[c] = fre * bi + fim * br


def ssm_params(a_re, a_im, log_dt, b_re_t, b_im_t):
    d, g, n = a_re.shape
    c = b_re_t.shape[1]
    gn = pl.BlockSpec((None, g, n), lambda l: (l, 0, 0))
    cgn = pl.BlockSpec((None, c, g, n), lambda l: (l, 0, 0, 0))
    return pl.pallas_call(
        _ssm_param_kernel,
        grid=(d,),
        in_specs=[gn, gn, pl.BlockSpec((None, g, 1), lambda l: (l, 0, 0)), cgn, cgn],
        out_specs=[gn, gn, cgn, cgn],
        out_shape=[jax.ShapeDtypeStruct((d, g, n), F32)] * 2 + [jax.ShapeDtypeStruct((d, c, g, n), F32)] * 2,
        compiler_params=_cp(("arbitrary",), 16),
        name="ssm_params",
    )(a_re, a_im, log_dt.reshape(d, g, 1), b_re_t, b_im_t)


def _cmul_add(ar, ai, hr, hi, br, bi):
    return ar * hr - ai * hi + br, ar * hi + ai * hr + bi


def _ssm_out(u, sre, sim, cre_ref, cim_ref, d_ref):
    y = (jnp.dot(sre[...].astype(BF16), cre_ref[...], preferred_element_type=F32)
         - jnp.dot(sim[...].astype(BF16), cim_ref[...], preferred_element_type=F32)
         + d_ref[...] * u)
    return jax.nn.gelu(y)


LANES = 128
PIECE_PITCH = 264


def _ssm_prompt_kernel(u_ref, bre_ref, bim_ref, cre_ref, cim_ref, ar_ref, ai_ref, d_ref,
                       y_ref, hr_ref, hi_ref, sre, sim, upad, uperm, ypad, *, clen):
    np_ = SCAN_LANES
    w = sre.shape[1]
    slabs = [slice(s * LANES, (s + 1) * LANES) for s in range(u_ref.shape[1] // LANES)]

    def rows(t):
        return pl.ds(pl.multiple_of(t * np_, np_), np_)

    def piece_rows(t):
        return pl.ds(t, np_, stride=PIECE_PITCH)

    for s, ls in enumerate(slabs):
        for p in range(np_):
            upad[s, p * PIECE_PITCH:p * PIECE_PITCH + clen, :] = u_ref[p * clen:(p + 1) * clen, ls]

    def gather(t, carry):
        for s, ls in enumerate(slabs):
            uperm[rows(t), ls] = upad[s, piece_rows(t), :]
        return carry

    lax.fori_loop(0, clen, gather, 0, unroll=SCAN_UNROLL)
    u = uperm[...]
    ub = u.astype(BF16)
    sre[...] = jnp.dot(ub, bre_ref[...], preferred_element_type=F32)
    sim[...] = jnp.dot(ub, bim_ref[...], preferred_element_type=F32)
    ar1, ai1 = ar_ref[...], ai_ref[...]
    ar = jnp.broadcast_to(ar1, (np_, w))
    ai = jnp.broadcast_to(ai1, (np_, w))

    def local_step(t, carry):
        return _cmul_add(ar, ai, carry[0], carry[1], sre[rows(t), :], sim[rows(t), :])

    zero = jnp.zeros((np_, w), F32)
    fr, fi = lax.fori_loop(0, clen, local_step, (zero, zero), unroll=SCAN_UNROLL)

    pr, pi = ar1, ai1
    for _ in range(int(math.log2(clen))):
        pr, pi = pr * pr - pi * pi, 2.0 * pr * pi
    row = lax.broadcasted_iota(jnp.int32, (np_, w), 0)
    cr = jnp.zeros((1, w), F32)
    ci = jnp.zeros((1, w), F32)
    hr0, hi0 = zero, zero
    for p in range(1, np_):
        cr, ci = _cmul_add(pr, pi, cr, ci, fr[p - 1:p], fi[p - 1:p])
        hr0 = jnp.where(row == p, cr, hr0)
        hi0 = jnp.where(row == p, ci, hi0)
    fin_r, fin_i = _cmul_add(pr, pi, cr, ci, fr[np_ - 1:np_], fi[np_ - 1:np_])
    hr_ref[...] = fin_r
    hi_ref[...] = fin_i

    def full_step(t, carry):
        nr, ni = _cmul_add(ar, ai, carry[0], carry[1], sre[rows(t), :], sim[rows(t), :])
        sre[rows(t), :] = nr
        sim[rows(t), :] = ni
        return nr, ni

    lax.fori_loop(0, clen, full_step, (hr0, hi0), unroll=SCAN_UNROLL)
    uperm[...] = _ssm_out(u, sre, sim, cre_ref, cim_ref, d_ref)

    def scatter(t, carry):
        for s, ls in enumerate(slabs):
            ypad[s, piece_rows(t), :] = uperm[rows(t), ls]
        return carry

    lax.fori_loop(0, clen, scatter, 0, unroll=SCAN_UNROLL)
    for s, ls in enumerate(slabs):
        for p in range(np_):
            y_ref[p * clen:(p + 1) * clen, ls] = ypad[s, p * PIECE_PITCH:p * PIECE_PITCH + clen, :]


def _ssm_specs(layer, kdim):
    kmap = lambda *idx: (layer, idx[kdim], 0, 0)
    shapes = [(SSM_UW, SSM_CW), (SSM_UW, SSM_CW), (SSM_CW, SSM_UW), (SSM_CW, SSM_UW), (1, SSM_CW), (1, SSM_CW),
              (1, SSM_UW)]
    return [pl.BlockSpec((None, None) + s, kmap) for s in shapes]


def ssm_prompt(z, nb, seq, mats, layer):
    clen = seq // SCAN_LANES
    nstate = SSM_GROUPS * SSM_STATE
    st = pl.BlockSpec((None, 1, SSM_CW), lambda b, k: (b, 0, k))
    pad_shape = (SSM_UW // LANES, SCAN_LANES * PIECE_PITCH, LANES)
    return pl.pallas_call(
        functools.partial(_ssm_prompt_kernel, clen=clen),
        grid=(nb, SSM_CHUNKS),
        in_specs=[pl.BlockSpec((seq, SSM_UW), lambda b, k: (b, COL_U // SSM_UW + k))] + _ssm_specs(layer, 1),
        out_specs=[pl.BlockSpec((seq, SSM_UW), lambda b, k: (b, k)), st, st],
        out_shape=[jax.ShapeDtypeStruct((nb * seq, BRANCH_DIM), F32),
                   jax.ShapeDtypeStruct((nb, 1, nstate), F32), jax.ShapeDtypeStruct((nb, 1, nstate), F32)],
        scratch_shapes=[pltpu.VMEM((seq, SSM_CW), F32), pltpu.VMEM((seq, SSM_CW), F32),
                        pltpu.VMEM(pad_shape, F32), pltpu.VMEM((seq, SSM_UW), F32), pltpu.VMEM(pad_shape, F32)],
        compiler_params=_cp(("parallel", "parallel"), 48),
        name="ssm_prompt",
    )(z, *mats)


def _ssm_sample_kernel(u_ref, bre_ref, bim_ref, cre_ref, cim_ref, ar_ref, ai_ref, d_ref, h0r_ref, h0i_ref,
                       y_ref, hr_ref, hi_ref, sre, sim, *, steps, nb):
    w = sre.shape[1]
    u = u_ref[...]
    ub = u.astype(BF16)
    bur = jnp.dot(ub, bre_ref[...], preferred_element_type=F32)
    bui = jnp.dot(ub, bim_ref[...], preferred_element_type=F32)
    ar = jnp.broadcast_to(ar_ref[...], (nb, w))
    ai = jnp.broadcast_to(ai_ref[...], (nb, w))
    hr, hi = h0r_ref[...], h0i_ref[...]
    for t in range(steps):
        rs = slice(t * nb, (t + 1) * nb)
        hr, hi = _cmul_add(ar, ai, hr, hi, bur[rs], bui[rs])
        sre[rs, :] = hr
        sim[rs, :] = hi
    hr_ref[...] = hr
    hi_ref[...] = hi
    y_ref[...] = _ssm_out(u, sre, sim, cre_ref, cim_ref, d_ref)


def ssm_sample(z, row0, steps, nb, mats, h0_re, h0_im, layer):
    rows = steps * nb
    nstate = SSM_GROUPS * SSM_STATE
    st_in = pl.BlockSpec((None, nb, SSM_CW), lambda k: (layer, 0, k))
    st_out = pl.BlockSpec((nb, SSM_CW), lambda k: (0, k))
    return pl.pallas_call(
        functools.partial(_ssm_sample_kernel, steps=steps, nb=nb),
        grid=(SSM_CHUNKS,),
        in_specs=[pl.BlockSpec((rows, SSM_UW), lambda k: (row0 // rows, COL_U // SSM_UW + k))]
                 + _ssm_specs(layer, 0) + [st_in, st_in],
        out_specs=[pl.BlockSpec((rows, SSM_UW), lambda k: (0, k)), st_out, st_out],
        out_shape=[jax.ShapeDtypeStruct((rows, BRANCH_DIM), F32),
                   jax.ShapeDtypeStruct((nb, nstate), F32), jax.ShapeDtypeStruct((nb, nstate), F32)],
        scratch_shapes=[pltpu.VMEM((rows, SSM_CW), F32), pltpu.VMEM((rows, SSM_CW), F32)],
        compiler_params=_cp(("parallel",), 32),
        name="ssm_sample",
    )(z, *mats, h0_re, h0_im)


def _block_diag(blocks, nblk):
    rows, c = blocks.shape[-2:]
    tiled = jnp.tile(blocks, (1,) * (blocks.ndim - 1) + (nblk,))
    row_blk = lax.broadcasted_iota(jnp.int32, (rows, nblk * c), 0) // (rows // nblk)
    col_blk = lax.broadcasted_iota(jnp.int32, (rows, nblk * c), 1) // c
    return jnp.where(row_blk == col_blk, tiled, 0.0)


def kernel(x_prompt, x_sample, cache_conv, cache_win_k, cache_win_v, state_ssm_re, state_ssm_im, cache_mem_k, cache_mem_v, mem_prompt, t5_bias, norm_mix_pre, norm_mix_post, norm_ffn_pre, norm_ffn_post, norm_mem, w_in, conv_w, conv_b, conv_ln_g, conv_ln_b, attn_sinks, ssm_a_re, ssm_a_im, ssm_log_dt, ssm_b_re, ssm_b_im, ssm_c_re, ssm_c_im, ssm_d, ssm_w_glu, w_mem_kv, w_branch, w_out, w_ffn_in, w_ffn_out):
    bp, seq, d = x_prompt.shape
    bs, steps, _ = x_sample.shape
    rows_p = bp * seq
    rows_s = bs * steps
    rows = rows_p + rows_s
    mlen = mem_prompt.shape[1]
    kw = SWA_KV_HEADS * SWA_HEAD_DIM
    gpc = SSM_GROUPS // SSM_CHUNKS

    tm = rows // 5
    tm_merge = rows // 8
    tm_ffn_out = rows // 8
    tr = rows // 20

    qi = np.arange(WINDOW)
    ki = np.arange(2 * WINDOW) - WINDOW
    dist_p = qi[:, None] - ki[None, :]
    bias_p = band_bias(t5_bias, dist_p, (dist_p >= 0) & (dist_p < WINDOW))
    bias_p = jnp.stack([jnp.where(jnp.asarray(ki < 0), NEG_INF, bias_p), bias_p])
    nk_s = WINDOW + steps
    dist_s = (WINDOW + np.arange(SUBLANES))[:, None] - np.arange(nk_s)[None, :]
    bias_s = band_bias(t5_bias, dist_s, (dist_s >= 0) & (dist_s < WINDOW))[:, :steps]
    bias_s = bias_s.reshape(SWA_KV_HEADS, SWA_REP, steps, nk_s).transpose(0, 2, 1, 3)
    same_group = jnp.eye(SWA_KV_HEADS, dtype=bool)[:, None, None, None, :]
    bias_s = jnp.where(same_group, bias_s[..., None], NEG_INF)
    bias_s = bias_s.reshape(SWA_HEADS * steps, nk_s * SWA_KV_HEADS)

    abar_re, abar_im, bbar_re, bbar_im = ssm_params(
        ssm_a_re, ssm_a_im, ssm_log_dt, ssm_b_re.transpose(0, 3, 1, 2), ssm_b_im.transpose(0, 3, 1, 2))

    def in_mat(x):
        x = x.reshape(DEPTH, SSM_GROUP, SSM_CHUNKS, gpc, SSM_STATE).transpose(0, 2, 3, 1, 4)
        return _block_diag(x.reshape(DEPTH, SSM_CHUNKS, SSM_UW, SSM_STATE), gpc).astype(BF16)

    def out_mat(x):
        x = x.reshape(DEPTH, SSM_CHUNKS, gpc, SSM_GROUP, SSM_STATE).transpose(0, 1, 2, 4, 3)
        return _block_diag(x.reshape(DEPTH, SSM_CHUNKS, SSM_CW, SSM_GROUP), gpc).astype(BF16)

    bmat_re, bmat_im = in_mat(bbar_re), in_mat(bbar_im)
    cmat_re, cmat_im = out_mat(ssm_c_re), out_mat(ssm_c_im)
    abar_re = abar_re.reshape(DEPTH, SSM_CHUNKS, 1, SSM_CW)
    abar_im = abar_im.reshape(DEPTH, SSM_CHUNKS, 1, SSM_CW)
    dvec = ssm_d.reshape(DEPTH, SSM_CHUNKS, 1, SSM_UW)

    h0_re = state_ssm_re.reshape(DEPTH, bs, -1)
    h0_im = state_ssm_im.reshape(DEPTH, bs, -1)
    mem_rows = mem_prompt.reshape(bp * mlen, d)

    x_p0 = x_prompt.reshape(rows_p, d)
    x_s0 = x_sample.transpose(1, 0, 2).reshape(rows_s, d)
    x = None
    xn = rmsnorm_stacked(x_p0, x_s0, (norm_mix_pre, 0))

    conv_params = (conv_w, conv_b, conv_ln_g, conv_ln_b)
    mats = (bmat_re, bmat_im, cmat_re, cmat_im, abar_re, abar_im, dvec)
    sink_cols = jnp.tile(attn_sinks.reshape(DEPTH, SWA_KV_HEADS, 1, SWA_REP), (1, 1, steps, 1))
    sink_cols = sink_cols.reshape(DEPTH, SWA_HEADS * steps, 1)

    ybuf = jnp.zeros((4, rows, BRANCH_DIM), BF16)
    outs = [[] for _ in range(12)]
    for l in range(DEPTH):
        z = matmul(xn, w_in, l, F32, tm, 512, n=COL_GATE)
        z_s = lax.slice(z, (rows_p, 0), (rows, COL_GATE))

        ybuf, conv_p = conv_prompt(z, bp, seq, conv_params, l, 256, ybuf)
        ybuf, conv_s = conv_sample(z, rows_p, steps, bs, cache_conv, conv_params, l, ybuf)

        ybuf = swa_prompt(z, bp, seq, attn_sinks, l, bias_p, ybuf)
        q_s = z_s[:, COL_Q:COL_K].reshape(steps, bs, SWA_KV_HEADS, SWA_REP, SWA_HEAD_DIM)
        q_s = q_s.transpose(1, 2, 0, 3, 4).reshape(bs, SWA_HEADS * steps, SWA_HEAD_DIM)
        k_s = z_s[:, COL_K:COL_V].reshape(steps, bs, SWA_KV_HEADS, SWA_HEAD_DIM).transpose(1, 0, 2, 3)
        v_s = z_s[:, COL_V:COL_U].reshape(steps, bs, SWA_KV_HEADS, SWA_HEAD_DIM).transpose(1, 0, 2, 3)
        k_s = k_s.reshape(bs, steps * SWA_KV_HEADS, SWA_HEAD_DIM)
        v_s = v_s.reshape(bs, steps * SWA_KV_HEADS, SWA_HEAD_DIM)
        ob_s, wk_s, wv_s = swa_sample(q_s, k_s, v_s, cache_win_k, cache_win_v, l, bias_s, sink_cols, 4)
        yb_s = ob_s.reshape(bs, SWA_KV_HEADS, steps, SWA_REP, SWA_HEAD_DIM).transpose(2, 0, 1, 3, 4)
        yb_s = yb_s.reshape(rows_s, BRANCH_DIM).astype(BF16)

        yg_p, hr_p, hi_p = ssm_prompt(z, bp, seq, mats, l)
        ybuf = glu(yg_p, ssm_w_glu, l, 1024, ybuf, 2, 0)
        yg_s, hr_s, hi_s = ssm_sample(z, rows_p, steps, bs, mats, h0_re, h0_im, l)
        ybuf = glu(yg_s, ssm_w_glu, l, rows_s, ybuf, 2, rows_p)

        kv = matmul(rmsnorm_bf16(mem_rows, (norm_mem, l), 256), w_mem_kv, l, F32, bp * mlen, 512)
        kv = kv.reshape(bp, mlen, 2 * BRANCH_DIM)
        ybuf = mem_attn_prompt(z, bp, seq, kv, 512, ybuf)
        xq_s = z_s[:, COL_XQ:COL_GATE].reshape(steps, bs, MEM_HEADS, MEM_HEAD_DIM).transpose(1, 0, 2, 3)
        yx_s = mem_attn_sample(xq_s.reshape(bs, steps * MEM_HEADS, MEM_HEAD_DIM), cache_mem_k, cache_mem_v, l, 2)
        yx_s = yx_s.reshape(bs, steps, BRANCH_DIM).transpose(1, 0, 2).reshape(rows_s, BRANCH_DIM)

        ybuf = lax.dynamic_update_slice(ybuf, yb_s[None], (1, rows_p, 0))
        ybuf = lax.dynamic_update_slice(ybuf, yx_s[None], (3, rows_p, 0))
        merged = merge_branches(xn, ybuf, w_in, w_branch, l, tm_merge, 512)
        mix = matmul(merged, w_out, l, BF16, tm, 512)
        if l == 0:
            x, hn = resid_norm_stacked(x_p0, x_s0, mix, (norm_mix_post, l), (norm_ffn_pre, l))
        else:
            x, hn = resid_norm(x, mix, (norm_mix_post, l), (norm_ffn_pre, l), tr)

        hid = ffn_in(hn, w_ffn_in, l, tm, 256)
        f = matmul(hid, w_ffn_out, l, BF16, tm_ffn_out, 256, single_a=True)
        if l + 1 < DEPTH:
            x, xn = resid_norm(x, f, (norm_ffn_post, l), (norm_mix_pre, l + 1), tr)
        else:
            y_p, y_s = resid_split(x, f, (norm_ffn_post, l), rows_s)

        kv_win = jnp.stack([lax.slice(z, ((b + 1) * seq - WINDOW, COL_K), ((b + 1) * seq, COL_U))
                            for b in range(bp)])
        new = (conv_p, conv_s,
               kv_win[..., :kw].reshape(bp, WINDOW, SWA_KV_HEADS, SWA_HEAD_DIM),
               kv_win[..., kw:].reshape(bp, WINDOW, SWA_KV_HEADS, SWA_HEAD_DIM),
               wk_s, wv_s,
               hr_p.reshape(bp, SSM_GROUPS, SSM_STATE), hi_p.reshape(bp, SSM_GROUPS, SSM_STATE),
               hr_s.reshape(bs, SSM_GROUPS, SSM_STATE), hi_s.reshape(bs, SSM_GROUPS, SSM_STATE),
               kv[..., :BRANCH_DIM].reshape(bp, mlen, MEM_HEADS, MEM_HEAD_DIM),
               kv[..., BRANCH_DIM:].reshape(bp, mlen, MEM_HEADS, MEM_HEAD_DIM))
        for acc, val in zip(outs, new):
            acc.append(val)

    y_prompt = y_p.reshape(bp, seq, d)
    y_sample = y_s.reshape(steps, bs, d).transpose(1, 0, 2)
    return (y_prompt, y_sample) + tuple(jnp.stack(o) for o in outs)
```

```python
import functools
import math

import jax
import jax.numpy as jnp
import numpy as np
from jax import lax
from jax.experimental import pallas as pl
from jax.experimental.pallas import tpu as pltpu

F32 = jnp.float32
BF16 = jnp.bfloat16

D_MODEL = 4096
DEPTH = 4
BRANCH_DIM = 1024
CONV_WIDTH = 31
SWA_HEAD_DIM = 64
SWA_HEADS = 16
SWA_KV_HEADS = 4
SWA_REP = 4
WINDOW = 128
N_BUCKETS = 32
SSM_GROUP = 16
SSM_GROUPS = 64
SSM_STATE = 64
MEM_HEADS = 4
MEM_HEAD_DIM = 256
D_FF = 11008
COL_Q = 2048
COL_K = 3072
COL_V = 3328
COL_U = 3584
COL_XQ = 4608
COL_GATE = 5632
N_IN = COL_GATE + 4 * D_MODEL
NEG_INF = -1e30

SWA_SCALE = SWA_HEAD_DIM ** -0.5
assert math.log2(SWA_HEAD_DIM) % 2 == 0

SSM_CHUNKS = 4
SSM_CW = SSM_GROUPS // SSM_CHUNKS * SSM_STATE
SSM_UW = SSM_GROUPS // SSM_CHUNKS * SSM_GROUP
SCAN_LANES = 8
SCAN_UNROLL = 8


def _cp(sem, vmem_mb):
    return pltpu.CompilerParams(dimension_semantics=sem, vmem_limit_bytes=vmem_mb << 20)


def _layer_rows(stacked, layer):
    arr = stacked if stacked.ndim == 3 else stacked.reshape(stacked.shape[0], 1, stacked.shape[1])
    return arr, pl.BlockSpec((None,) + arr.shape[1:], lambda *_: (layer, 0, 0))


def _rmsnorm_kernel(x_ref, g_ref, o_ref):
    x = x_ref[...]
    y = x * lax.rsqrt(jnp.mean(x * x, axis=-1, keepdims=True) + 1e-6)
    o_ref[...] = (y * g_ref[...]).astype(o_ref.dtype)


def rmsnorm_bf16(x, g, tr):
    m, d = x.shape
    g_arr, g_spec = _layer_rows(*g)
    return pl.pallas_call(
        _rmsnorm_kernel,
        grid=(m // tr,),
        in_specs=[pl.BlockSpec((tr, d), lambda i: (i, 0)), g_spec],
        out_specs=pl.BlockSpec((tr, d), lambda i: (i, 0)),
        out_shape=jax.ShapeDtypeStruct((m, d), BF16),
        compiler_params=_cp(("parallel",), 40),
        name="rmsnorm",
    )(x, g_arr)


def _stacked_rows(nfull, p_ref, s_ref):
    return jnp.where(pl.program_id(0) < nfull, p_ref[...], s_ref[...])


def _stacked_specs(tr, d, nfull):
    return [pl.BlockSpec((tr, d), lambda i: (jnp.minimum(i, nfull - 1), 0)), pl.BlockSpec((tr, d), lambda i: (0, 0))]


def _rmsnorm_stacked_kernel(xp_ref, xs_ref, g_ref, o_ref, *, nfull):
    x = _stacked_rows(nfull, xp_ref, xs_ref)
    y = x * lax.rsqrt(jnp.mean(x * x, axis=-1, keepdims=True) + 1e-6)
    o_ref[...] = (y * g_ref[...]).astype(o_ref.dtype)


def rmsnorm_stacked(x_p, x_s, g):
    tr, d = x_s.shape
    nfull = x_p.shape[0] // tr
    g_arr, g_spec = _layer_rows(*g)
    return pl.pallas_call(
        functools.partial(_rmsnorm_stacked_kernel, nfull=nfull),
        grid=(nfull + 1,),
        in_specs=_stacked_specs(tr, d, nfull) + [g_spec],
        out_specs=pl.BlockSpec((tr, d), lambda i: (i, 0)),
        out_shape=jax.ShapeDtypeStruct((x_p.shape[0] + tr, d), BF16),
        compiler_params=_cp(("parallel",), 40),
        name="rmsnorm_stacked",
    )(x_p, x_s, g_arr)


def _resid_norm_stacked_kernel(xp_ref, xs_ref, y_ref, gp_ref, gn_ref, xo_ref, hn_ref, *, nfull):
    y = y_ref[...].astype(F32)
    yn = y * lax.rsqrt(jnp.mean(y * y, axis=-1, keepdims=True) + 1e-6) * gp_ref[...]
    x = _stacked_rows(nfull, xp_ref, xs_ref) + yn
    xo_ref[...] = x
    h = x * lax.rsqrt(jnp.mean(x * x, axis=-1, keepdims=True) + 1e-6)
    hn_ref[...] = (h * gn_ref[...]).astype(hn_ref.dtype)


def resid_norm_stacked(x_p, x_s, y, g_post, g_next):
    tr, d = x_s.shape
    nfull = x_p.shape[0] // tr
    m = y.shape[0]
    row = pl.BlockSpec((tr, d), lambda i: (i, 0))
    gp_arr, gp_spec = _layer_rows(*g_post)
    gn_arr, gn_spec = _layer_rows(*g_next)
    return pl.pallas_call(
        functools.partial(_resid_norm_stacked_kernel, nfull=nfull),
        grid=(nfull + 1,),
        in_specs=_stacked_specs(tr, d, nfull) + [row, gp_spec, gn_spec],
        out_specs=[row, row],
        out_shape=[jax.ShapeDtypeStruct((m, d), F32), jax.ShapeDtypeStruct((m, d), BF16)],
        compiler_params=_cp(("parallel",), 48),
        name="resid_norm_stacked",
    )(x_p, x_s, y, gp_arr, gn_arr)


def _resid_split_kernel(x_ref, y_ref, gp_ref, op_ref, os_ref, *, nfull):
    y = y_ref[...].astype(F32)
    out = x_ref[...] + y * lax.rsqrt(jnp.mean(y * y, axis=-1, keepdims=True) + 1e-6) * gp_ref[...]

    @pl.when(pl.program_id(0) < nfull)
    def _():
        op_ref[...] = out

    @pl.when(pl.program_id(0) == nfull)
    def _():
        os_ref[...] = out


def resid_split(x, y, g_post, rows_s):
    m, d = x.shape
    tr = rows_s
    nfull = (m - rows_s) // tr
    row = pl.BlockSpec((tr, d), lambda i: (i, 0))
    gp_arr, gp_spec = _layer_rows(*g_post)
    return pl.pallas_call(
        functools.partial(_resid_split_kernel, nfull=nfull),
        grid=(nfull + 1,),
        in_specs=[row, row, gp_spec],
        out_specs=_stacked_specs(tr, d, nfull),
        out_shape=[jax.ShapeDtypeStruct((m - rows_s, d), F32), jax.ShapeDtypeStruct((rows_s, d), F32)],
        compiler_params=_cp(("arbitrary",), 48),
        name="resid_split",
    )(x, y, gp_arr)


def _resid_norm_kernel(x_ref, y_ref, gp_ref, gn_ref, xo_ref, hn_ref):
    y = y_ref[...].astype(F32)
    yn = y * lax.rsqrt(jnp.mean(y * y, axis=-1, keepdims=True) + 1e-6) * gp_ref[...]
    x = x_ref[...] + yn
    xo_ref[...] = x
    h = x * lax.rsqrt(jnp.mean(x * x, axis=-1, keepdims=True) + 1e-6)
    hn_ref[...] = (h * gn_ref[...]).astype(hn_ref.dtype)


def resid_norm(x, y, g_post, g_next, tr):
    m, d = x.shape
    row = pl.BlockSpec((tr, d), lambda i: (i, 0))
    gp_arr, gp_spec = _layer_rows(*g_post)
    gn_arr, gn_spec = _layer_rows(*g_next)
    return pl.pallas_call(
        _resid_norm_kernel,
        grid=(m // tr,),
        in_specs=[row, row, gp_spec, gn_spec],
        out_specs=[row, row],
        out_shape=[jax.ShapeDtypeStruct((m, d), F32), jax.ShapeDtypeStruct((m, d), BF16)],
        compiler_params=_cp(("parallel",), 48),
        name="resid_norm",
    )(x, y, gp_arr, gn_arr)


def _mm_kernel(a_ref, w_ref, o_ref):
    o_ref[...] = jnp.dot(a_ref[...], w_ref[...].astype(BF16), preferred_element_type=F32).astype(o_ref.dtype)


def matmul(a, w, layer, out_dtype, tm, tn, n=None, single_a=False):
    m, k = a.shape
    n = w.shape[-1] if n is None else n
    a_mode = dict(pipeline_mode=pl.Buffered(1)) if single_a else {}
    return pl.pallas_call(
        _mm_kernel,
        grid=(m // tm, n // tn),
        in_specs=[pl.BlockSpec((tm, k), lambda i, j: (i, 0), **a_mode),
                  pl.BlockSpec((None, k, tn), lambda i, j: (layer, 0, j))],
        out_specs=pl.BlockSpec((tm, tn), lambda i, j: (i, j)),
        out_shape=jax.ShapeDtypeStruct((m, n), out_dtype),
        compiler_params=_cp(("parallel", "parallel"), 56),
        name="matmul",
    )(a, w)


def _ffn_in_kernel(a_ref, wg_ref, wu_ref, o_ref):
    a = a_ref[...]
    g = jnp.dot(a, wg_ref[...].astype(BF16), preferred_element_type=F32)
    u = jnp.dot(a, wu_ref[...].astype(BF16), preferred_element_type=F32)
    o_ref[...] = (jax.nn.silu(g) * u).astype(o_ref.dtype)


def ffn_in(a, w, layer, tm, tn):
    m, k = a.shape
    nt = D_FF // tn
    return pl.pallas_call(
        _ffn_in_kernel,
        grid=(m // tm, nt),
        in_specs=[pl.BlockSpec((tm, k), lambda i, j: (i, 0)),
                  pl.BlockSpec((None, k, tn), lambda i, j: (layer, 0, j)),
                  pl.BlockSpec((None, k, tn), lambda i, j: (layer, 0, j + nt))],
        out_specs=pl.BlockSpec((tm, tn), lambda i, j: (i, j)),
        out_shape=jax.ShapeDtypeStruct((m, D_FF), BF16),
        compiler_params=_cp(("parallel", "parallel"), 56),
        name="ffn_in",
    )(a, w, w)


def _merge_kernel(xn_ref, y_ref, wg_ref, wb_ref, o_ref, acc):
    b = pl.program_id(2)
    @pl.when(b == 0)
    def _():
        acc[...] = jnp.zeros(acc.shape, F32)

    gate = jnp.dot(xn_ref[...], wg_ref[...].astype(BF16), preferred_element_type=F32)
    term = jax.nn.sigmoid(gate) * jnp.dot(y_ref[...], wb_ref[...].astype(BF16), preferred_element_type=F32)
    total = acc[...] + term
    acc[...] = total
    o_ref[...] = total.astype(o_ref.dtype)


def merge_branches(xn, y_all, w_in, w_branch, layer, tm, tn):
    nb, m, kb = y_all.shape
    k = xn.shape[1]
    gate0 = COL_GATE // tn
    per = D_MODEL // tn
    return pl.pallas_call(
        _merge_kernel,
        grid=(m // tm, per, nb),
        in_specs=[pl.BlockSpec((tm, k), lambda i, j, b: (i, 0)),
                  pl.BlockSpec((None, tm, kb), lambda i, j, b: (b, i, 0)),
                  pl.BlockSpec((None, k, tn), lambda i, j, b: (layer, 0, gate0 + b * per + j)),
                  pl.BlockSpec((None, None, kb, tn), lambda i, j, b: (layer, b, 0, j))],
        out_specs=pl.BlockSpec((tm, tn), lambda i, j, b: (i, j)),
        out_shape=jax.ShapeDtypeStruct((m, D_MODEL), BF16),
        scratch_shapes=[pltpu.VMEM((tm, tn), F32)],
        compiler_params=_cp(("parallel", "parallel", "arbitrary"), 56),
        name="merge",
    )(xn, y_all, w_in, w_branch)


def _skip_ref(body, pos):
    def wrapped(*refs):
        return body(*refs[:pos], *refs[pos + 1:])
    return wrapped


def _glu_kernel(y_ref, w_ref, o_ref):
    y = y_ref[...]
    s = jnp.dot(y.astype(BF16), w_ref[...].astype(BF16), preferred_element_type=F32)
    o_ref[...] = (y * jax.nn.sigmoid(s)).astype(o_ref.dtype)


def glu(y, w, layer, tr, ybuf=None, branch=0, row0=0):
    m, d = y.shape
    in_specs = [pl.BlockSpec((tr, d), lambda i: (i, 0)), pl.BlockSpec((None, d, d), lambda i: (layer, 0, 0))]
    if ybuf is None:
        return pl.pallas_call(
            _glu_kernel,
            grid=(m // tr,),
            in_specs=in_specs,
            out_specs=pl.BlockSpec((tr, d), lambda i: (i, 0)),
            out_shape=jax.ShapeDtypeStruct((m, d), BF16),
            compiler_params=_cp(("parallel",), 40),
            name="glu",
        )(y, w)
    return pl.pallas_call(
        _skip_ref(_glu_kernel, 2),
        grid=(m // tr,),
        in_specs=in_specs + [pl.BlockSpec(memory_space=pl.ANY)],
        out_specs=pl.BlockSpec((None, tr, d), lambda i: (branch, row0 // tr + i, 0)),
        out_shape=jax.ShapeDtypeStruct(ybuf.shape, ybuf.dtype),
        input_output_aliases={2: 0},
        compiler_params=_cp(("parallel",), 40),
        name="glu_into",
    )(y, w, ybuf)


def _ln_silu(y, g, b):
    mu = jnp.mean(y, axis=-1, keepdims=True)
    yc = y - mu
    yn = yc * lax.rsqrt(jnp.mean(yc * yc, axis=-1, keepdims=True) + 1e-5)
    return jax.nn.silu(yn * g + b)


CONV_HALO = 32
CONV_LANES = 128


SUBLANES = 8


def _conv_prompt_kernel(z_ref, w_ref, b_ref, g_ref, beta_ref, y_ref, cn_ref, xx, xs, acc, *, tt):
    c = CONV_DIM_
    off = CONV_HALO - (CONV_WIDTH - 1)
    span = tt + CONV_HALO - SUBLANES

    @pl.when(pl.program_id(1) == 0)
    def _():
        xx[0:CONV_HALO, :] = jnp.zeros((CONV_HALO, c), F32)

    xx[CONV_HALO:CONV_HALO + tt, :] = z_ref[:, :c] * jax.nn.sigmoid(z_ref[:, c:])
    for r in range(1, SUBLANES):
        xs[r - 1, 0:span, :] = xx[r:r + span, :]
    def lane_chunk(lc, carry):
        ls = pl.ds(pl.multiple_of(lc * CONV_LANES, CONV_LANES), CONV_LANES)
        a = None
        for w in range(CONV_WIDTH):
            q, r = divmod(off + w, SUBLANES)
            lo = SUBLANES * q
            src = xx[lo:lo + tt, ls] if r == 0 else xs[r - 1, lo:lo + tt, ls]
            term = src * w_ref[w:w + 1, ls]
            a = term if a is None else a + term
        acc[:, ls] = a
        return carry

    lax.fori_loop(0, c // CONV_LANES, lane_chunk, 0)
    y_ref[...] = _ln_silu(acc[...] + b_ref[...], g_ref[...], beta_ref[...]).astype(y_ref.dtype)
    cn_ref[...] = xx[tt + off:tt + CONV_HALO, :]
    xx[0:CONV_HALO, :] = xx[tt:tt + CONV_HALO, :]


CONV_DIM_ = BRANCH_DIM


def conv_prompt(z, nb, seq, params, layer, tt, ybuf):
    c = CONV_DIM_
    nt = seq // tt
    p_arrs, p_specs = zip(*[_layer_rows(p, layer) for p in params])
    return pl.pallas_call(
        _skip_ref(functools.partial(_conv_prompt_kernel, tt=tt), 5),
        grid=(nb, nt),
        in_specs=[pl.BlockSpec((tt, 2 * c), lambda bi, t: (bi * nt + t, 0)), *p_specs,
                  pl.BlockSpec(memory_space=pl.ANY)],
        out_specs=[pl.BlockSpec((None, tt, c), lambda bi, t: (0, bi * nt + t, 0)),
                   pl.BlockSpec((None, CONV_WIDTH - 1, c), lambda bi, t: (bi, 0, 0))],
        out_shape=[jax.ShapeDtypeStruct(ybuf.shape, ybuf.dtype),
                   jax.ShapeDtypeStruct((nb, CONV_WIDTH - 1, c), F32)],
        input_output_aliases={5: 0},
        scratch_shapes=[pltpu.VMEM((tt + CONV_HALO, c), F32),
                        pltpu.VMEM((SUBLANES - 1, tt + CONV_HALO - SUBLANES, c), F32),
                        pltpu.VMEM((tt, c), F32)],
        compiler_params=_cp(("parallel", "arbitrary"), 40),
        name="conv_prompt",
    )(z, *p_arrs, ybuf)


def _conv_sample_kernel(z_ref, cache_ref, w_ref, b_ref, g_ref, beta_ref, y_ref, cn_ref, past, *, steps, nb):
    c = CONV_DIM_
    hist = CONV_WIDTH - 1
    for r in range(hist):
        past[r] = cache_ref[:, r, :]
    a = z_ref[:, :c] * jax.nn.sigmoid(z_ref[:, c:])
    for t in range(steps):
        acc = None
        for w in range(CONV_WIDTH):
            idx = t + w
            src = past[idx] if idx < hist else a[(idx - hist) * nb:(idx - hist + 1) * nb]
            term = src * w_ref[w:w + 1, :]
            acc = term if acc is None else acc + term
        y_ref[t * nb:(t + 1) * nb, :] = _ln_silu(acc + b_ref[...], g_ref[...], beta_ref[...]).astype(y_ref.dtype)
    for r in range(hist - steps):
        cn_ref[:, r, :] = past[r + steps]
    for t in range(steps):
        cn_ref[:, hist - steps + t, :] = a[t * nb:(t + 1) * nb]


def conv_sample(z, row0, steps, nb, cache, params, layer, ybuf, cn_all):
    c = CONV_DIM_
    rows = steps * nb
    hist = CONV_WIDTH - 1
    p_arrs, p_specs = zip(*[_layer_rows(p, layer) for p in params])
    cache_spec = pl.BlockSpec((None, nb, hist, c), lambda i: (layer, 0, 0, 0))
    any_spec = pl.BlockSpec(memory_space=pl.ANY)
    return pl.pallas_call(
        _skip_ref(_skip_ref(functools.partial(_conv_sample_kernel, steps=steps, nb=nb), 6), 6),
        grid=(1,),
        in_specs=[pl.BlockSpec((rows, 2 * c), lambda i: (row0 // rows, 0)), cache_spec, *p_specs, any_spec, any_spec],
        out_specs=[pl.BlockSpec((None, rows, c), lambda i: (0, row0 // rows, 0)), cache_spec],
        out_shape=[jax.ShapeDtypeStruct(ybuf.shape, ybuf.dtype), jax.ShapeDtypeStruct(cn_all.shape, F32)],
        input_output_aliases={6: 0, 7: 1},
        scratch_shapes=[pltpu.VMEM((hist, nb, c), F32)],
        compiler_params=_cp(("arbitrary",), 40),
        name="conv_sample",
    )(z, cache, *p_arrs, ybuf, cn_all)


def _t5_bucket_np(dist):
    n = np.maximum(dist, 0)
    max_exact = N_BUCKETS // 2
    nf = np.maximum(n, 1).astype(np.float32)
    large = max_exact + (np.log(nf / np.float32(max_exact)) / np.float32(math.log(WINDOW / max_exact))
                         * np.float32(N_BUCKETS - max_exact)).astype(np.int32)
    large = np.minimum(large, N_BUCKETS - 1)
    return np.where(n < max_exact, n, large)


def _bias_kernel(t5_ref, bucket_ref, o_ref):
    h = pl.program_id(0)
    bucket = bucket_ref[...]
    acc = jnp.full(bucket.shape, NEG_INF, F32)
    for b in range(N_BUCKETS):
        acc = jnp.where(bucket == b, t5_ref[b, h], acc)
    o_ref[...] = acc


def band_bias(t5_bias, dist, mask):
    bucket = np.where(mask, _t5_bucket_np(dist), -1).astype(np.int32)
    nq, nk = bucket.shape
    return pl.pallas_call(
        _bias_kernel,
        grid=(SWA_HEADS,),
        in_specs=[pl.BlockSpec(memory_space=pltpu.SMEM), pl.BlockSpec((nq, nk), lambda h: (0, 0))],
        out_specs=pl.BlockSpec((None, nq, nk), lambda h: (h, 0, 0)),
        out_shape=jax.ShapeDtypeStruct((SWA_HEADS, nq, nk), F32),
        compiler_params=_cp(("arbitrary",), 16),
        name="band_bias",
    )(t5_bias, jnp.asarray(bucket))


def _softmax_sink_pv(s, sink, v):
    m = jnp.maximum(jnp.max(s, axis=-1, keepdims=True), sink)
    p = jnp.exp(s - m)
    den = jnp.sum(p, axis=-1, keepdims=True) + jnp.exp(sink - m)
    return jnp.dot(p.astype(BF16), v, preferred_element_type=F32) / den


def _swa_prompt_kernel(sink_ref, q_ref, kp_ref, kc_ref, vp_ref, vc_ref, bias_ref, o_ref, *, layer):
    k = jnp.concatenate([kp_ref[...], kc_ref[...]], axis=0)
    v = jnp.concatenate([vp_ref[...], vc_ref[...]], axis=0)
    q = q_ref[...] * SWA_SCALE
    for hp in range(SWA_HEADS // 2):
        outs = []
        for h in (2 * hp, 2 * hp + 1):
            g = h // SWA_REP
            hs = slice(h * SWA_HEAD_DIM, (h + 1) * SWA_HEAD_DIM)
            gs = slice(g * SWA_HEAD_DIM, (g + 1) * SWA_HEAD_DIM)
            s = lax.dot_general(q[:, hs].astype(BF16), k[:, gs].astype(BF16), (((1,), (1,)), ((), ())),
                                preferred_element_type=F32) + bias_ref[h]
            outs.append(_softmax_sink_pv(s, sink_ref[layer, h], v[:, gs].astype(BF16)))
        o_ref[:, 2 * hp * SWA_HEAD_DIM:(2 * hp + 2) * SWA_HEAD_DIM] = (
            jnp.concatenate(outs, axis=1).astype(o_ref.dtype))


def swa_prompt(z, nb, seq, sinks, layer, bias, ybuf):
    blk = WINDOW
    nj = seq // blk
    qw = SWA_HEADS * SWA_HEAD_DIM
    kw = SWA_KV_HEADS * SWA_HEAD_DIM
    cur = lambda col: (lambda b, j: (b * nj + j, col))
    prev = lambda col: (lambda b, j: (b * nj + jnp.maximum(j - 1, 0), col))
    return pl.pallas_call(
        _skip_ref(functools.partial(_swa_prompt_kernel, layer=layer), 7),
        grid=(nb, nj),
        in_specs=[pl.BlockSpec(memory_space=pltpu.SMEM),
                  pl.BlockSpec((blk, qw), cur(COL_Q // qw)),
                  pl.BlockSpec((blk, kw), prev(COL_K // kw)), pl.BlockSpec((blk, kw), cur(COL_K // kw)),
                  pl.BlockSpec((blk, kw), prev(COL_V // kw)), pl.BlockSpec((blk, kw), cur(COL_V // kw)),
                  pl.BlockSpec((None, SWA_HEADS, blk, 2 * blk), lambda b, j: (jnp.minimum(j, 1), 0, 0, 0)),
                  pl.BlockSpec(memory_space=pl.ANY)],
        out_specs=pl.BlockSpec((None, blk, qw), lambda b, j: (1, b * nj + j, 0)),
        out_shape=jax.ShapeDtypeStruct(ybuf.shape, ybuf.dtype),
        input_output_aliases={7: 0},
        compiler_params=_cp(("parallel", "arbitrary"), 32),
        name="swa_prompt",
    )(sinks, z, z, z, z, z, bias, ybuf)


def _swa_sample_kernel(q_ref, kc_ref, vc_ref, kn_ref, vn_ref, bias_ref, sink_ref, o_ref, wk_ref, wv_ref):
    scale = SWA_HEAD_DIM ** -0.5
    nbat, win, ng, hd = kc_ref.shape
    new_rows = kn_ref.shape[1]
    for bi in range(nbat):
        kk = jnp.concatenate([kc_ref[bi].reshape(win * ng, hd), kn_ref[bi]], axis=0)
        vv = jnp.concatenate([vc_ref[bi].reshape(win * ng, hd), vn_ref[bi]], axis=0)
        wk_ref[bi] = kk[new_rows:, :].reshape(win, ng, hd)
        wv_ref[bi] = vv[new_rows:, :].reshape(win, ng, hd)
        s = lax.dot_general(q_ref[bi].astype(BF16), kk.astype(BF16), (((1,), (1,)), ((), ())),
                            preferred_element_type=F32) * scale + bias_ref[...]
        o_ref[bi] = _softmax_sink_pv(s, sink_ref[...], vv.astype(BF16))


def swa_sample(q, k_new, v_new, cache_k, cache_v, layer, bias, sink_col, bb, wk_all, wv_all):
    nb, rows, hd = q.shape
    new_rows = k_new.shape[1]
    _, _, win, ng, _ = cache_k.shape
    nk = win * ng + new_rows
    cache_spec = pl.BlockSpec((None, bb, win, ng, hd), lambda b: (layer, b, 0, 0, 0))
    new_spec = pl.BlockSpec((bb, new_rows, hd), lambda b: (b, 0, 0))
    qo_spec = pl.BlockSpec((bb, rows, hd), lambda b: (b, 0, 0))
    any_spec = pl.BlockSpec(memory_space=pl.ANY)
    win_shape = jax.ShapeDtypeStruct(wk_all.shape, F32)
    return pl.pallas_call(
        _skip_ref(_skip_ref(_swa_sample_kernel, 7), 7),
        grid=(nb // bb,),
        in_specs=[qo_spec, cache_spec, cache_spec, new_spec, new_spec,
                  pl.BlockSpec((rows, nk), lambda b: (0, 0)), pl.BlockSpec((None, rows, 1), lambda b: (layer, 0, 0)),
                  any_spec, any_spec],
        out_specs=[qo_spec, cache_spec, cache_spec],
        out_shape=[jax.ShapeDtypeStruct(q.shape, F32), win_shape, win_shape],
        input_output_aliases={7: 1, 8: 2},
        compiler_params=_cp(("parallel",), 32),
        name="swa_sample",
    )(q, cache_k, cache_v, k_new, v_new, bias, sink_col, wk_all, wv_all)


def _attend(q, mk, mv, allowed=None):
    s = lax.dot_general(q.astype(BF16), mk.astype(BF16), (((1,), (1,)), ((), ())),
                        preferred_element_type=F32) * (MEM_HEAD_DIM ** -0.5)
    if allowed is not None:
        s = jnp.where(allowed, s, NEG_INF)
    m = jnp.max(s, axis=-1, keepdims=True)
    p = jnp.exp(s - m)
    den = jnp.sum(p, axis=-1, keepdims=True)
    return jnp.dot(p.astype(BF16), mv.astype(BF16), preferred_element_type=F32) / den


def _mem_attn_kernel(q0_ref, q1_ref, q2_ref, q3_ref, mk_ref, mv_ref, o_ref):
    for h, q_ref in enumerate((q0_ref, q1_ref, q2_ref, q3_ref)):
        hs = slice(h * MEM_HEAD_DIM, (h + 1) * MEM_HEAD_DIM)
        o_ref[:, hs] = _attend(q_ref[...], mk_ref[:, hs], mv_ref[:, hs]).astype(o_ref.dtype)


def _mem_attn_sample_kernel(q_ref, mk_ref, mv_ref, o_ref):
    nbat, rows, hd = q_ref.shape
    nkeys = mk_ref.shape[1] * MEM_HEADS
    same_head = (lax.broadcasted_iota(jnp.int32, (rows, nkeys), 0) % MEM_HEADS
                 == lax.broadcasted_iota(jnp.int32, (rows, nkeys), 1) % MEM_HEADS)
    for bi in range(nbat):
        mk = mk_ref[bi].reshape(nkeys, hd)
        mv = mv_ref[bi].reshape(nkeys, hd)
        o_ref[bi] = _attend(q_ref[bi], mk, mv, same_head).astype(o_ref.dtype)


def mem_attn_prompt(z, nb, seq, kv, tq, ybuf):
    nt = seq // tq
    mlen = kv.shape[1]
    hd = MEM_HEAD_DIM
    q_specs = [pl.BlockSpec((tq, hd), functools.partial(lambda b, t, h: (b * nt + t, COL_XQ // hd + h), h=h))
               for h in range(MEM_HEADS)]
    return pl.pallas_call(
        _skip_ref(_mem_attn_kernel, 6),
        grid=(nb, nt),
        in_specs=q_specs + [pl.BlockSpec((None, mlen, BRANCH_DIM), lambda b, t: (b, 0, 0)),
                            pl.BlockSpec((None, mlen, BRANCH_DIM), lambda b, t: (b, 0, 1)),
                            pl.BlockSpec(memory_space=pl.ANY)],
        out_specs=pl.BlockSpec((None, tq, BRANCH_DIM), lambda b, t: (3, b * nt + t, 0)),
        out_shape=jax.ShapeDtypeStruct(ybuf.shape, ybuf.dtype),
        input_output_aliases={6: 0},
        compiler_params=_cp(("parallel", "arbitrary"), 32),
        name="mem_attn_prompt",
    )(z, z, z, z, kv, kv, ybuf)


def mem_attn_sample(q, cache_k, cache_v, layer, bb):
    nb, rows, hd = q.shape
    mlen = cache_k.shape[2]
    qo_spec = pl.BlockSpec((bb, rows, hd), lambda b: (b, 0, 0))
    cache_spec = pl.BlockSpec((None, bb, mlen, MEM_HEADS, hd), lambda b: (layer, b, 0, 0, 0))
    return pl.pallas_call(
        _mem_attn_sample_kernel,
        grid=(nb // bb,),
        in_specs=[qo_spec, cache_spec, cache_spec],
        out_specs=qo_spec,
        out_shape=jax.ShapeDtypeStruct(q.shape, BF16),
        compiler_params=_cp(("parallel",), 40),
        name="mem_attn_sample",
    )(q, cache_k, cache_v)


def _ssm_param_kernel(are_ref, aim_ref, ldt_ref, bre_ref, bim_ref, abr_ref, abi_ref, bbr_ref, bbi_ref):
    dt = jnp.exp(ldt_ref[...])
    ar, ai = are_ref[...], aim_ref[...]
    mag = jnp.exp(dt * ar)
    abr, abi = mag * jnp.cos(dt * ai), mag * jnp.sin(dt * ai)
    den = ar * ar + ai * ai
    nr, ni = abr - 1.0, abi
    fre, fim = (nr * ar + ni * ai) / den, (ni * ar - nr * ai) / den
    abr_ref[...] = abr
    abi_ref[...] = abi
    for c in range(SSM_GROUP):
        br, bi = bre_ref[c], bim_ref[c]
        bbr_ref[c] = fre * br - fim * bi
        bbi_ref[c] = fre * bi + fim * br


def ssm_params(a_re, a_im, log_dt, b_re_t, b_im_t):
    d, g, n = a_re.shape
    c = b_re_t.shape[1]
    gn = pl.BlockSpec((None, g, n), lambda l: (l, 0, 0))
    cgn = pl.BlockSpec((None, c, g, n), lambda l: (l, 0, 0, 0))
    return pl.pallas_call(
        _ssm_param_kernel,
        grid=(d,),
        in_specs=[gn, gn, pl.BlockSpec((None, g, 1), lambda l: (l, 0, 0)), cgn, cgn],
        out_specs=[gn, gn, cgn, cgn],
        out_shape=[jax.ShapeDtypeStruct((d, g, n), F32)] * 2 + [jax.ShapeDtypeStruct((d, c, g, n), F32)] * 2,
        compiler_params=_cp(("arbitrary",), 16),
        name="ssm_params",
    )(a_re, a_im, log_dt.reshape(d, g, 1), b_re_t, b_im_t)


def _cmul_add(ar, ai, hr, hi, br, bi):
    return ar * hr - ai * hi + br, ar * hi + ai * hr + bi


def _ssm_out(u, sre, sim, cre_ref, cim_ref, d_ref):
    y = (jnp.dot(sre[...].astype(BF16), cre_ref[...], preferred_element_type=F32)
         - jnp.dot(sim[...].astype(BF16), cim_ref[...], preferred_element_type=F32)
         + d_ref[...] * u)
    return jax.nn.gelu(y)


LANES = 128
PIECE_PITCH = 264


def _ssm_prompt_kernel(u_ref, bre_ref, bim_ref, cre_ref, cim_ref, ar_ref, ai_ref, d_ref,
                       y_ref, hr_ref, hi_ref, sre, sim, upad, uperm, ypad, *, clen):
    np_ = SCAN_LANES
    w = sre.shape[1]
    slabs = [slice(s * LANES, (s + 1) * LANES) for s in range(u_ref.shape[1] // LANES)]

    def rows(t):
        return pl.ds(pl.multiple_of(t * np_, np_), np_)

    def piece_rows(t):
        return pl.ds(t, np_, stride=PIECE_PITCH)

    for s, ls in enumerate(slabs):
        for p in range(np_):
            upad[s, p * PIECE_PITCH:p * PIECE_PITCH + clen, :] = u_ref[p * clen:(p + 1) * clen, ls]

    def gather(t, carry):
        for s, ls in enumerate(slabs):
            uperm[rows(t), ls] = upad[s, piece_rows(t), :]
        return carry

    lax.fori_loop(0, clen, gather, 0, unroll=SCAN_UNROLL)
    u = uperm[...]
    ub = u.astype(BF16)
    sre[...] = jnp.dot(ub, bre_ref[...], preferred_element_type=F32)
    sim[...] = jnp.dot(ub, bim_ref[...], preferred_element_type=F32)
    ar1, ai1 = ar_ref[...], ai_ref[...]
    ar = jnp.broadcast_to(ar1, (np_, w))
    ai = jnp.broadcast_to(ai1, (np_, w))

    def local_step(t, carry):
        return _cmul_add(ar, ai, carry[0], carry[1], sre[rows(t), :], sim[rows(t), :])

    zero = jnp.zeros((np_, w), F32)
    fr, fi = lax.fori_loop(0, clen, local_step, (zero, zero), unroll=SCAN_UNROLL)

    pr, pi = ar1, ai1
    for _ in range(int(math.log2(clen))):
        pr, pi = pr * pr - pi * pi, 2.0 * pr * pi
    row = lax.broadcasted_iota(jnp.int32, (np_, w), 0)
    cr = jnp.zeros((1, w), F32)
    ci = jnp.zeros((1, w), F32)
    hr0, hi0 = zero, zero
    for p in range(1, np_):
        cr, ci = _cmul_add(pr, pi, cr, ci, fr[p - 1:p], fi[p - 1:p])
        hr0 = jnp.where(row == p, cr, hr0)
        hi0 = jnp.where(row == p, ci, hi0)
    fin_r, fin_i = _cmul_add(pr, pi, cr, ci, fr[np_ - 1:np_], fi[np_ - 1:np_])
    hr_ref[...] = fin_r
    hi_ref[...] = fin_i

    def full_step(t, carry):
        nr, ni = _cmul_add(ar, ai, carry[0], carry[1], sre[rows(t), :], sim[rows(t), :])
        sre[rows(t), :] = nr
        sim[rows(t), :] = ni
        return nr, ni

    lax.fori_loop(0, clen, full_step, (hr0, hi0), unroll=SCAN_UNROLL)
    uperm[...] = _ssm_out(u, sre, sim, cre_ref, cim_ref, d_ref)

    def scatter(t, carry):
        for s, ls in enumerate(slabs):
            ypad[s, piece_rows(t), :] = uperm[rows(t), ls]
        return carry

    lax.fori_loop(0, clen, scatter, 0, unroll=SCAN_UNROLL)
    for s, ls in enumerate(slabs):
        for p in range(np_):
            y_ref[p * clen:(p + 1) * clen, ls] = ypad[s, p * PIECE_PITCH:p * PIECE_PITCH + clen, :]


def _ssm_specs(layer, kdim):
    kmap = lambda *idx: (layer, idx[kdim], 0, 0)
    shapes = [(SSM_UW, SSM_CW), (SSM_UW, SSM_CW), (SSM_CW, SSM_UW), (SSM_CW, SSM_UW), (1, SSM_CW), (1, SSM_CW),
              (1, SSM_UW)]
    return [pl.BlockSpec((None, None) + s, kmap) for s in shapes]


def ssm_prompt(z, nb, seq, mats, layer):
    clen = seq // SCAN_LANES
    nstate = SSM_GROUPS * SSM_STATE
    st = pl.BlockSpec((None, 1, SSM_CW), lambda b, k: (b, 0, k))
    pad_shape = (SSM_UW // LANES, SCAN_LANES * PIECE_PITCH, LANES)
    return pl.pallas_call(
        functools.partial(_ssm_prompt_kernel, clen=clen),
        grid=(nb, SSM_CHUNKS),
        in_specs=[pl.BlockSpec((seq, SSM_UW), lambda b, k: (b, COL_U // SSM_UW + k))] + _ssm_specs(layer, 1),
        out_specs=[pl.BlockSpec((seq, SSM_UW), lambda b, k: (b, k)), st, st],
        out_shape=[jax.ShapeDtypeStruct((nb * seq, BRANCH_DIM), F32),
                   jax.ShapeDtypeStruct((nb, 1, nstate), F32), jax.ShapeDtypeStruct((nb, 1, nstate), F32)],
        scratch_shapes=[pltpu.VMEM((seq, SSM_CW), F32), pltpu.VMEM((seq, SSM_CW), F32),
                        pltpu.VMEM(pad_shape, F32), pltpu.VMEM((seq, SSM_UW), F32), pltpu.VMEM(pad_shape, F32)],
        compiler_params=_cp(("parallel", "parallel"), 48),
        name="ssm_prompt",
    )(z, *mats)


def _ssm_sample_kernel(u_ref, bre_ref, bim_ref, cre_ref, cim_ref, ar_ref, ai_ref, d_ref, h0r_ref, h0i_ref,
                       y_ref, hr_ref, hi_ref, sre, sim, *, steps, nb):
    w = sre.shape[1]
    u = u_ref[...]
    ub = u.astype(BF16)
    bur = jnp.dot(ub, bre_ref[...], preferred_element_type=F32)
    bui = jnp.dot(ub, bim_ref[...], preferred_element_type=F32)
    ar = jnp.broadcast_to(ar_ref[...], (nb, w))
    ai = jnp.broadcast_to(ai_ref[...], (nb, w))
    hr, hi = h0r_ref[...], h0i_ref[...]
    for t in range(steps):
        rs = slice(t * nb, (t + 1) * nb)
        hr, hi = _cmul_add(ar, ai, hr, hi, bur[rs], bui[rs])
        sre[rs, :] = hr
        sim[rs, :] = hi
    hr_ref[...] = hr
    hi_ref[...] = hi
    y_ref[...] = _ssm_out(u, sre, sim, cre_ref, cim_ref, d_ref)


def ssm_sample(z, row0, steps, nb, mats, h0_re, h0_im, layer):
    rows = steps * nb
    nstate = SSM_GROUPS * SSM_STATE
    st_in = pl.BlockSpec((None, nb, SSM_CW), lambda k: (layer, 0, k))
    st_out = pl.BlockSpec((nb, SSM_CW), lambda k: (0, k))
    return pl.pallas_call(
        functools.partial(_ssm_sample_kernel, steps=steps, nb=nb),
        grid=(SSM_CHUNKS,),
        in_specs=[pl.BlockSpec((rows, SSM_UW), lambda k: (row0 // rows, COL_U // SSM_UW + k))]
                 + _ssm_specs(layer, 0) + [st_in, st_in],
        out_specs=[pl.BlockSpec((rows, SSM_UW), lambda k: (0, k)), st_out, st_out],
        out_shape=[jax.ShapeDtypeStruct((rows, BRANCH_DIM), F32),
                   jax.ShapeDtypeStruct((nb, nstate), F32), jax.ShapeDtypeStruct((nb, nstate), F32)],
        scratch_shapes=[pltpu.VMEM((rows, SSM_CW), F32), pltpu.VMEM((rows, SSM_CW), F32)],
        compiler_params=_cp(("parallel",), 32),
        name="ssm_sample",
    )(z, *mats, h0_re, h0_im)


def _block_diag(blocks, nblk):
    rows, c = blocks.shape[-2:]
    tiled = jnp.tile(blocks, (1,) * (blocks.ndim - 1) + (nblk,))
    row_blk = lax.broadcasted_iota(jnp.int32, (rows, nblk * c), 0) // (rows // nblk)
    col_blk = lax.broadcasted_iota(jnp.int32, (rows, nblk * c), 1) // c
    return jnp.where(row_blk == col_blk, tiled, 0.0)


def kernel(x_prompt, x_sample, cache_conv, cache_win_k, cache_win_v, state_ssm_re, state_ssm_im, cache_mem_k, cache_mem_v, mem_prompt, t5_bias, norm_mix_pre, norm_mix_post, norm_ffn_pre, norm_ffn_post, norm_mem, w_in, conv_w, conv_b, conv_ln_g, conv_ln_b, attn_sinks, ssm_a_re, ssm_a_im, ssm_log_dt, ssm_b_re, ssm_b_im, ssm_c_re, ssm_c_im, ssm_d, ssm_w_glu, w_mem_kv, w_branch, w_out, w_ffn_in, w_ffn_out):
    bp, seq, d = x_prompt.shape
    bs, steps, _ = x_sample.shape
    rows_p = bp * seq
    rows_s = bs * steps
    rows = rows_p + rows_s
    mlen = mem_prompt.shape[1]
    kw = SWA_KV_HEADS * SWA_HEAD_DIM
    gpc = SSM_GROUPS // SSM_CHUNKS

    tm = rows // 5
    tm_merge = rows // 8
    tm_ffn_out = rows // 8
    tr = 320

    qi = np.arange(WINDOW)
    ki = np.arange(2 * WINDOW) - WINDOW
    dist_p = qi[:, None] - ki[None, :]
    bias_p = band_bias(t5_bias, dist_p, (dist_p >= 0) & (dist_p < WINDOW))
    bias_p = jnp.stack([jnp.where(jnp.asarray(ki < 0), NEG_INF, bias_p), bias_p])
    nk_s = WINDOW + steps
    dist_s = (WINDOW + np.arange(SUBLANES))[:, None] - np.arange(nk_s)[None, :]
    bias_s = band_bias(t5_bias, dist_s, (dist_s >= 0) & (dist_s < WINDOW))[:, :steps]
    bias_s = bias_s.reshape(SWA_KV_HEADS, SWA_REP, steps, nk_s).transpose(0, 2, 1, 3)
    same_group = jnp.eye(SWA_KV_HEADS, dtype=bool)[:, None, None, None, :]
    bias_s = jnp.where(same_group, bias_s[..., None], NEG_INF)
    bias_s = bias_s.reshape(SWA_HEADS * steps, nk_s * SWA_KV_HEADS)

    abar_re, abar_im, bbar_re, bbar_im = ssm_params(
        ssm_a_re, ssm_a_im, ssm_log_dt, ssm_b_re.transpose(0, 3, 1, 2), ssm_b_im.transpose(0, 3, 1, 2))

    def in_mat(x):
        x = x.reshape(DEPTH, SSM_GROUP, SSM_CHUNKS, gpc, SSM_STATE).transpose(0, 2, 3, 1, 4)
        return _block_diag(x.reshape(DEPTH, SSM_CHUNKS, SSM_UW, SSM_STATE), gpc).astype(BF16)

    def out_mat(x):
        x = x.reshape(DEPTH, SSM_CHUNKS, gpc, SSM_GROUP, SSM_STATE).transpose(0, 1, 2, 4, 3)
        return _block_diag(x.reshape(DEPTH, SSM_CHUNKS, SSM_CW, SSM_GROUP), gpc).astype(BF16)

    bmat_re, bmat_im = in_mat(bbar_re), in_mat(bbar_im)
    cmat_re, cmat_im = out_mat(ssm_c_re), out_mat(ssm_c_im)
    abar_re = abar_re.reshape(DEPTH, SSM_CHUNKS, 1, SSM_CW)
    abar_im = abar_im.reshape(DEPTH, SSM_CHUNKS, 1, SSM_CW)
    dvec = ssm_d.reshape(DEPTH, SSM_CHUNKS, 1, SSM_UW)

    h0_re = state_ssm_re.reshape(DEPTH, bs, -1)
    h0_im = state_ssm_im.reshape(DEPTH, bs, -1)
    mem_rows = mem_prompt.reshape(bp * mlen, d)

    x_p0 = x_prompt.reshape(rows_p, d)
    x_s0 = x_sample.transpose(1, 0, 2).reshape(rows_s, d)
    x = None
    xn = rmsnorm_stacked(x_p0, x_s0, (norm_mix_pre, 0))

    conv_params = (conv_w, conv_b, conv_ln_g, conv_ln_b)
    mats = (bmat_re, bmat_im, cmat_re, cmat_im, abar_re, abar_im, dvec)
    sink_cols = jnp.tile(attn_sinks.reshape(DEPTH, SWA_KV_HEADS, 1, SWA_REP), (1, 1, steps, 1))
    sink_cols = sink_cols.reshape(DEPTH, SWA_HEADS * steps, 1)

    ybuf = jnp.zeros((4, rows, BRANCH_DIM), BF16)
    conv_s_all = jnp.zeros(cache_conv.shape, F32)
    wk_s_all = jnp.zeros(cache_win_k.shape, F32)
    wv_s_all = jnp.zeros(cache_win_v.shape, F32)
    outs = [[] for _ in range(12)]
    for l in range(DEPTH):
        z = matmul(xn, w_in, l, F32, tm, 512, n=COL_GATE)
        z_s = lax.slice(z, (rows_p, 0), (rows, COL_GATE))

        ybuf, conv_p = conv_prompt(z, bp, seq, conv_params, l, 256, ybuf)
        ybuf, conv_s_all = conv_sample(z, rows_p, steps, bs, cache_conv, conv_params, l, ybuf, conv_s_all)

        ybuf = swa_prompt(z, bp, seq, attn_sinks, l, bias_p, ybuf)
        q_s = z_s[:, COL_Q:COL_K].reshape(steps, bs, SWA_KV_HEADS, SWA_REP, SWA_HEAD_DIM)
        q_s = q_s.transpose(1, 2, 0, 3, 4).reshape(bs, SWA_HEADS * steps, SWA_HEAD_DIM)
        k_s = z_s[:, COL_K:COL_V].reshape(steps, bs, SWA_KV_HEADS, SWA_HEAD_DIM).transpose(1, 0, 2, 3)
        v_s = z_s[:, COL_V:COL_U].reshape(steps, bs, SWA_KV_HEADS, SWA_HEAD_DIM).transpose(1, 0, 2, 3)
        k_s = k_s.reshape(bs, steps * SWA_KV_HEADS, SWA_HEAD_DIM)
        v_s = v_s.reshape(bs, steps * SWA_KV_HEADS, SWA_HEAD_DIM)
        ob_s, wk_s_all, wv_s_all = swa_sample(q_s, k_s, v_s, cache_win_k, cache_win_v, l, bias_s, sink_cols, 4,
                                              wk_s_all, wv_s_all)
        yb_s = ob_s.reshape(bs, SWA_KV_HEADS, steps, SWA_REP, SWA_HEAD_DIM).transpose(2, 0, 1, 3, 4)
        yb_s = yb_s.reshape(rows_s, BRANCH_DIM).astype(BF16)

        yg_p, hr_p, hi_p = ssm_prompt(z, bp, seq, mats, l)
        ybuf = glu(yg_p, ssm_w_glu, l, 1024, ybuf, 2, 0)
        yg_s, hr_s, hi_s = ssm_sample(z, rows_p, steps, bs, mats, h0_re, h0_im, l)
        ybuf = glu(yg_s, ssm_w_glu, l, rows_s, ybuf, 2, rows_p)

        kv = matmul(rmsnorm_bf16(mem_rows, (norm_mem, l), 256), w_mem_kv, l, F32, bp * mlen, 512)
        kv = kv.reshape(bp, mlen, 2 * BRANCH_DIM)
        ybuf = mem_attn_prompt(z, bp, seq, kv, 512, ybuf)
        xq_s = z_s[:, COL_XQ:COL_GATE].reshape(steps, bs, MEM_HEADS, MEM_HEAD_DIM).transpose(1, 0, 2, 3)
        yx_s = mem_attn_sample(xq_s.reshape(bs, steps * MEM_HEADS, MEM_HEAD_DIM), cache_mem_k, cache_mem_v, l, 2)
        yx_s = yx_s.reshape(bs, steps, BRANCH_DIM).transpose(1, 0, 2).reshape(rows_s, BRANCH_DIM)

        ybuf = lax.dynamic_update_slice(ybuf, yb_s[None], (1, rows_p, 0))
        ybuf = lax.dynamic_update_slice(ybuf, yx_s[None], (3, rows_p, 0))
        merged = merge_branches(xn, ybuf, w_in, w_branch, l, tm_merge, 512)
        mix = matmul(merged, w_out, l, BF16, tm, 512)
        if l == 0:
            x, hn = resid_norm_stacked(x_p0, x_s0, mix, (norm_mix_post, l), (norm_ffn_pre, l))
        else:
            x, hn = resid_norm(x, mix, (norm_mix_post, l), (norm_ffn_pre, l), tr)

        hid = ffn_in(hn, w_ffn_in, l, tm, 256)
        f = matmul(hid, w_ffn_out, l, BF16, tm_ffn_out, 256, single_a=True)
        if l + 1 < DEPTH:
            x, xn = resid_norm(x, f, (norm_ffn_post, l), (norm_mix_pre, l + 1), tr)
        else:
            y_p, y_s = resid_split(x, f, (norm_ffn_post, l), rows_s)

        kv_win = jnp.stack([lax.slice(z, ((b + 1) * seq - WINDOW, COL_K), ((b + 1) * seq, COL_U))
                            for b in range(bp)])
        new = (conv_p, None,
               kv_win[..., :kw].reshape(bp, WINDOW, SWA_KV_HEADS, SWA_HEAD_DIM),
               kv_win[..., kw:].reshape(bp, WINDOW, SWA_KV_HEADS, SWA_HEAD_DIM),
               None, None,
               hr_p.reshape(bp, SSM_GROUPS, SSM_STATE), hi_p.reshape(bp, SSM_GROUPS, SSM_STATE),
               hr_s.reshape(bs, SSM_GROUPS, SSM_STATE), hi_s.reshape(bs, SSM_GROUPS, SSM_STATE),
               kv[..., :BRANCH_DIM].reshape(bp, mlen, MEM_HEADS, MEM_HEAD_DIM),
               kv[..., BRANCH_DIM:].reshape(bp, mlen, MEM_HEADS, MEM_HEAD_DIM))
        for acc, val in zip(outs, new):
            acc.append(val)

    y_prompt = y_p.reshape(bp, seq, d)
    y_sample = y_s.reshape(steps, bs, d).transpose(1, 0, 2)
    stacked = [jnp.stack(o) if o[0] is not None else None for o in outs]
    stacked[1], stacked[4], stacked[5] = conv_s_all, wk_s_all, wv_s_all
    return (y_prompt, y_sample) + tuple(stacked)
```
